```python
import math
import jax, jax.numpy as jnp
from jax import lax
import numpy as np

D_MODEL = 2048
BATCH = 2
SEQ = 4096
DEPTH = 2
DEC_BATCH = 8
DEC_SEQ = 1
PAST_LEN = 16384
PAGE_SIZE = 128

N_HEADS = 16
HEAD_DIM = D_MODEL // N_HEADS
N_KV_HEADS = 4
GQA_GROUP = N_HEADS // N_KV_HEADS
Q_DIM = N_HEADS * HEAD_DIM
KV_DIM = N_KV_HEADS * HEAD_DIM
ATTN_SCALE = HEAD_DIM ** -0.5
MOBA_BLOCK = 256
MOBA_TOPK = 3
MOBA_Q_CHUNK = 16
IDX_HEADS = 16
IDX_DIM = 64
IDX_SCALE = IDX_DIM ** -0.5
IDX_W_SCALE = IDX_HEADS ** -0.5
DSA_TOPK = 256
DSA_Q_CHUNK = 128
REL_BUCKETS = 32
REL_MAX_DIST = 128
N_GROUPS = 4
EXPERTS_PER_GROUP = 8
N_EXPERTS = N_GROUPS * EXPERTS_PER_GROUP
D_EXPERT = 768
EXPERT_TOPK = 2
MOE_MAX_BLOCK_ROWS = 128
N_MIXERS = 2
N_LAYERS_A = (DEPTH + 1) // 2
N_LAYERS_B = DEPTH // 2
IN_DIM_A = Q_DIM + 2 * KV_DIM
IN_DIM_B = IN_DIM_A + IDX_HEADS * IDX_DIM + IDX_DIM + IDX_HEADS
RMS_EPS = 1e-6
NEG = -1e30

kernel_name = 'moba_dsa_hier_moe_adaln_step'


def rms_norm(x, g):
    xf = x.astype(jnp.float32)
    y = xf * lax.rsqrt(jnp.mean(xf * xf, axis=-1, keepdims=True) + RMS_EPS)
    return (y * g.astype(jnp.float32)).astype(x.dtype)


def rel_bucket(dist):
    n = jnp.maximum(dist, 0)
    exact = REL_BUCKETS // 2
    logr = jnp.log(jnp.maximum(n, 1).astype(jnp.float32) / exact) / math.log(REL_MAX_DIST / exact)
    large = jnp.minimum(exact + (logr * (REL_BUCKETS - exact)).astype(jnp.int32), REL_BUCKETS - 1)
    return jnp.where(n < exact, n, large)


def masked_softmax(logits, bias, valid):
    z = jnp.where(valid, logits.astype(jnp.float32) + bias.astype(jnp.float32), NEG)
    return jax.nn.softmax(z, axis=-1)


def adaln(c, w, b):
    m = jax.nn.silu(c) @ w + b
    return [p[:, None, :] for p in jnp.split(m, 6, axis=-1)]


def contiguous_rows(arr, pos, head_idx=None):
    b = jnp.arange(arr.shape[0]).reshape((-1,) + (1,) * (pos.ndim - 1))
    return arr[b, pos] if head_idx is None else arr[b, pos, head_idx]


def paged_rows(pool, new, page_table, past, pos, head_idx=None):
    b = jnp.arange(pos.shape[0]).reshape((-1,) + (1,) * (pos.ndim - 1))
    p_old = jnp.clip(pos, 0, past - 1)
    phys = page_table[b, p_old // PAGE_SIZE]
    off = p_old % PAGE_SIZE
    p_new = jnp.clip(pos - past, 0, new.shape[1] - 1)
    if head_idx is None:
        old_rows, new_rows = pool[phys, off], new[b, p_new]
    else:
        old_rows, new_rows = pool[phys, off, head_idx], new[b, p_new, head_idx]
    is_old = (pos < past).reshape(pos.shape + (1,) * (old_rows.ndim - pos.ndim))
    return jnp.where(is_old, old_rows.astype(new_rows.dtype), new_rows)


def project_heads(h, w_in, qn, kn):
    b, t = h.shape[:2]
    z = h @ w_in
    q = rms_norm(z[..., :Q_DIM].reshape(b, t, N_HEADS, HEAD_DIM), qn)
    k = rms_norm(z[..., Q_DIM:Q_DIM + KV_DIM].reshape(b, t, N_KV_HEADS, HEAD_DIM), kn)
    v = z[..., Q_DIM + KV_DIM:IN_DIM_A].reshape(b, t, N_KV_HEADS, HEAD_DIM)
    return q, k, v, z[..., IN_DIM_A:]


def split_indexer(r):
    b, t = r.shape[:2]
    qi = r[..., :IDX_HEADS * IDX_DIM].reshape(b, t, IDX_HEADS, IDX_DIM)
    ki = r[..., IDX_HEADS * IDX_DIM:IDX_HEADS * IDX_DIM + IDX_DIM]
    wi = r[..., IDX_HEADS * IDX_DIM + IDX_DIM:]
    return qi, ki, wi


def block_means(k_all):
    b, lp = k_all.shape[:2]
    kb = k_all.reshape(b, lp // MOBA_BLOCK, MOBA_BLOCK, N_KV_HEADS, HEAD_DIM).astype(jnp.float32)
    return jnp.mean(kb, axis=2).astype(k_all.dtype)


def moba_chunk(q, qpos, kbar, k_all, fetch_v, rel_table):
    b, t = q.shape[:2]
    nblk = kbar.shape[1]
    qg = q.reshape(b, t, N_KV_HEADS, GQA_GROUP, HEAD_DIM)
    gate = jnp.einsum('btkgd,bjkd->btkgj', qg, kbar).reshape(b, t, N_HEADS, nblk).astype(jnp.float32)
    own = qpos // MOBA_BLOCK
    past_ok = jnp.arange(nblk)[None, :] < own[:, None]
    gate = jnp.where(past_ok[None, :, None, :], gate, NEG)
    _, sel = lax.top_k(gate, min(MOBA_TOPK, nblk))
    sel_ok = sel < own[None, :, None, None]
    own_b = jnp.broadcast_to(own[None, :, None, None], (b, t, N_HEADS, 1))
    blocks = jnp.concatenate([sel, own_b], axis=-1)
    key_pos = blocks[..., None] * MOBA_BLOCK + jnp.arange(MOBA_BLOCK)
    qp = qpos[None, :, None, None, None]
    valid = jnp.concatenate([jnp.broadcast_to(sel_ok[..., None], sel.shape + (MOBA_BLOCK,)),
                             key_pos[..., -1:, :] <= qp], axis=-2)
    kv_head = (jnp.arange(N_HEADS) // GQA_GROUP)[None, None, :, None, None]
    kg = contiguous_rows(k_all, key_pos, kv_head)
    logits = jnp.einsum('bthd,bthnrd->bthnr', q, kg).astype(jnp.float32) * ATTN_SCALE
    bias = rel_table.T[jnp.arange(N_HEADS)[None, None, :, None, None], rel_bucket(qp - key_pos)]
    p = masked_softmax(logits.reshape(b, t, N_HEADS, -1), bias.reshape(b, t, N_HEADS, -1),
                       valid.reshape(b, t, N_HEADS, -1))
    vg = fetch_v(key_pos, kv_head)
    return jnp.einsum('bthnr,bthnrd->bthd', p.reshape(logits.shape).astype(vg.dtype), vg)


def moba_prompt(h, w_in, qn, kn, rel_table):
    q, k, v, _ = project_heads(h, w_in, qn, kn)
    b, s = h.shape[:2]
    lp = -(-s // MOBA_BLOCK) * MOBA_BLOCK
    pad = ((0, 0), (0, lp - s), (0, 0), (0, 0))
    k_all, v_all = jnp.pad(k, pad), jnp.pad(v, pad)
    kbar = block_means(k_all)
    nc = s // MOBA_Q_CHUNK
    qc = q.reshape(b, nc, MOBA_Q_CHUNK, N_HEADS, HEAD_DIM).swapaxes(0, 1)
    pc = jnp.arange(s, dtype=jnp.int32).reshape(nc, MOBA_Q_CHUNK)
    fetch_v = lambda pos, hidx: contiguous_rows(v_all, pos, hidx)
    out = lax.map(lambda a: moba_chunk(a[0], a[1], kbar, k_all, fetch_v, rel_table), (qc, pc))
    return out.swapaxes(0, 1).reshape(b, s, N_HEADS, HEAD_DIM), (k, v)


def moba_sample(h, w_in, qn, kn, rel_table, cache_k, cache_v, page_table):
    q, k, v, _ = project_heads(h, w_in, qn, kn)
    b, t = h.shape[:2]
    past = page_table.shape[1] * PAGE_SIZE
    lp = -(-(past + t) // MOBA_BLOCK) * MOBA_BLOCK
    k_past = cache_k[page_table].reshape(b, past, N_KV_HEADS, HEAD_DIM).astype(k.dtype)
    k_all = jnp.concatenate([k_past, k, jnp.zeros((b, lp - past - t, N_KV_HEADS, HEAD_DIM), k.dtype)], axis=1)
    kbar = block_means(k_all)
    qpos = past + jnp.arange(t, dtype=jnp.int32)
    fetch_v = lambda pos, hidx: paged_rows(cache_v, v, page_table, past, pos, hidx)
    return moba_chunk(q, qpos, kbar, k_all, fetch_v, rel_table), (k, v)


def dsa_chunk(q, qi, wi, qpos, kidx, n_sel, fetch_kv, rel_table):
    b, t = q.shape[:2]
    s = jnp.einsum('bthd,bsd->btsh', qi, kidx).astype(jnp.float32) * IDX_SCALE
    score = jnp.einsum('btsh,bth->bts', jax.nn.relu(s), wi.astype(jnp.float32)) * IDX_W_SCALE
    adm = jnp.arange(kidx.shape[1])[None, :] <= qpos[:, None]
    score = jnp.where(adm[None], score, NEG)
    _, idx = lax.top_k(score, n_sel)
    valid = idx <= qpos[None, :, None]
    kg, vg = fetch_kv(idx)
    qg = q.reshape(b, t, N_KV_HEADS, GQA_GROUP, HEAD_DIM)
    logits = jnp.einsum('btkgd,btnkd->btkgn', qg, kg).astype(jnp.float32) * ATTN_SCALE
    bias = rel_table[rel_bucket(qpos[None, :, None] - idx)]
    bias = jnp.moveaxis(bias, -1, 2).reshape(b, t, N_KV_HEADS, GQA_GROUP, n_sel)
    p = masked_softmax(logits, bias, valid[:, :, None, None, :])
    out = jnp.einsum('btkgn,btnkd->btkgd', p.astype(vg.dtype), vg)
    return out.reshape(b, t, N_HEADS, HEAD_DIM)


def dsa_prompt(h, w_in, qn, kn, rel_table):
    q, k, v, rest = project_heads(h, w_in, qn, kn)
    qi, ki, wi = split_indexer(rest)
    b, s = h.shape[:2]
    n_sel = min(DSA_TOPK, s // 4)
    nc = s // DSA_Q_CHUNK
    chunk = lambda a: a.reshape((b, nc, DSA_Q_CHUNK) + a.shape[2:]).swapaxes(0, 1)
    pc = jnp.arange(s, dtype=jnp.int32).reshape(nc, DSA_Q_CHUNK)
    fetch_kv = lambda idx: (contiguous_rows(k, idx), contiguous_rows(v, idx))
    out = lax.map(lambda a: dsa_chunk(a[0], a[1], a[2], a[3], ki, n_sel, fetch_kv, rel_table),
                  (chunk(q), chunk(qi), chunk(wi), pc))
    return out.swapaxes(0, 1).reshape(b, s, N_HEADS, HEAD_DIM), (k, v, ki)


def dsa_sample(h, w_in, qn, kn, rel_table, cache_k, cache_v, cache_kidx, page_table):
    q, k, v, rest = project_heads(h, w_in, qn, kn)
    qi, ki, wi = split_indexer(rest)
    b, t = h.shape[:2]
    past = page_table.shape[1] * PAGE_SIZE
    kidx_all = jnp.concatenate([cache_kidx[page_table].reshape(b, past, IDX_DIM).astype(ki.dtype), ki], axis=1)
    n_sel = min(DSA_TOPK, (past + t) // 4)
    qpos = past + jnp.arange(t, dtype=jnp.int32)
    fetch_kv = lambda idx: (paged_rows(cache_k, k, page_table, past, idx),
                            paged_rows(cache_v, v, page_table, past, idx))
    return dsa_chunk(q, qi, wi, qpos, kidx_all, n_sel, fetch_kv, rel_table), (k, v, ki)


def moe_block_rows(n_assign):
    per_expert = -(-n_assign // N_EXPERTS)
    rows = 8
    while rows < min(per_expert, MOE_MAX_BLOCK_ROWS):
        rows *= 2
    return rows


def hier_moe(h, w_grp, b_grp, w_rt, b_rt, w_gate, w_up, w_down):
    n, d = h.shape
    g_logit = (h @ w_grp).astype(jnp.float32) + b_grp.astype(jnp.float32)
    g_sel = jnp.argmax(g_logit, axis=-1)
    g_w = jnp.take_along_axis(jax.nn.softmax(g_logit, axis=-1), g_sel[:, None], axis=-1)
    e_logit = ((h @ w_rt).astype(jnp.float32) + b_rt.astype(jnp.float32)).reshape(n, N_GROUPS, EXPERTS_PER_GROUP)
    e_logit = jnp.take_along_axis(e_logit, g_sel[:, None, None], axis=1)[:, 0]
    top_v, top_i = lax.top_k(jax.nn.softmax(e_logit, axis=-1), EXPERT_TOPK)
    gate = g_w * top_v / jnp.sum(top_v, axis=-1, keepdims=True)
    eid = g_sel[:, None] * EXPERTS_PER_GROUP + top_i
    a = n * EXPERT_TOPK
    blk = moe_block_rows(a)
    flat_e = eid.reshape(a)
    flat_t = jnp.repeat(jnp.arange(n), EXPERT_TOPK)
    order = jnp.argsort(flat_e)
    se, st, sg = flat_e[order], flat_t[order], gate.reshape(a)[order]
    counts = jnp.bincount(flat_e, length=N_EXPERTS)
    padded = (counts + blk - 1) // blk * blk
    pend = jnp.cumsum(padded)
    pstart = pend - padded
    cstart = jnp.cumsum(counts) - counts
    dest = pstart[se] + jnp.arange(a) - cstart[se]
    rows = -(-(a + N_EXPERTS * (blk - 1)) // blk) * blk
    nblk = rows // blk
    buf = jnp.zeros((rows, d), h.dtype).at[dest].set(h[st])
    blk_e = jnp.minimum(jnp.searchsorted(pend, jnp.arange(nblk) * blk, side='right'), N_EXPERTS - 1)

    def expert_rows(args):
        xb, e = args
        return (jax.nn.silu(xb @ w_gate[e]) * (xb @ w_up[e])) @ w_down[e]

    out = lax.map(expert_rows, (buf.reshape(nblk, blk, d), blk_e)).reshape(rows, d)
    return jax.ops.segment_sum(out[dest] * sg[:, None].astype(out.dtype), st, num_segments=n)


def block_forward(x, c, mixer, mixer_args, ada_w, ada_b, g_attn, g_ffn, w_o, moe_w):
    sh1, sc1, gt1, sh2, sc2, gt2 = adaln(c, ada_w, ada_b)
    h = rms_norm(x, g_attn) * (1 + sc1) + sh1
    o, state = mixer(h, *mixer_args)
    x = x + gt1 * (o.reshape(x.shape[:2] + (Q_DIM,)) @ w_o)
    h = rms_norm(x, g_ffn) * (1 + sc2) + sh2
    x = x + gt2 * hier_moe(h.reshape(-1, x.shape[-1]), *moe_w).reshape(x.shape)
    return x, state


def setup_inputs(seed: int = 0) -> dict:
    key = jax.random.key(seed)
    ks = jax.random.split(key, 32)
    f32 = jnp.float32
    nrm = lambda k, shape, s=1.0: jax.random.normal(k, shape, f32) * s
    n_pages = PAST_LEN // PAGE_SIZE
    used = DEC_BATCH * n_pages
    pool = used + max(1, used // 4)
    page_table = jax.random.permutation(ks[0], pool)[:used].reshape(DEC_BATCH, n_pages).astype(jnp.int32)
    sd = D_MODEL ** -0.5
    return {
        'x_prompt': nrm(ks[1], (BATCH, SEQ, D_MODEL)),
        'x_sample': nrm(ks[2], (DEC_BATCH, DEC_SEQ, D_MODEL)),
        'cache_k_a': nrm(ks[3], (N_LAYERS_A, pool, PAGE_SIZE, N_KV_HEADS, HEAD_DIM)),
        'cache_v_a': nrm(ks[4], (N_LAYERS_A, pool, PAGE_SIZE, N_KV_HEADS, HEAD_DIM)),
        'cache_k_b': nrm(ks[5], (N_LAYERS_B, pool, PAGE_SIZE, N_KV_HEADS, HEAD_DIM)),
        'cache_v_b': nrm(ks[6], (N_LAYERS_B, pool, PAGE_SIZE, N_KV_HEADS, HEAD_DIM)),
        'cache_kidx_b': nrm(ks[7], (N_LAYERS_B, pool, PAGE_SIZE, IDX_DIM)),
        'page_table': page_table,
        'c_prompt': nrm(ks[8], (BATCH, D_MODEL)),
        'c_sample': nrm(ks[9], (DEC_BATCH, D_MODEL)),
        'rel_table': nrm(ks[10], (REL_BUCKETS, N_HEADS), 0.5),
        'ada_w': nrm(ks[11], (DEPTH, D_MODEL, 6 * D_MODEL), 0.5 * sd),
        'ada_b': nrm(ks[12], (DEPTH, 6 * D_MODEL), 0.02),
        'norm_attn': 1.0 + nrm(ks[13], (DEPTH, D_MODEL), 0.02),
        'norm_ffn': 1.0 + nrm(ks[14], (DEPTH, D_MODEL), 0.02),
        'q_norm': 1.0 + nrm(ks[15], (DEPTH, HEAD_DIM), 0.02),
        'k_norm': 1.0 + nrm(ks[16], (DEPTH, HEAD_DIM), 0.02),
        'w_in_a': nrm(ks[17], (N_LAYERS_A, D_MODEL, IN_DIM_A), sd),
        'w_in_b': nrm(ks[18], (N_LAYERS_B, D_MODEL, IN_DIM_B), sd),
        'w_o': nrm(ks[19], (DEPTH, Q_DIM, D_MODEL), Q_DIM ** -0.5),
        'w_grp': nrm(ks[20], (DEPTH, D_MODEL, N_GROUPS), sd),
        'b_grp': nrm(ks[21], (DEPTH, N_GROUPS), 0.01),
        'w_rt': nrm(ks[22], (DEPTH, D_MODEL, N_EXPERTS), sd),
        'b_rt': nrm(ks[23], (DEPTH, N_EXPERTS), 0.01),
        'w_gate': nrm(ks[24], (DEPTH, N_EXPERTS, D_MODEL, D_EXPERT), sd),
        'w_up': nrm(ks[25], (DEPTH, N_EXPERTS, D_MODEL, D_EXPERT), sd),
        'w_down': nrm(ks[26], (DEPTH, N_EXPERTS, D_EXPERT, D_MODEL), D_EXPERT ** -0.5),
    }


def reference(x_prompt, x_sample, cache_k_a, cache_v_a, cache_k_b, cache_v_b, cache_kidx_b, page_table,
              c_prompt, c_sample, rel_table, ada_w, ada_b, norm_attn, norm_ffn, q_norm, k_norm,
              w_in_a, w_in_b, w_o, w_grp, b_grp, w_rt, b_rt, w_gate, w_up, w_down):
    xp, xs = x_prompt, x_sample
    ka_p, va_p, kb_p, vb_p, ib_p = [], [], [], [], []
    ka_s, va_s, kb_s, vb_s, ib_s = [], [], [], [], []
    for i in range(DEPTH):
        j = i // N_MIXERS
        common = (ada_w[i], ada_b[i], norm_attn[i], norm_ffn[i], w_o[i],
                  (w_grp[i], b_grp[i], w_rt[i], b_rt[i], w_gate[i], w_up[i], w_down[i]))
        if i % N_MIXERS == 0:
            xp, (k_, v_) = block_forward(xp, c_prompt, moba_prompt,
                                         (w_in_a[j], q_norm[i], k_norm[i], rel_table), *common)
            xs, (ks_, vs_) = block_forward(xs, c_sample, moba_sample,
                                           (w_in_a[j], q_norm[i], k_norm[i], rel_table,
                                            cache_k_a[j], cache_v_a[j], page_table), *common)
            ka_p.append(k_); va_p.append(v_); ka_s.append(ks_); va_s.append(vs_)
        else:
            xp, (k_, v_, i_) = block_forward(xp, c_prompt, dsa_prompt,
                                             (w_in_b[j], q_norm[i], k_norm[i], rel_table), *common)
            xs, (ks_, vs_, is_) = block_forward(xs, c_sample, dsa_sample,
                                                (w_in_b[j], q_norm[i], k_norm[i], rel_table,
                                                 cache_k_b[j], cache_v_b[j], cache_kidx_b[j], page_table), *common)
            kb_p.append(k_); vb_p.append(v_); ib_p.append(i_)
            kb_s.append(ks_); vb_s.append(vs_); ib_s.append(is_)
    return (xp, xs,
            jnp.stack(ka_p), jnp.stack(va_p), jnp.stack(kb_p), jnp.stack(vb_p), jnp.stack(ib_p),
            jnp.stack(ka_s), jnp.stack(va_s), jnp.stack(kb_s), jnp.stack(vb_s), jnp.stack(ib_s))
```

```python
import functools
import math

import numpy as np
import jax
import jax.numpy as jnp
from jax import lax
from jax.experimental import pallas as pl
from jax.experimental.pallas import tpu as pltpu

F32 = jnp.float32
BF16 = jnp.bfloat16
I32 = jnp.int32

D_MODEL = 2048
N_HEADS = 16
HEAD_DIM = 128
N_KV_HEADS = 4
GQA_GROUP = N_HEADS // N_KV_HEADS
Q_DIM = N_HEADS * HEAD_DIM
KV_DIM = N_KV_HEADS * HEAD_DIM
ATTN_SCALE = HEAD_DIM ** -0.5
MOBA_BLOCK = 256
MOBA_TOPK = 3
IDX_HEADS = 16
IDX_DIM = 64
IDX_SCALE = IDX_DIM ** -0.5
IDX_W_SCALE = IDX_HEADS ** -0.5
DSA_TOPK = 256
REL_BUCKETS = 32
REL_MAX_DIST = 128
N_GROUPS = 4
EXPERTS_PER_GROUP = 8
N_EXPERTS = N_GROUPS * EXPERTS_PER_GROUP
D_EXPERT = 768
PAGE_SIZE = 128
IN_DIM_A = Q_DIM + 2 * KV_DIM
IDX_Q_DIM = IDX_HEADS * IDX_DIM
RMS_EPS = 1e-6
NEG = -1e30

LANES = 128
ATT_TILE = 256
ROW_TILE = 1024
COL_TILE = 512
MOE_BLOCK = 256
FF_CHUNK = 256
VMEM_LIMIT = 56 * 1024 * 1024


def _params(sem):
    return pltpu.CompilerParams(dimension_semantics=sem, vmem_limit_bytes=VMEM_LIMIT)


def _adaln_body(c_ref, w_ref, b_ref, o_ref):
    c = c_ref[...]
    a = c * (1.0 / (1.0 + jnp.exp(-c)))
    o_ref[0] = jnp.dot(a, w_ref[0], preferred_element_type=F32) + b_ref[0]


def adaln_call(c_all, ada_w, ada_b):
    n_layers, d, n = ada_w.shape
    r = c_all.shape[0]
    tn = 1024
    return pl.pallas_call(
        _adaln_body,
        grid=(n_layers, n // tn),
        in_specs=[pl.BlockSpec((r, d), lambda l, j: (0, 0)),
                  pl.BlockSpec((1, d, tn), lambda l, j: (l, 0, j)),
                  pl.BlockSpec((1, 1, tn), lambda l, j: (l, 0, j))],
        out_specs=pl.BlockSpec((1, r, tn), lambda l, j: (l, 0, j)),
        out_shape=jax.ShapeDtypeStruct((n_layers, r, n), F32),
        compiler_params=_params(("arbitrary", "arbitrary")),
        name="adaln",
    )(c_all, ada_w, ada_b.reshape(n_layers, 1, n))


def _modulated_norm(x, g, sc, sh):
    ms = jnp.mean(x * x, axis=-1, keepdims=True)
    return (x * lax.rsqrt(ms + RMS_EPS) * g) * (1.0 + sc) + sh


def _norm_proj_body(x_ref, g_ref, sc_ref, sh_ref, w_ref, gain_ref, o_ref, h_scr, *, n_norm_tiles, tn):
    j = pl.program_id(1)

    @pl.when(j == 0)
    def _():
        h_scr[...] = _modulated_norm(x_ref[...], g_ref[...], sc_ref[0], sh_ref[0]).astype(BF16)

    z = jnp.dot(h_scr[...], w_ref[...].astype(BF16), preferred_element_type=F32)

    @pl.when(j < n_norm_tiles)
    def _():
        for k in range(tn // HEAD_DIM):
            sl = slice(k * HEAD_DIM, (k + 1) * HEAD_DIM)
            zk = z[:, sl]
            ms = jnp.mean(zk * zk, axis=-1, keepdims=True)
            o_ref[:, sl] = zk * lax.rsqrt(ms + RMS_EPS) * gain_ref[:, sl]

    @pl.when(j >= n_norm_tiles)
    def _():
        o_ref[...] = z


def norm_proj_call(x2d, g, sc, sh, w, gain, *, n_cols, n_norm_cols, tm, rows_per_mod, tn=COL_TILE):
    m, d = x2d.shape
    assert m % tm == 0 and n_cols % tn == 0 and n_norm_cols % tn == 0
    r = sc.shape[1]
    tiles_per_mod = rows_per_mod // tm
    mod_spec = pl.BlockSpec((1, r, d), lambda i, j: (i // tiles_per_mod, 0, 0))
    return pl.pallas_call(
        functools.partial(_norm_proj_body, n_norm_tiles=n_norm_cols // tn, tn=tn),
        grid=(m // tm, n_cols // tn),
        in_specs=[pl.BlockSpec((tm, d), lambda i, j: (i, 0)),
                  pl.BlockSpec((1, d), lambda i, j: (0, 0)),
                  mod_spec, mod_spec,
                  pl.BlockSpec((d, tn), lambda i, j: (0, j)),
                  pl.BlockSpec((1, tn), lambda i, j: (0, j))],
        out_specs=pl.BlockSpec((tm, tn), lambda i, j: (i, j)),
        out_shape=jax.ShapeDtypeStruct((m, n_cols), F32),
        scratch_shapes=[pltpu.VMEM((tm, d), BF16)],
        compiler_params=_params(("arbitrary", "arbitrary")),
        name="norm_proj",
    )(x2d, g.reshape(1, d), sc, sh, w, gain)


def _linear_residual_body(a_ref, w_ref, x_ref, gt_ref, o_ref):
    y = jnp.dot(a_ref[...], w_ref[...].astype(BF16), preferred_element_type=F32)
    o_ref[...] = x_ref[...] + gt_ref[0] * y


def linear_residual_call(a, w, x2d, gt, *, tm, rows_per_mod, tn=COL_TILE):
    m, k = a.shape
    n = w.shape[1]
    r = gt.shape[1]
    tiles_per_mod = rows_per_mod // tm
    return pl.pallas_call(
        _linear_residual_body,
        grid=(m // tm, n // tn),
        in_specs=[pl.BlockSpec((tm, k), lambda i, j: (i, 0)),
                  pl.BlockSpec((k, tn), lambda i, j: (0, j)),
                  pl.BlockSpec((tm, tn), lambda i, j: (i, j)),
                  pl.BlockSpec((1, r, tn), lambda i, j: (i // tiles_per_mod, 0, j))],
        out_specs=pl.BlockSpec((tm, tn), lambda i, j: (i, j)),
        out_shape=jax.ShapeDtypeStruct((m, n), F32),
        compiler_params=_params(("arbitrary", "arbitrary")),
        name="linear_residual",
    )(a, w, x2d, gt)


def _rel_bucket_np(dist):
    n = np.maximum(dist, 0)
    exact = REL_BUCKETS // 2
    logr = (np.log(np.maximum(n, 1).astype(np.float32) / np.float32(exact))
            / np.float32(math.log(REL_MAX_DIST / exact))).astype(np.float32)
    large = np.minimum(exact + (logr * np.float32(REL_BUCKETS - exact)).astype(np.int32), REL_BUCKETS - 1)
    return np.where(n < exact, n, large).astype(np.int32)


def _bias_body(bk_ref, tab_ref, o_ref):
    h = pl.program_id(0)
    bk = bk_ref[...]
    acc = jnp.zeros(bk.shape, F32)
    for b in range(REL_BUCKETS):
        acc = jnp.where(bk == b, tab_ref[b, h], acc)
    o_ref[0] = acc


def bias_tiles_call(rel_table):
    t = np.arange(ATT_TILE)
    d = t[:, None] - t[None, :]
    buckets = np.stack([_rel_bucket_np(d), _rel_bucket_np(d + ATT_TILE)]).astype(np.int32)
    return pl.pallas_call(
        _bias_body,
        grid=(N_HEADS,),
        in_specs=[pl.BlockSpec((2, ATT_TILE, ATT_TILE), lambda h: (0, 0, 0)),
                  pl.BlockSpec(memory_space=pltpu.SMEM)],
        out_specs=pl.BlockSpec((1, 2, ATT_TILE, ATT_TILE), lambda h: (h, 0, 0, 0)),
        out_shape=jax.ShapeDtypeStruct((N_HEADS, 2, ATT_TILE, ATT_TILE), F32),
        compiler_params=_params(("arbitrary",)),
        name="bias_tiles",
    )(jnp.asarray(buckets), rel_table)


def _moba_select_body(q_ref, k_ref, o_ref, kbar_scr, code_scr):
    i = pl.program_id(1)
    nblk = kbar_scr.shape[0]

    @pl.when(i == 0)
    def _():
        k = k_ref[0]
        kbar_scr[...] = jnp.mean(k.reshape(nblk, MOBA_BLOCK, KV_DIM), axis=1)
        code_scr[...] = jnp.zeros(code_scr.shape, F32)

    blk = lax.broadcasted_iota(I32, (nblk, ATT_TILE), 0)
    for h in range(N_HEADS):
        kv = h // GQA_GROUP
        gate = lax.dot_general(kbar_scr[:, kv * HEAD_DIM:(kv + 1) * HEAD_DIM],
                               q_ref[0, :, h * HEAD_DIM:(h + 1) * HEAD_DIM],
                               (((1,), (1,)), ((), ())), precision=lax.Precision.HIGHEST,
                               preferred_element_type=F32)
        gate = jnp.where(blk < i, gate, NEG)
        code = jnp.zeros((1, ATT_TILE), F32)
        for _ in range(MOBA_TOPK):
            mx = jnp.max(gate, axis=0, keepdims=True)
            idx = jnp.min(jnp.where(gate == mx, blk, nblk), axis=0, keepdims=True)
            code = code + jnp.where(idx < i, jnp.exp2(idx.astype(F32)), 0.0)
            gate = jnp.where(blk == idx, -jnp.inf, gate)
        code_scr[h:h + 1, :] = code
    o_ref[0] = code_scr[...].T


def moba_select_call(z3, n_blocks):
    b, s, _ = z3.shape
    return pl.pallas_call(
        _moba_select_body,
        grid=(b, s // ATT_TILE),
        in_specs=[pl.BlockSpec((1, ATT_TILE, Q_DIM), lambda bb, i: (bb, i, 0)),
                  pl.BlockSpec((1, s, KV_DIM), lambda bb, i: (bb, 0, Q_DIM // KV_DIM))],
        out_specs=pl.BlockSpec((1, ATT_TILE, LANES), lambda bb, i: (bb, i, 0)),
        out_shape=jax.ShapeDtypeStruct((b, s, LANES), F32),
        scratch_shapes=[pltpu.VMEM((n_blocks, KV_DIM), F32), pltpu.VMEM((LANES, ATT_TILE), F32)],
        compiler_params=_params(("arbitrary", "arbitrary")),
        name="moba_select",
    )(z3, z3)


def _flash_update(kv, q4, kb, vb, bias4, mask3, m_scr, l_scr, acc_scr):
    s = lax.dot_general(q4, kb, (((1,), (1,)), ((), ())), preferred_element_type=F32)
    s = s.reshape(GQA_GROUP, ATT_TILE, ATT_TILE) * ATTN_SCALE + bias4
    s = jnp.where(mask3, s, NEG)
    m_prev = m_scr[kv]
    m_new = jnp.maximum(m_prev, jnp.max(s, axis=-1, keepdims=True))
    alpha = jnp.exp(m_prev - m_new)
    p = jnp.exp(s - m_new)
    l_scr[kv] = alpha * l_scr[kv] + jnp.sum(p, axis=-1, keepdims=True)
    pv = jnp.dot(p.reshape(GQA_GROUP * ATT_TILE, ATT_TILE).astype(BF16), vb, preferred_element_type=F32)
    acc_scr[kv] = alpha * acc_scr[kv] + pv.reshape(GQA_GROUP, ATT_TILE, HEAD_DIM)
    m_scr[kv] = m_new


def _attn_body(q_ref, k_ref, v_ref, sel_ref, thr_ref, bias_ref, far_ref, o_ref, m_scr, l_scr, acc_scr, *, dsa):
    i = pl.program_id(1)
    j = pl.program_id(2)

    @pl.when(j == 0)
    def _():
        m_scr[...] = jnp.full(m_scr.shape, NEG, F32)
        l_scr[...] = jnp.zeros(l_scr.shape, F32)
        acc_scr[...] = jnp.zeros(acc_scr.shape, F32)

    def step(which):
        causal = (lax.broadcasted_iota(I32, (ATT_TILE, ATT_TILE), 1)
                  <= lax.broadcasted_iota(I32, (ATT_TILE, ATT_TILE), 0))
        if dsa:
            picked = jnp.where(sel_ref[0] >= thr_ref[0], 1.0, 0.0).T > 0.5
            mask_all = (picked & causal if which == 0 else picked)[None]
        for kv in range(N_KV_HEADS):
            heads = range(kv * GQA_GROUP, (kv + 1) * GQA_GROUP)
            q4 = jnp.concatenate([q_ref[0, :, h * HEAD_DIM:(h + 1) * HEAD_DIM] for h in heads],
                                 axis=0).astype(BF16)
            kb = k_ref[0, :, kv * HEAD_DIM:(kv + 1) * HEAD_DIM].astype(BF16)
            vb = v_ref[0, :, kv * HEAD_DIM:(kv + 1) * HEAD_DIM].astype(BF16)
            if which == 2:
                bias4 = far_ref[kv * GQA_GROUP:(kv + 1) * GQA_GROUP, :, 0:1]
            else:
                bias4 = bias_ref[kv * GQA_GROUP:(kv + 1) * GQA_GROUP, which]
            if dsa:
                mask3 = mask_all
            elif which == 0:
                mask3 = causal[None]
            else:
                code = sel_ref[0]
                bits = [(jnp.right_shift(code[:, h:h + 1].astype(I32), j) & 1) > 0 for h in heads]
                mask3 = jnp.stack(bits)
            _flash_update(kv, q4, kb, vb, bias4, mask3, m_scr, l_scr, acc_scr)

    @pl.when(j == i)
    def _():
        step(0)

    @pl.when(j == i - 1)
    def _():
        step(1)

    @pl.when(j < i - 1)
    def _():
        step(2)

    @pl.when(j == i)
    def _():
        for kv in range(N_KV_HEADS):
            out = acc_scr[kv] / l_scr[kv]
            for g in range(GQA_GROUP):
                h = kv * GQA_GROUP + g
                o_ref[0, :, h * HEAD_DIM:(h + 1) * HEAD_DIM] = out[g].astype(o_ref.dtype)


def attn_call(z3, sel, thr, bias, far, *, dsa):
    b, s, _ = z3.shape
    nt = s // ATT_TILE
    if dsa:
        sel_spec = pl.BlockSpec((1, ATT_TILE, ATT_TILE), lambda bb, i, j: (bb, jnp.minimum(j, i), i))
    else:
        sel_spec = pl.BlockSpec((1, ATT_TILE, LANES), lambda bb, i, j: (bb, i, 0))
    kcol, vcol = Q_DIM // KV_DIM, Q_DIM // KV_DIM + 1
    return pl.pallas_call(
        functools.partial(_attn_body, dsa=dsa),
        grid=(b, nt, nt),
        in_specs=[pl.BlockSpec((1, ATT_TILE, Q_DIM), lambda bb, i, j: (bb, i, 0)),
                  pl.BlockSpec((1, ATT_TILE, KV_DIM), lambda bb, i, j: (bb, jnp.minimum(j, i), kcol)),
                  pl.BlockSpec((1, ATT_TILE, KV_DIM), lambda bb, i, j: (bb, jnp.minimum(j, i), vcol)),
                  sel_spec,
                  pl.BlockSpec((1, 1, ATT_TILE), lambda bb, i, j: (bb, 0, i)),
                  pl.BlockSpec((N_HEADS, 2, ATT_TILE, ATT_TILE), lambda bb, i, j: (0, 0, 0, 0)),
                  pl.BlockSpec((N_HEADS, 1, LANES), lambda bb, i, j: (0, 0, 0))],
        out_specs=pl.BlockSpec((1, ATT_TILE, Q_DIM), lambda bb, i, j: (bb, i, 0)),
        out_shape=jax.ShapeDtypeStruct((b, s, Q_DIM), BF16),
        scratch_shapes=[pltpu.VMEM((N_KV_HEADS, GQA_GROUP, ATT_TILE, 1), F32),
                        pltpu.VMEM((N_KV_HEADS, GQA_GROUP, ATT_TILE, 1), F32),
                        pltpu.VMEM((N_KV_HEADS, GQA_GROUP, ATT_TILE, HEAD_DIM), F32)],
        compiler_params=_params(("arbitrary", "arbitrary", "arbitrary")),
        name="dsa_attn" if dsa else "moba_attn",
    )(z3, z3, z3, sel, thr, bias, far)


def _sortable_key(x):
    bits = pltpu.bitcast(x, I32)
    return bits ^ (jnp.right_shift(bits, 31) & 0x7FFFFFFF)


def _kth_largest_key(count_ge, k, shape):
    def bit_body(bi, t):
        cand = t ^ jnp.left_shift(jnp.int32(1), 31 - bi)
        return jnp.where(count_ge(cand) >= k, cand, t)
    return lax.fori_loop(0, 32, bit_body, jnp.full(shape, -2 ** 31, I32))


def _dsa_score_body(qi_ref, tail_ref, wi_ref, key_ref, thr_ref, key_scr, *, n_sel):
    i = pl.program_id(1)
    j = pl.program_id(2)

    @pl.when(j <= i)
    def _():
        t = tail_ref[0]
        lane = lax.broadcasted_iota(I32, t.shape, 1)
        klo = jnp.where(lane < IDX_DIM, t, 0.0)
        khi = pltpu.roll(klo, IDX_DIM, 1)
        qstk = jnp.concatenate([qi_ref[0, :, p * LANES:(p + 1) * LANES] for p in range(IDX_HEADS // 2)],
                               axis=0).astype(BF16)
        dn = (((1,), (1,)), ((), ()))
        s_lo = lax.dot_general(klo.astype(BF16), qstk, dn, preferred_element_type=F32)
        s_hi = lax.dot_general(khi.astype(BF16), qstk, dn, preferred_element_type=F32)
        sc = jnp.zeros((ATT_TILE, ATT_TILE), F32)
        for p in range(IDX_HEADS // 2):
            sl = slice(p * ATT_TILE, (p + 1) * ATT_TILE)
            sc = sc + jnp.maximum(s_lo[:, sl] * IDX_SCALE, 0.0) * wi_ref[0, 2 * p:2 * p + 1, :]
            sc = sc + jnp.maximum(s_hi[:, sl] * IDX_SCALE, 0.0) * wi_ref[0, 2 * p + 1:2 * p + 2, :]
        sc = sc * IDX_W_SCALE
        kpos = j * ATT_TILE + lax.broadcasted_iota(I32, sc.shape, 0)
        qpos = i * ATT_TILE + lax.broadcasted_iota(I32, sc.shape, 1)
        key = _sortable_key(jnp.where(kpos <= qpos, sc, NEG))
        key_ref[0] = key
        key_scr[pl.ds(pl.multiple_of(j * ATT_TILE, ATT_TILE), ATT_TILE), :] = key

    @pl.when(j > i)
    def _():
        key_ref[0] = jnp.full((ATT_TILE, ATT_TILE), -2 ** 31, I32)

    @pl.when(j == i)
    def _():
        def count_ge(cand):
            def chunk(c, cnt):
                kc = key_scr[pl.ds(pl.multiple_of(c * ATT_TILE, ATT_TILE), ATT_TILE), :]
                hit = jnp.where(kc >= cand, 1, 0).astype(I32)
                return cnt + jnp.sum(hit.reshape(ATT_TILE // 8, 8, ATT_TILE), axis=0)
            cnt8 = lax.fori_loop(0, i + 1, chunk, jnp.zeros((8, ATT_TILE), I32))
            return jnp.sum(cnt8, axis=0, keepdims=True)
        thr_ref[0] = _kth_largest_key(count_ge, n_sel, (1, ATT_TILE))


def dsa_score_call(z3, tail3, wi_t, n_sel):
    b, s, _ = z3.shape
    nt = s // ATT_TILE
    return pl.pallas_call(
        functools.partial(_dsa_score_body, n_sel=n_sel),
        grid=(b, nt, nt),
        in_specs=[pl.BlockSpec((1, ATT_TILE, IDX_Q_DIM), lambda bb, i, j: (bb, i, IN_DIM_A // IDX_Q_DIM)),
                  pl.BlockSpec((1, ATT_TILE, LANES), lambda bb, i, j: (bb, jnp.minimum(j, i), 0)),
                  pl.BlockSpec((1, IDX_HEADS, ATT_TILE), lambda bb, i, j: (bb, 0, i))],
        out_specs=[pl.BlockSpec((1, ATT_TILE, ATT_TILE), lambda bb, i, j: (bb, j, i)),
                   pl.BlockSpec((1, 1, ATT_TILE), lambda bb, i, j: (bb, 0, i))],
        out_shape=[jax.ShapeDtypeStruct((b, s, s), I32), jax.ShapeDtypeStruct((b, 1, s), I32)],
        scratch_shapes=[pltpu.VMEM((s, ATT_TILE), I32)],
        compiler_params=_params(("arbitrary", "arbitrary", "arbitrary")),
        name="dsa_score",
    )(z3, tail3, wi_t)


def _router_body(x_ref, g_ref, sc_ref, sh_ref, wr_ref, br_ref, h_ref, r_ref):
    h = _modulated_norm(x_ref[...], g_ref[...], sc_ref[0], sh_ref[0])
    h_ref[...] = h.astype(h_ref.dtype)
    lg = jnp.dot(h, wr_ref[...], precision=lax.Precision.HIGHEST, preferred_element_type=F32) + br_ref[...]
    lane = lax.broadcasted_iota(I32, lg.shape, 1)
    gl = jnp.where(lane < N_GROUPS, lg, -jnp.inf)
    gmax = jnp.max(gl, axis=-1, keepdims=True)
    g_sel = jnp.min(jnp.where(gl == gmax, lane, LANES), axis=-1, keepdims=True)
    g_w = 1.0 / jnp.sum(jnp.exp(gl - gmax), axis=-1, keepdims=True)
    lo = N_GROUPS + EXPERTS_PER_GROUP * g_sel
    el = jnp.where((lane >= lo) & (lane < lo + EXPERTS_PER_GROUP), lg, -jnp.inf)
    e1 = jnp.max(el, axis=-1, keepdims=True)
    i1 = jnp.min(jnp.where(el == e1, lane, LANES), axis=-1, keepdims=True)
    el2 = jnp.where(lane == i1, -jnp.inf, el)
    e2 = jnp.max(el2, axis=-1, keepdims=True)
    i2 = jnp.min(jnp.where(el2 == e2, lane, LANES), axis=-1, keepdims=True)
    v2 = jnp.exp(e2 - e1)
    w1 = g_w / (1.0 + v2)
    w2 = g_w * v2 / (1.0 + v2)
    r_ref[...] = jnp.where(lane == 0, (i1 - N_GROUPS).astype(F32),
                           jnp.where(lane == 1, (i2 - N_GROUPS).astype(F32),
                                     jnp.where(lane == 2, w1, jnp.where(lane == 3, w2, 0.0))))


def router_call(x2d, g, sc, sh, w_route, b_route, *, tm, rows_per_mod):
    m, d = x2d.shape
    r = sc.shape[1]
    tiles_per_mod = rows_per_mod // tm
    mod_spec = pl.BlockSpec((1, r, d), lambda i: (i // tiles_per_mod, 0, 0))
    return pl.pallas_call(
        _router_body,
        grid=(m // tm,),
        in_specs=[pl.BlockSpec((tm, d), lambda i: (i, 0)),
                  pl.BlockSpec((1, d), lambda i: (0, 0)),
                  mod_spec, mod_spec,
                  pl.BlockSpec((d, LANES), lambda i: (0, 0)),
                  pl.BlockSpec((1, LANES), lambda i: (0, 0))],
        out_specs=[pl.BlockSpec((tm, d), lambda i: (i, 0)), pl.BlockSpec((tm, LANES), lambda i: (i, 0))],
        out_shape=[jax.ShapeDtypeStruct((m, d), BF16), jax.ShapeDtypeStruct((m, LANES), F32)],
        compiler_params=_params(("arbitrary",)),
        name="router",
    )(x2d, g.reshape(1, d), sc, sh, w_route, b_route)


def _ffn_body(be_ref, nu_ref, x_ref, wg_ref, wu_ref, wd_ref, o_ref):
    i = pl.program_id(0)

    @pl.when(i < nu_ref[0])
    def _():
        x = x_ref[...]
        acc = jnp.zeros(o_ref.shape, F32)
        for c in range(D_EXPERT // FF_CHUNK):
            sl = slice(c * FF_CHUNK, (c + 1) * FF_CHUNK)
            a = jnp.dot(x, wg_ref[0, :, sl].astype(BF16), preferred_element_type=F32)
            u = jnp.dot(x, wu_ref[0, :, sl].astype(BF16), preferred_element_type=F32)
            act = a * (1.0 / (1.0 + jnp.exp(-a))) * u
            acc = acc + jnp.dot(act.astype(BF16), wd_ref[0, sl, :].astype(BF16), preferred_element_type=F32)
        o_ref[...] = acc

    @pl.when(i >= nu_ref[0])
    def _():
        o_ref[...] = jnp.zeros(o_ref.shape, F32)


def ffn_call(blk_e, n_used, buf, w_gate, w_up, w_down):
    rows, d = buf.shape
    nblk = rows // MOE_BLOCK
    last = lambda i, nu: jnp.minimum(i, nu[0] - 1)
    grid_spec = pltpu.PrefetchScalarGridSpec(
        num_scalar_prefetch=2,
        grid=(nblk,),
        in_specs=[pl.BlockSpec((MOE_BLOCK, d), lambda i, be, nu: (last(i, nu), 0)),
                  pl.BlockSpec((1, d, D_EXPERT), lambda i, be, nu: (be[last(i, nu)], 0, 0)),
                  pl.BlockSpec((1, d, D_EXPERT), lambda i, be, nu: (be[last(i, nu)], 0, 0)),
                  pl.BlockSpec((1, D_EXPERT, d), lambda i, be, nu: (be[last(i, nu)], 0, 0))],
        out_specs=pl.BlockSpec((MOE_BLOCK, d), lambda i, be, nu: (i, 0)),
    )
    return pl.pallas_call(
        _ffn_body,
        grid_spec=grid_spec,
        out_shape=jax.ShapeDtypeStruct((rows, d), F32),
        compiler_params=_params(("arbitrary",)),
        name="expert_ffn",
    )(blk_e, n_used, buf, w_gate, w_up, w_down)


def _combine_body(x_ref, r0_ref, r1_ref, rt_ref, gt_ref, o_ref):
    w0 = rt_ref[:, 2:3]
    w1 = rt_ref[:, 3:4]
    o_ref[...] = x_ref[...] + gt_ref[0] * (w0 * r0_ref[...] + w1 * r1_ref[...])


def combine_call(x2d, r0, r1, route, gt, *, tm, rows_per_mod, tn=1024):
    m, d = x2d.shape
    r = gt.shape[1]
    tiles_per_mod = rows_per_mod // tm
    blk = pl.BlockSpec((tm, tn), lambda i, j: (i, j))
    return pl.pallas_call(
        _combine_body,
        grid=(m // tm, d // tn),
        in_specs=[blk, blk, blk,
                  pl.BlockSpec((tm, LANES), lambda i, j: (i, 0)),
                  pl.BlockSpec((1, r, tn), lambda i, j: (i // tiles_per_mod, 0, j))],
        out_specs=blk,
        out_shape=jax.ShapeDtypeStruct((m, d), F32),
        compiler_params=_params(("arbitrary", "arbitrary")),
        name="moe_combine",
    )(x2d, r0, r1, route, gt)


def moe_dispatch(eid):
    n = eid.shape[0]
    a = n * 2
    flat_e = eid.reshape(a)
    order = jnp.argsort(flat_e, stable=True)
    se = flat_e[order]
    counts = jnp.bincount(flat_e, length=N_EXPERTS)
    padded = (counts + MOE_BLOCK - 1) // MOE_BLOCK * MOE_BLOCK
    pend = jnp.cumsum(padded)
    pstart = pend - padded
    cstart = jnp.cumsum(counts) - counts
    dest_sorted = (pstart[se] + jnp.arange(a) - cstart[se]).astype(I32)
    nblk = -(-(a + N_EXPERTS * (MOE_BLOCK - 1)) // MOE_BLOCK)
    rows = nblk * MOE_BLOCK
    src_tok = jnp.full((rows,), n, I32).at[dest_sorted].set((order // 2).astype(I32))
    dest = jnp.zeros((a,), I32).at[order].set(dest_sorted).reshape(n, 2)
    blk_e = jnp.minimum(jnp.searchsorted(pend, jnp.arange(nblk) * MOE_BLOCK, side='right'),
                        N_EXPERTS - 1).astype(I32)
    n_used = (pend[-1] // MOE_BLOCK).astype(I32).reshape(1)
    return src_tok, dest, blk_e, n_used


def moe_layer(xs, mods, rows_per_mods, g_ffn, w_grp, b_grp, w_rt, b_rt, w_gate, w_up, w_down, tms):
    d = xs[0].shape[1]
    n_route = N_GROUPS + N_EXPERTS
    w_route = jnp.pad(jnp.concatenate([w_grp, w_rt], axis=1), ((0, 0), (0, LANES - n_route)))
    b_route = jnp.pad(jnp.concatenate([b_grp, b_rt]), (0, LANES - n_route)).reshape(1, LANES)
    hs, routes = [], []
    for x, (sc, sh, _), rpm, tm in zip(xs, mods, rows_per_mods, tms):
        h, rt = router_call(x, g_ffn, sc, sh, w_route, b_route, tm=tm, rows_per_mod=rpm)
        hs.append(h)
        routes.append(rt)
    route = jnp.concatenate(routes) if len(routes) > 1 else routes[0]
    src_tok, dest, blk_e, n_used = moe_dispatch(route[:, :2].astype(I32))
    h_all = jnp.concatenate(hs + [jnp.zeros((1, d), hs[0].dtype)])
    buf = h_all[src_tok]
    out_buf = ffn_call(blk_e, n_used, buf, w_gate, w_up, w_down)
    r0 = out_buf[dest[:, 0]]
    r1 = out_buf[dest[:, 1]]
    outs, off = [], 0
    for x, (_, _, gt), rt, rpm, tm in zip(xs, mods, routes, rows_per_mods, tms):
        m = x.shape[0]
        outs.append(combine_call(x, r0[off:off + m], r1[off:off + m], rt, gt, tm=tm, rows_per_mod=rpm))
        off += m
    return outs


MEAN_PAGES = 8
SCORE_PAGES = 16


def _page_spec(shape, slot_of, col_of=None):
    def index_map(*args):
        ids, pages = args[:2], args[2]
        return (pages[slot_of(*ids)], 0, 0 if col_of is None else col_of(*ids))
    return pl.BlockSpec(shape, index_map)


def _block_mean_body(pt_ref, *refs):
    pages, o_ref = refs[:MEAN_PAGES], refs[MEAN_PAGES]
    per_blk = MOBA_BLOCK // PAGE_SIZE
    for blk in range(MEAN_PAGES // per_blk):
        tot = sum(jnp.sum(pages[blk * per_blk + p][0], axis=0, keepdims=True) for p in range(per_blk))
        o_ref[0, 0, blk:blk + 1, :] = tot * (1.0 / MOBA_BLOCK)


def paged_block_mean_call(cache, pt_flat, nb, n_pages):
    per_step = MEAN_PAGES * PAGE_SIZE // MOBA_BLOCK
    specs = [_page_spec((1, PAGE_SIZE, KV_DIM), functools.partial(
        lambda p, b, g: b * n_pages + g * MEAN_PAGES + p, p)) for p in range(MEAN_PAGES)]
    out = pl.pallas_call(
        _block_mean_body,
        grid_spec=pltpu.PrefetchScalarGridSpec(
            num_scalar_prefetch=1, grid=(nb, n_pages // MEAN_PAGES), in_specs=specs,
            out_specs=pl.BlockSpec((1, 1, per_step, KV_DIM), lambda b, g, pt: (b, g, 0, 0))),
        out_shape=jax.ShapeDtypeStruct((nb, n_pages // MEAN_PAGES, per_step, KV_DIM), F32),
        compiler_params=_params(("arbitrary", "arbitrary")),
        name="paged_block_mean",
    )(pt_flat, *([cache] * MEAN_PAGES))
    return out.reshape(nb, n_pages * PAGE_SIZE // MOBA_BLOCK, KV_DIM)


def _sample_select_body(q_ref, kbar_ref, o_ref, *, own):
    q = q_ref[0]
    row = lax.broadcasted_iota(I32, (N_HEADS, LANES), 0)
    lane = lax.broadcasted_iota(I32, (N_HEADS, LANES), 1)
    gate = jnp.zeros((N_HEADS, LANES), F32)
    for kv in range(N_KV_HEADS):
        gk = lax.dot_general(q, kbar_ref[0, :, kv * HEAD_DIM:(kv + 1) * HEAD_DIM], (((1,), (1,)), ((), ())),
                             precision=lax.Precision.HIGHEST, preferred_element_type=F32)
        gate = jnp.where(row // GQA_GROUP == kv, gk, gate)
    gate = jnp.where(lane < own, gate, NEG)
    out = jnp.zeros((N_HEADS, LANES), I32)
    for c in range(MOBA_TOPK):
        mx = jnp.max(gate, axis=-1, keepdims=True)
        idx = jnp.min(jnp.where(gate == mx, lane, LANES), axis=-1, keepdims=True)
        out = jnp.where(lane == c, idx, out)
        gate = jnp.where(lane == idx, -jnp.inf, gate)
    o_ref[0] = out


def sample_moba_select_call(q3, kbar_pad, own):
    nb = q3.shape[0]
    return pl.pallas_call(
        functools.partial(_sample_select_body, own=own),
        grid=(nb,),
        in_specs=[pl.BlockSpec((1, N_HEADS, HEAD_DIM), lambda b: (b, 0, 0)),
                  pl.BlockSpec((1, LANES, KV_DIM), lambda b: (b, 0, 0))],
        out_specs=pl.BlockSpec((1, N_HEADS, LANES), lambda b: (b, 0, 0)),
        out_shape=jax.ShapeDtypeStruct((nb, N_HEADS, LANES), I32),
        compiler_params=_params(("arbitrary",)),
        name="sample_moba_select",
    )(q3, kbar_pad)


def _sample_moba_attn_body(pg_ref, blk_ref, q_ref, kn_ref, vn_ref, prev_ref, far_ref, own_ref, *refs, own):
    n_pg = MOBA_TOPK * (MOBA_BLOCK // PAGE_SIZE)
    k_pages, v_pages, o_ref = refs[:n_pg], refs[n_pg:2 * n_pg], refs[2 * n_pg]
    b, h = pl.program_id(0), pl.program_id(1)
    q = q_ref[0]
    q8 = jnp.broadcast_to(q, (8, HEAD_DIM)).astype(BF16)
    per_blk = MOBA_BLOCK // PAGE_SIZE
    logits = []
    for n in range(MOBA_TOPK):
        is_prev = blk_ref[(b * N_HEADS + h) * MOBA_TOPK + n] == own - 1
        for p in range(per_blk):
            s = lax.dot_general(q8, k_pages[n * per_blk + p][0].astype(BF16), (((1,), (1,)), ((), ())),
                                preferred_element_type=F32)
            bias = jnp.where(is_prev, prev_ref[0, :, p * PAGE_SIZE:(p + 1) * PAGE_SIZE], far_ref[0])
            logits.append(s * ATTN_SCALE + bias)
    s_own = jnp.sum(q * kn_ref[0], axis=-1, keepdims=True) * ATTN_SCALE + own_ref[0, :, 0:1]
    m = s_own
    for s in logits:
        m = jnp.maximum(m, jnp.max(s, axis=-1, keepdims=True))
    p_own = jnp.exp(s_own - m)
    l = p_own
    acc = p_own * vn_ref[0]
    for s, vp in zip(logits, v_pages):
        p = jnp.exp(s - m)
        l = l + jnp.sum(p, axis=-1, keepdims=True)
        acc = acc + jnp.dot(p.astype(BF16), vp[0].astype(BF16), preferred_element_type=F32)
    o_ref[0] = (acc / l)[0:1]


def sample_moba_attn_call(z_s3, cache_k, cache_v, pages, blks, prev_row, far, own_bias, own):
    nb = z_s3.shape[0]
    per_blk = MOBA_BLOCK // PAGE_SIZE
    n_pg = MOBA_TOPK * per_blk
    kvcol = lambda b, h: h // GQA_GROUP
    page_specs = [_page_spec((1, PAGE_SIZE, HEAD_DIM),
                             functools.partial(lambda s, b, h: (b * N_HEADS + h) * n_pg + s, s), kvcol)
                  for s in range(n_pg)]
    head_spec = lambda w: pl.BlockSpec((1, 1, w), lambda b, h, pg, bl: (h, 0, 0))
    out = pl.pallas_call(
        functools.partial(_sample_moba_attn_body, own=own),
        grid_spec=pltpu.PrefetchScalarGridSpec(
            num_scalar_prefetch=2, grid=(nb, N_HEADS),
            in_specs=[pl.BlockSpec((1, 1, HEAD_DIM), lambda b, h, pg, bl: (b, 0, h)),
                      pl.BlockSpec((1, 1, HEAD_DIM), lambda b, h, pg, bl: (b, 0, N_HEADS + h // GQA_GROUP)),
                      pl.BlockSpec((1, 1, HEAD_DIM),
                                   lambda b, h, pg, bl: (b, 0, N_HEADS + N_KV_HEADS + h // GQA_GROUP)),
                      head_spec(MOBA_BLOCK), head_spec(LANES), head_spec(LANES)] + page_specs + page_specs,
            out_specs=pl.BlockSpec((1, 1, HEAD_DIM), lambda b, h, pg, bl: (b * N_HEADS + h, 0, 0))),
        out_shape=jax.ShapeDtypeStruct((nb * N_HEADS, 1, HEAD_DIM), F32),
        compiler_params=_params(("arbitrary", "arbitrary")),
        name="sample_moba_attn",
    )(pages, blks, z_s3, z_s3, z_s3, prev_row, far, own_bias, *([cache_k] * n_pg), *([cache_v] * n_pg))
    return out.reshape(nb, Q_DIM)


def _indexer_score(qi, wi_col, keys_pad):
    s = lax.dot_general(qi.astype(BF16), keys_pad.astype(BF16), (((1,), (1,)), ((), ())),
                        preferred_element_type=F32)
    return jnp.sum(jnp.maximum(s * IDX_SCALE, 0.0) * wi_col, axis=0, keepdims=True) * IDX_W_SCALE


def _sample_score_body(pt_ref, qi_ref, wi_ref, *refs):
    pages, o_ref = refs[:SCORE_PAGES], refs[SCORE_PAGES]
    pad = jnp.zeros((PAGE_SIZE, LANES - IDX_DIM), F32)
    for p in range(SCORE_PAGES):
        keys = jnp.concatenate([pages[p][0], pad], axis=1)
        o_ref[0, p:p + 1, :] = _indexer_score(qi_ref[0], wi_ref[0], keys)


def sample_dsa_score_call(cache_kidx, pt_flat, qi_pad, wi_col, n_pages):
    nb = qi_pad.shape[0]
    specs = [_page_spec((1, PAGE_SIZE, IDX_DIM), functools.partial(
        lambda p, b, g: b * n_pages + g * SCORE_PAGES + p, p)) for p in range(SCORE_PAGES)]
    return pl.pallas_call(
        _sample_score_body,
        grid_spec=pltpu.PrefetchScalarGridSpec(
            num_scalar_prefetch=1, grid=(nb, n_pages // SCORE_PAGES),
            in_specs=[pl.BlockSpec((1, IDX_HEADS, LANES), lambda b, g, pt: (b, 0, 0)),
                      pl.BlockSpec((1, IDX_HEADS, 1), lambda b, g, pt: (b, 0, 0))] + specs,
            out_specs=pl.BlockSpec((1, SCORE_PAGES, PAGE_SIZE), lambda b, g, pt: (b, g, 0))),
        out_shape=jax.ShapeDtypeStruct((nb, n_pages, PAGE_SIZE), F32),
        compiler_params=_params(("arbitrary", "arbitrary")),
        name="sample_dsa_score",
    )(pt_flat, qi_pad, wi_col, *([cache_kidx] * SCORE_PAGES))


def _sample_dsa_select_body(sc_ref, qi_ref, wi_ref, ki_ref, m_ref, mnew_ref, *, n_sel):
    keys = _sortable_key(sc_ref[0])
    key_new = _sortable_key(_indexer_score(qi_ref[0], wi_ref[0], jnp.broadcast_to(ki_ref[0], (8, LANES))))[:, 0:1]

    def count_ge(cand):
        hit = jnp.where(keys >= cand, 1, 0).astype(I32)
        tot = jnp.sum(jnp.sum(hit, axis=0, keepdims=True), axis=1, keepdims=True)
        return tot + jnp.where(key_new >= cand, 1, 0).astype(I32)

    thr = _kth_largest_key(count_ge, n_sel, (1, 1))
    m_ref[0] = jnp.where(keys >= thr, 1.0, 0.0)
    mnew_ref[0] = jnp.broadcast_to(jnp.where(key_new >= thr, 1.0, 0.0), (1, LANES))


def sample_dsa_select_call(scores, qi_pad, wi_col, ki_pad, n_sel):
    nb, n_pages, _ = scores.shape
    return pl.pallas_call(
        functools.partial(_sample_dsa_select_body, n_sel=n_sel),
        grid=(nb,),
        in_specs=[pl.BlockSpec((1, n_pages, PAGE_SIZE), lambda b: (b, 0, 0)),
                  pl.BlockSpec((1, IDX_HEADS, LANES), lambda b: (b, 0, 0)),
                  pl.BlockSpec((1, IDX_HEADS, 1), lambda b: (b, 0, 0)),
                  pl.BlockSpec((1, 1, LANES), lambda b: (b, 0, 0))],
        out_specs=[pl.BlockSpec((1, n_pages, PAGE_SIZE), lambda b: (b, 0, 0)),
                   pl.BlockSpec((1, 1, LANES), lambda b: (b, 0, 0))],
        out_shape=[jax.ShapeDtypeStruct((nb, n_pages, PAGE_SIZE), F32), jax.ShapeDtypeStruct((nb, 1, LANES), F32)],
        compiler_params=_params(("arbitrary",)),
        name="sample_dsa_select",
    )(scores, qi_pad, wi_col, ki_pad)


def _row_copy(src_hbm, dst_vmem, row, slot, sem):
    return pltpu.make_async_copy(src_hbm.at[pl.ds(row, 1)], dst_vmem.at[pl.ds(slot, 1)], sem)


def _sample_dsa_attn_body(rows_ref, q_ref, valid_ref, bucket_ref, tab_ref, kn_ref, vn_ref, new_ref,
                          kc_ref, vc_ref, o_ref, kbuf, vbuf, sems, *, n_sel):
    b = pl.program_id(0)

    def start(r, carry):
        row = rows_ref[b * n_sel + r]
        _row_copy(kc_ref, kbuf, row, r, sems.at[0]).start()
        _row_copy(vc_ref, vbuf, row, r, sems.at[1]).start()
        return carry

    def wait(r, carry):
        _row_copy(kc_ref, kbuf, 0, r, sems.at[0]).wait()
        _row_copy(vc_ref, vbuf, 0, r, sems.at[1]).wait()
        return carry

    lax.fori_loop(0, n_sel, start, 0)
    lax.fori_loop(0, n_sel, wait, 0)

    onehot = jnp.where(lax.broadcasted_iota(I32, (LANES, n_sel), 0) == bucket_ref[0], 1.0, 0.0)
    bias = jnp.dot(tab_ref[...], onehot, precision=lax.Precision.HIGHEST, preferred_element_type=F32)
    valid = valid_ref[0] > 0.5
    new_ok = new_ref[0, :, 0:1] > 0.5
    for kv in range(N_KV_HEADS):
        cols = slice(kv * HEAD_DIM, (kv + 1) * HEAD_DIM)
        q8 = q_ref[0, kv]
        s = lax.dot_general(q8.astype(BF16), kbuf[:, cols].astype(BF16), (((1,), (1,)), ((), ())),
                            preferred_element_type=F32)
        bias8 = jnp.concatenate([bias[kv * GQA_GROUP:(kv + 1) * GQA_GROUP],
                                 jnp.zeros((8 - GQA_GROUP, n_sel), F32)], axis=0)
        s = jnp.where(valid, s * ATTN_SCALE + bias8, NEG)
        own_bias = jnp.concatenate([tab_ref[kv * GQA_GROUP:(kv + 1) * GQA_GROUP, 0:1],
                                    jnp.zeros((8 - GQA_GROUP, 1), F32)], axis=0)
        s_new = jnp.sum(q8 * kn_ref[0, :, cols], axis=-1, keepdims=True) * ATTN_SCALE + own_bias
        s_new = jnp.where(new_ok, s_new, NEG)
        m = jnp.maximum(jnp.max(s, axis=-1, keepdims=True), s_new)
        p = jnp.exp(s - m)
        p_new = jnp.exp(s_new - m)
        l = jnp.sum(p, axis=-1, keepdims=True) + p_new
        acc = jnp.dot(p.astype(BF16), vbuf[:, cols].astype(BF16), preferred_element_type=F32)
        o_ref[0, kv] = (acc + p_new * vn_ref[0, :, cols]) / l


def sample_dsa_attn_call(rows, q8, valid, bucket, table_t, k_new, v_new, new_sel, cache_k2d, cache_v2d, n_sel):
    nb = q8.shape[0]
    return pl.pallas_call(
        functools.partial(_sample_dsa_attn_body, n_sel=n_sel),
        grid_spec=pltpu.PrefetchScalarGridSpec(
            num_scalar_prefetch=1, grid=(nb,),
            in_specs=[pl.BlockSpec((1, N_KV_HEADS, 8, HEAD_DIM), lambda b, rw: (b, 0, 0, 0)),
                      pl.BlockSpec((1, 1, n_sel), lambda b, rw: (b, 0, 0)),
                      pl.BlockSpec((1, 1, n_sel), lambda b, rw: (b, 0, 0)),
                      pl.BlockSpec((N_HEADS, LANES), lambda b, rw: (0, 0)),
                      pl.BlockSpec((1, 1, KV_DIM), lambda b, rw: (b, 0, 0)),
                      pl.BlockSpec((1, 1, KV_DIM), lambda b, rw: (b, 0, 0)),
                      pl.BlockSpec((1, 1, LANES), lambda b, rw: (b, 0, 0)),
                      pl.BlockSpec(memory_space=pl.ANY),
                      pl.BlockSpec(memory_space=pl.ANY)],
            out_specs=pl.BlockSpec((1, N_KV_HEADS, 8, HEAD_DIM), lambda b, rw: (b, 0, 0, 0)),
            scratch_shapes=[pltpu.VMEM((n_sel, KV_DIM), F32), pltpu.VMEM((n_sel, KV_DIM), F32),
                            pltpu.SemaphoreType.DMA((2,))]),
        out_shape=jax.ShapeDtypeStruct((nb, N_KV_HEADS, 8, HEAD_DIM), F32),
        compiler_params=_params(("arbitrary",)),
        name="sample_dsa_attn",
    )(rows, q8, valid, bucket, table_t, k_new, v_new, new_sel, cache_k2d, cache_v2d)


def _rel_bucket(dist):
    n = jnp.maximum(dist, 0)
    exact = REL_BUCKETS // 2
    logr = jnp.log(jnp.maximum(n, 1).astype(F32) / exact) / math.log(REL_MAX_DIST / exact)
    large = jnp.minimum(exact + (logr * (REL_BUCKETS - exact)).astype(I32), REL_BUCKETS - 1)
    return jnp.where(n < exact, n, large)


def _head_gain(qn, kn, n_cols):
    return jnp.concatenate([jnp.tile(qn, N_HEADS), jnp.tile(kn, N_KV_HEADS),
                            jnp.ones((n_cols - Q_DIM - KV_DIM,), F32)]).reshape(1, n_cols)


def _project(x2d, g, sc, sh, w, qn, kn, *, tm, rows_per_mod):
    n_w = w.shape[1]
    n_main = IN_DIM_A if n_w == IN_DIM_A else IN_DIM_A + IDX_Q_DIM
    z = norm_proj_call(x2d, g, sc, sh, w, _head_gain(qn, kn, n_main), n_cols=n_main,
                       n_norm_cols=Q_DIM + KV_DIM, tm=tm, rows_per_mod=rows_per_mod)
    if n_w == IN_DIM_A:
        return z, None
    w_tail = jnp.pad(w[:, n_main:], ((0, 0), (0, LANES - (n_w - n_main))))
    tail = norm_proj_call(x2d, g, sc, sh, w_tail, jnp.ones((1, LANES), F32), n_cols=LANES, n_norm_cols=0,
                          tm=tm, rows_per_mod=rows_per_mod, tn=LANES)
    return z, tail


def moba_sample_attention(zs, cache_k, cache_v, page_table, prev_row, far, own_bias):
    nb, n_pages = page_table.shape
    pool = cache_k.shape[0]
    own = n_pages * PAGE_SIZE // MOBA_BLOCK
    ck = cache_k.reshape(pool, PAGE_SIZE, KV_DIM)
    cv = cache_v.reshape(pool, PAGE_SIZE, KV_DIM)
    kbar = paged_block_mean_call(ck, page_table.reshape(-1), nb, n_pages)
    kbar_pad = jnp.pad(kbar, ((0, 0), (0, LANES - own), (0, 0)))
    sel = sample_moba_select_call(zs[:, :Q_DIM].reshape(nb, N_HEADS, HEAD_DIM), kbar_pad, own)[:, :, :MOBA_TOPK]
    per_blk = MOBA_BLOCK // PAGE_SIZE
    logical = sel[..., None] * per_blk + jnp.arange(per_blk)
    pages = page_table[jnp.arange(nb)[:, None, None, None], logical]
    return sample_moba_attn_call(zs.reshape(nb, 1, -1), ck, cv, pages.reshape(-1), sel.reshape(-1),
                                 prev_row, far, own_bias, own)


def dsa_sample_attention(zs, tail_s, cache_k, cache_v, cache_kidx, page_table, table_t):
    nb, n_pages = page_table.shape
    pool = cache_k.shape[0]
    past = n_pages * PAGE_SIZE
    n_sel = min(DSA_TOPK, (past + 1) // 4)
    qi_pad = jnp.pad(zs[:, IN_DIM_A:].reshape(nb, IDX_HEADS, IDX_DIM), ((0, 0), (0, 0), (0, LANES - IDX_DIM)))
    wi_col = tail_s[:, IDX_DIM:IDX_DIM + IDX_HEADS].reshape(nb, IDX_HEADS, 1)
    ki_pad = jnp.pad(tail_s[:, :IDX_DIM], ((0, 0), (0, LANES - IDX_DIM))).reshape(nb, 1, LANES)
    scores = sample_dsa_score_call(cache_kidx, page_table.reshape(-1), qi_pad, wi_col, n_pages)
    picked, new_sel = sample_dsa_select_call(scores, qi_pad, wi_col, ki_pad, n_sel)
    chosen = jnp.concatenate([picked.reshape(nb, past) > 0.5, new_sel[:, 0, :1] > 0.5], axis=1)
    idx = jnp.argsort(jnp.logical_not(chosen), axis=1, stable=True)[:, :n_sel].astype(I32)
    pos = jnp.minimum(idx, past - 1)
    rows = page_table[jnp.arange(nb)[:, None], pos // PAGE_SIZE] * PAGE_SIZE + pos % PAGE_SIZE
    valid = (idx < past).astype(F32)
    bucket = _rel_bucket(past - pos).astype(I32)
    q8 = jnp.pad(zs[:, :Q_DIM].reshape(nb, N_KV_HEADS, GQA_GROUP, HEAD_DIM),
                 ((0, 0), (0, 0), (0, 8 - GQA_GROUP), (0, 0)))
    k_new, v_new = zs[:, Q_DIM:Q_DIM + KV_DIM], zs[:, Q_DIM + KV_DIM:IN_DIM_A]
    o8 = sample_dsa_attn_call(rows.reshape(-1), q8, valid[:, None], bucket[:, None], table_t,
                              k_new[:, None], v_new[:, None], new_sel,
                              cache_k.reshape(pool * PAGE_SIZE, KV_DIM), cache_v.reshape(pool * PAGE_SIZE, KV_DIM),
                              n_sel)
    return o8[:, :, :GQA_GROUP].reshape(nb, Q_DIM)


def kernel(x_prompt, x_sample, cache_k_a, cache_v_a, cache_k_b, cache_v_b, cache_kidx_b, page_table, c_prompt, c_sample, rel_table, ada_w, ada_b, norm_attn, norm_ffn, q_norm, k_norm, w_in_a, w_in_b, w_o, w_grp, b_grp, w_rt, b_rt, w_gate, w_up, w_down):
    b, s, d = x_prompt.shape
    nb = x_sample.shape[0]
    depth = ada_w.shape[0]
    n_pages = page_table.shape[1]
    past = n_pages * PAGE_SIZE
    own = past // MOBA_BLOCK
    assert x_sample.shape[1] == 1 and past % MOBA_BLOCK == 0 and MOBA_TOPK <= own <= LANES
    assert s % ROW_TILE == 0 and nb % 8 == 0

    n_mod_rows = -(-(b + nb) // 8) * 8
    c_all = jnp.concatenate([c_prompt, c_sample, jnp.zeros((n_mod_rows - b - nb, d), F32)])
    mod = adaln_call(c_all, ada_w, ada_b)

    bias = bias_tiles_call(rel_table)
    lane_rep = lambda v: jnp.broadcast_to(v[:, None, None], (N_HEADS, 1, LANES))
    far = lane_rep(rel_table[REL_BUCKETS - 1])
    own_bias = lane_rep(rel_table[0])
    prev_row = bias[:, 1, 0:1, :]
    table_t = jnp.pad(rel_table.T, ((0, 0), (0, LANES - REL_BUCKETS)))

    xp = x_prompt.reshape(b * s, d)
    xs = x_sample.reshape(nb, d)
    kv_p = {"ka": [], "va": [], "kb": [], "vb": [], "ib": []}
    kv_s = {"ka": [], "va": [], "kb": [], "vb": [], "ib": []}
    heads = lambda t, n: t.reshape(t.shape[:-1] + (n, HEAD_DIM))

    for i in range(depth):
        j = i // 2
        mp = mod[i, :b].reshape(b, 1, 6, d)
        ms = mod[i, b:b + nb].reshape(1, nb, 6, d)
        sh1, sc1, gt1, sh2, sc2, gt2 = (mp[:, :, k] for k in range(6))
        sh1s, sc1s, gt1s, sh2s, sc2s, gt2s = (ms[:, :, k] for k in range(6))
        moba = i % 2 == 0
        w_in = w_in_a[j] if moba else w_in_b[j]
        z, tail = _project(xp, norm_attn[i], sc1, sh1, w_in, q_norm[i], k_norm[i], tm=ROW_TILE, rows_per_mod=s)
        zs, tail_s = _project(xs, norm_attn[i], sc1s, sh1s, w_in, q_norm[i], k_norm[i], tm=nb, rows_per_mod=nb)
        z3 = z.reshape(b, s, -1)
        k_new, v_new = zs[:, Q_DIM:Q_DIM + KV_DIM], zs[:, Q_DIM + KV_DIM:IN_DIM_A]
        tag = "a" if moba else "b"
        kv_p["k" + tag].append(heads(z3[:, :, Q_DIM:Q_DIM + KV_DIM], N_KV_HEADS))
        kv_p["v" + tag].append(heads(z3[:, :, Q_DIM + KV_DIM:IN_DIM_A], N_KV_HEADS))
        kv_s["k" + tag].append(heads(k_new[:, None], N_KV_HEADS))
        kv_s["v" + tag].append(heads(v_new[:, None], N_KV_HEADS))

        if moba:
            sel = moba_select_call(z3, s // MOBA_BLOCK)
            o = attn_call(z3, sel, jnp.zeros((b, 1, s), I32), bias, far, dsa=False)

            o_s = moba_sample_attention(zs, cache_k_a[j], cache_v_a[j], page_table, prev_row, far, own_bias)
        else:
            tail3 = tail.reshape(b, s, LANES)
            kv_p["ib"].append(tail3[:, :, :IDX_DIM])
            kv_s["ib"].append(tail_s[:, None, :IDX_DIM])
            wi_t = jnp.swapaxes(tail3[:, :, IDX_DIM:IDX_DIM + IDX_HEADS], 1, 2)
            key_t, thr = dsa_score_call(z3, tail3, wi_t, min(DSA_TOPK, s // 4))
            o = attn_call(z3, key_t, thr, bias, far, dsa=True)

            o_s = dsa_sample_attention(zs, tail_s, cache_k_b[j], cache_v_b[j], cache_kidx_b[j], page_table, table_t)

        xp = linear_residual_call(o.reshape(b * s, Q_DIM), w_o[i], xp, gt1, tm=ROW_TILE, rows_per_mod=s)
        xs = linear_residual_call(o_s.astype(BF16), w_o[i], xs, gt1s, tm=nb, rows_per_mod=nb)
        xp, xs = moe_layer([xp, xs], [(sc2, sh2, gt2), (sc2s, sh2s, gt2s)], [s, nb], norm_ffn[i],
                           w_grp[i], b_grp[i], w_rt[i], b_rt[i], w_gate[i], w_up[i], w_down[i], tms=[512, nb])

    stack = lambda lst: jnp.stack(lst)
    return (xp.reshape(b, s, d), xs.reshape(nb, 1, d),
            stack(kv_p["ka"]), stack(kv_p["va"]), stack(kv_p["kb"]), stack(kv_p["vb"]), stack(kv_p["ib"]),
            stack(kv_s["ka"]), stack(kv_s["va"]), stack(kv_s["kb"]), stack(kv_s["vb"]), stack(kv_s["ib"]))
```

```python
import functools
import math

import numpy as np
import jax
import jax.numpy as jnp
from jax import lax
from jax.experimental import pallas as pl
from jax.experimental.pallas import tpu as pltpu

F32 = jnp.float32
BF16 = jnp.bfloat16
I32 = jnp.int32

D_MODEL = 2048
N_HEADS = 16
HEAD_DIM = 128
N_KV_HEADS = 4
GQA_GROUP = N_HEADS // N_KV_HEADS
Q_DIM = N_HEADS * HEAD_DIM
KV_DIM = N_KV_HEADS * HEAD_DIM
ATTN_SCALE = HEAD_DIM ** -0.5
MOBA_BLOCK = 256
MOBA_TOPK = 3
IDX_HEADS = 16
IDX_DIM = 64
IDX_SCALE = IDX_DIM ** -0.5
IDX_W_SCALE = IDX_HEADS ** -0.5
DSA_TOPK = 256
REL_BUCKETS = 32
REL_MAX_DIST = 128
N_GROUPS = 4
EXPERTS_PER_GROUP = 8
N_EXPERTS = N_GROUPS * EXPERTS_PER_GROUP
D_EXPERT = 768
PAGE_SIZE = 128
IN_DIM_A = Q_DIM + 2 * KV_DIM
IDX_Q_DIM = IDX_HEADS * IDX_DIM
RMS_EPS = 1e-6
NEG = -1e30

LANES = 128
ATT_TILE = 256
ROW_TILE = 1024
COL_TILE = 512
MOE_BLOCK = 256
FF_CHUNK = 256
VMEM_LIMIT = 56 * 1024 * 1024


def _params(sem):
    return pltpu.CompilerParams(dimension_semantics=sem, vmem_limit_bytes=VMEM_LIMIT)


def _adaln_body(c_ref, w_ref, b_ref, o_ref):
    c = c_ref[...]
    a = c * (1.0 / (1.0 + jnp.exp(-c)))
    o_ref[0] = jnp.dot(a, w_ref[0], preferred_element_type=F32) + b_ref[0]


def adaln_call(c_all, ada_w, ada_b):
    n_layers, d, n = ada_w.shape
    r = c_all.shape[0]
    tn = 1024
    return pl.pallas_call(
        _adaln_body,
        grid=(n_layers, n // tn),
        in_specs=[pl.BlockSpec((r, d), lambda l, j: (0, 0)),
                  pl.BlockSpec((1, d, tn), lambda l, j: (l, 0, j)),
                  pl.BlockSpec((1, 1, tn), lambda l, j: (l, 0, j))],
        out_specs=pl.BlockSpec((1, r, tn), lambda l, j: (l, 0, j)),
        out_shape=jax.ShapeDtypeStruct((n_layers, r, n), F32),
        compiler_params=_params(("arbitrary", "arbitrary")),
        name="adaln",
    )(c_all, ada_w, ada_b.reshape(n_layers, 1, n))


def _modulated_norm(x, g, sc, sh):
    ms = jnp.mean(x * x, axis=-1, keepdims=True)
    return (x * lax.rsqrt(ms + RMS_EPS) * g) * (1.0 + sc) + sh


def _norm_proj_body(x_ref, g_ref, sc_ref, sh_ref, w_ref, gain_ref, o_ref, *rest, n_norm_tiles, tn, kv_out):
    h_scr = rest[-1]
    j = pl.program_id(1)

    @pl.when(j == 0)
    def _():
        h_scr[...] = _modulated_norm(x_ref[...], g_ref[...], sc_ref[0], sh_ref[0]).astype(BF16)

    z = jnp.dot(h_scr[...], w_ref[...].astype(BF16), preferred_element_type=F32)

    @pl.when(j < n_norm_tiles)
    def _():
        for k in range(tn // HEAD_DIM):
            sl = slice(k * HEAD_DIM, (k + 1) * HEAD_DIM)
            zk = z[:, sl]
            ms = jnp.mean(zk * zk, axis=-1, keepdims=True)
            o_ref[:, sl] = zk * lax.rsqrt(ms + RMS_EPS) * gain_ref[:, sl]

    @pl.when(j >= n_norm_tiles)
    def _():
        o_ref[...] = z

    if kv_out:
        for ref, tile in ((rest[0], Q_DIM // tn), (rest[1], Q_DIM // tn + 1)):
            @pl.when(j == tile)
            def _(ref=ref):
                for kv in range(N_KV_HEADS):
                    ref[:, kv, :] = o_ref[:, kv * HEAD_DIM:(kv + 1) * HEAD_DIM]


def norm_proj_call(x2d, g, sc, sh, w, gain, *, n_cols, n_norm_cols, tm, rows_per_mod, tn=COL_TILE, kv_out=False):
    m, d = x2d.shape
    assert m % tm == 0 and n_cols % tn == 0 and n_norm_cols % tn == 0
    assert not kv_out or tn == KV_DIM
    r = sc.shape[1]
    tiles_per_mod = rows_per_mod // tm
    mod_spec = pl.BlockSpec((1, r, d), lambda i, j: (i // tiles_per_mod, 0, 0))
    out_specs = [pl.BlockSpec((tm, tn), lambda i, j: (i, j))]
    out_shape = [jax.ShapeDtypeStruct((m, n_cols), F32)]
    if kv_out:
        out_specs += [pl.BlockSpec((tm, N_KV_HEADS, HEAD_DIM), lambda i, j: (i, 0, 0))] * 2
        out_shape += [jax.ShapeDtypeStruct((m, N_KV_HEADS, HEAD_DIM), F32)] * 2
    return pl.pallas_call(
        functools.partial(_norm_proj_body, n_norm_tiles=n_norm_cols // tn, tn=tn, kv_out=kv_out),
        grid=(m // tm, n_cols // tn),
        in_specs=[pl.BlockSpec((tm, d), lambda i, j: (i, 0)),
                  pl.BlockSpec((1, d), lambda i, j: (0, 0)),
                  mod_spec, mod_spec,
                  pl.BlockSpec((d, tn), lambda i, j: (0, j)),
                  pl.BlockSpec((1, tn), lambda i, j: (0, j))],
        out_specs=out_specs,
        out_shape=out_shape,
        scratch_shapes=[pltpu.VMEM((tm, d), BF16)],
        compiler_params=_params(("arbitrary", "arbitrary")),
        name="norm_proj",
    )(x2d, g.reshape(1, d), sc, sh, w, gain)


def _linear_residual_body(a_ref, w_ref, x_ref, gt_ref, o_ref):
    y = jnp.dot(a_ref[...], w_ref[...].astype(BF16), preferred_element_type=F32)
    o_ref[...] = x_ref[...] + gt_ref[0] * y


def linear_residual_call(a, w, x2d, gt, *, tm, rows_per_mod, tn=COL_TILE):
    m, k = a.shape
    n = w.shape[1]
    r = gt.shape[1]
    tiles_per_mod = rows_per_mod // tm
    return pl.pallas_call(
        _linear_residual_body,
        grid=(m // tm, n // tn),
        in_specs=[pl.BlockSpec((tm, k), lambda i, j: (i, 0)),
                  pl.BlockSpec((k, tn), lambda i, j: (0, j)),
                  pl.BlockSpec((tm, tn), lambda i, j: (i, j)),
                  pl.BlockSpec((1, r, tn), lambda i, j: (i // tiles_per_mod, 0, j))],
        out_specs=pl.BlockSpec((tm, tn), lambda i, j: (i, j)),
        out_shape=jax.ShapeDtypeStruct((m, n), F32),
        compiler_params=_params(("arbitrary", "arbitrary")),
        name="linear_residual",
    )(a, w, x2d, gt)


def _rel_bucket_np(dist):
    n = np.maximum(dist, 0)
    exact = REL_BUCKETS // 2
    logr = (np.log(np.maximum(n, 1).astype(np.float32) / np.float32(exact))
            / np.float32(math.log(REL_MAX_DIST / exact))).astype(np.float32)
    large = np.minimum(exact + (logr * np.float32(REL_BUCKETS - exact)).astype(np.int32), REL_BUCKETS - 1)
    return np.where(n < exact, n, large).astype(np.int32)


def _bias_body(bk_ref, tab_ref, o_ref):
    h = pl.program_id(0)
    bk = bk_ref[...]
    acc = jnp.zeros(bk.shape, F32)
    for b in range(REL_BUCKETS):
        acc = jnp.where(bk == b, tab_ref[b, h], acc)
    o_ref[0] = acc


def bias_tiles_call(rel_table):
    t = np.arange(ATT_TILE)
    d = t[:, None] - t[None, :]
    buckets = np.stack([_rel_bucket_np(d), _rel_bucket_np(d + ATT_TILE)]).astype(np.int32)
    return pl.pallas_call(
        _bias_body,
        grid=(N_HEADS,),
        in_specs=[pl.BlockSpec((2, ATT_TILE, ATT_TILE), lambda h: (0, 0, 0)),
                  pl.BlockSpec(memory_space=pltpu.SMEM)],
        out_specs=pl.BlockSpec((1, 2, ATT_TILE, ATT_TILE), lambda h: (h, 0, 0, 0)),
        out_shape=jax.ShapeDtypeStruct((N_HEADS, 2, ATT_TILE, ATT_TILE), F32),
        compiler_params=_params(("arbitrary",)),
        name="bias_tiles",
    )(jnp.asarray(buckets), rel_table)


def _moba_select_body(q_ref, k_ref, o_ref, kbar_scr, code_scr):
    i = pl.program_id(1)
    nblk = kbar_scr.shape[0]

    @pl.when(i == 0)
    def _():
        k = k_ref[0]
        kbar_scr[...] = jnp.mean(k.reshape(nblk, MOBA_BLOCK, KV_DIM), axis=1)
        code_scr[...] = jnp.zeros(code_scr.shape, F32)

    blk = lax.broadcasted_iota(I32, (nblk, ATT_TILE), 0)
    for h in range(N_HEADS):
        kv = h // GQA_GROUP
        gate = lax.dot_general(kbar_scr[:, kv * HEAD_DIM:(kv + 1) * HEAD_DIM],
                               q_ref[0, :, h * HEAD_DIM:(h + 1) * HEAD_DIM],
                               (((1,), (1,)), ((), ())), precision=lax.Precision.HIGHEST,
                               preferred_element_type=F32)
        gate = jnp.where(blk < i, gate, NEG)
        code = jnp.zeros((1, ATT_TILE), F32)
        for _ in range(MOBA_TOPK):
            mx = jnp.max(gate, axis=0, keepdims=True)
            idx = jnp.min(jnp.where(gate == mx, blk, nblk), axis=0, keepdims=True)
            code = code + jnp.where(idx < i, jnp.exp2(idx.astype(F32)), 0.0)
            gate = jnp.where(blk == idx, -jnp.inf, gate)
        code_scr[h:h + 1, :] = code
    o_ref[0] = code_scr[...].T


def moba_select_call(z3, n_blocks):
    b, s, _ = z3.shape
    return pl.pallas_call(
        _moba_select_body,
        grid=(b, s // ATT_TILE),
        in_specs=[pl.BlockSpec((1, ATT_TILE, Q_DIM), lambda bb, i: (bb, i, 0)),
                  pl.BlockSpec((1, s, KV_DIM), lambda bb, i: (bb, 0, Q_DIM // KV_DIM))],
        out_specs=pl.BlockSpec((1, ATT_TILE, LANES), lambda bb, i: (bb, i, 0)),
        out_shape=jax.ShapeDtypeStruct((b, s, LANES), F32),
        scratch_shapes=[pltpu.VMEM((n_blocks, KV_DIM), F32), pltpu.VMEM((LANES, ATT_TILE), F32)],
        compiler_params=_params(("arbitrary", "arbitrary")),
        name="moba_select",
    )(z3, z3)


def _flash_update(kv, q4, kb, vb, bias4, mask3, m_scr, l_scr, acc_scr):
    s = lax.dot_general(q4, kb, (((1,), (1,)), ((), ())), preferred_element_type=F32)
    s = s.reshape(GQA_GROUP, ATT_TILE, ATT_TILE) * ATTN_SCALE + bias4
    s = jnp.where(mask3, s, NEG)
    m_prev = m_scr[kv]
    m_new = jnp.maximum(m_prev, jnp.max(s, axis=-1, keepdims=True))
    alpha = jnp.exp(m_prev - m_new)
    p = jnp.exp(s - m_new)
    l_scr[kv] = alpha * l_scr[kv] + jnp.sum(p, axis=-1, keepdims=True)
    pv = jnp.dot(p.reshape(GQA_GROUP * ATT_TILE, ATT_TILE).astype(BF16), vb, preferred_element_type=F32)
    acc_scr[kv] = alpha * acc_scr[kv] + pv.reshape(GQA_GROUP, ATT_TILE, HEAD_DIM)
    m_scr[kv] = m_new


def _attn_body(q_ref, k_ref, v_ref, sel_ref, thr_ref, bias_ref, far_ref, o_ref, m_scr, l_scr, acc_scr, *, dsa):
    i = pl.program_id(1)
    j = pl.program_id(2)

    @pl.when(j == 0)
    def _():
        m_scr[...] = jnp.full(m_scr.shape, NEG, F32)
        l_scr[...] = jnp.zeros(l_scr.shape, F32)
        acc_scr[...] = jnp.zeros(acc_scr.shape, F32)

    def step(which):
        causal = (lax.broadcasted_iota(I32, (ATT_TILE, ATT_TILE), 1)
                  <= lax.broadcasted_iota(I32, (ATT_TILE, ATT_TILE), 0))
        if dsa:
            picked = jnp.where(sel_ref[0] >= thr_ref[0], 1.0, 0.0).T > 0.5
            mask_all = (picked & causal if which == 0 else picked)[None]
        for kv in range(N_KV_HEADS):
            heads = range(kv * GQA_GROUP, (kv + 1) * GQA_GROUP)
            q4 = jnp.concatenate([q_ref[0, :, h * HEAD_DIM:(h + 1) * HEAD_DIM] for h in heads],
                                 axis=0).astype(BF16)
            kb = k_ref[0, :, kv * HEAD_DIM:(kv + 1) * HEAD_DIM].astype(BF16)
            vb = v_ref[0, :, kv * HEAD_DIM:(kv + 1) * HEAD_DIM].astype(BF16)
            if which == 2:
                bias4 = far_ref[kv * GQA_GROUP:(kv + 1) * GQA_GROUP, :, 0:1]
            else:
                bias4 = bias_ref[kv * GQA_GROUP:(kv + 1) * GQA_GROUP, which]
            if dsa:
                mask3 = mask_all
            elif which == 0:
                mask3 = causal[None]
            else:
                code = sel_ref[0]
                bits = [(jnp.right_shift(code[:, h:h + 1].astype(I32), j) & 1) > 0 for h in heads]
                mask3 = jnp.stack(bits)
            _flash_update(kv, q4, kb, vb, bias4, mask3, m_scr, l_scr, acc_scr)

    @pl.when(j == i)
    def _():
        step(0)

    @pl.when(j == i - 1)
    def _():
        step(1)

    @pl.when(j < i - 1)
    def _():
        step(2)

    @pl.when(j == i)
    def _():
        for kv in range(N_KV_HEADS):
            out = acc_scr[kv] / l_scr[kv]
            for g in range(GQA_GROUP):
                h = kv * GQA_GROUP + g
                o_ref[0, :, h * HEAD_DIM:(h + 1) * HEAD_DIM] = out[g].astype(o_ref.dtype)


def attn_call(z3, sel, thr, bias, far, *, dsa):
    b, s, _ = z3.shape
    nt = s // ATT_TILE
    if dsa:
        sel_spec = pl.BlockSpec((1, ATT_TILE, ATT_TILE), lambda bb, i, j: (bb, jnp.minimum(j, i), i))
    else:
        sel_spec = pl.BlockSpec((1, ATT_TILE, LANES), lambda bb, i, j: (bb, i, 0))
    kcol, vcol = Q_DIM // KV_DIM, Q_DIM // KV_DIM + 1
    return pl.pallas_call(
        functools.partial(_attn_body, dsa=dsa),
        grid=(b, nt, nt),
        in_specs=[pl.BlockSpec((1, ATT_TILE, Q_DIM), lambda bb, i, j: (bb, i, 0)),
                  pl.BlockSpec((1, ATT_TILE, KV_DIM), lambda bb, i, j: (bb, jnp.minimum(j, i), kcol)),
                  pl.BlockSpec((1, ATT_TILE, KV_DIM), lambda bb, i, j: (bb, jnp.minimum(j, i), vcol)),
                  sel_spec,
                  pl.BlockSpec((1, 1, ATT_TILE), lambda bb, i, j: (bb, 0, i)),
                  pl.BlockSpec((N_HEADS, 2, ATT_TILE, ATT_TILE), lambda bb, i, j: (0, 0, 0, 0)),
                  pl.BlockSpec((N_HEADS, 1, LANES), lambda bb, i, j: (0, 0, 0))],
        out_specs=pl.BlockSpec((1, ATT_TILE, Q_DIM), lambda bb, i, j: (bb, i, 0)),
        out_shape=jax.ShapeDtypeStruct((b, s, Q_DIM), BF16),
        scratch_shapes=[pltpu.VMEM((N_KV_HEADS, GQA_GROUP, ATT_TILE, 1), F32),
                        pltpu.VMEM((N_KV_HEADS, GQA_GROUP, ATT_TILE, 1), F32),
                        pltpu.VMEM((N_KV_HEADS, GQA_GROUP, ATT_TILE, HEAD_DIM), F32)],
        compiler_params=_params(("arbitrary", "arbitrary", "arbitrary")),
        name="dsa_attn" if dsa else "moba_attn",
    )(z3, z3, z3, sel, thr, bias, far)


def _sortable_key(x):
    bits = pltpu.bitcast(x, I32)
    return bits ^ (jnp.right_shift(bits, 31) & 0x7FFFFFFF)


def _kth_largest_key(count_ge, k, shape):
    def bit_body(bi, t):
        cand = t ^ jnp.left_shift(jnp.int32(1), 31 - bi)
        return jnp.where(count_ge(cand) >= k, cand, t)
    return lax.fori_loop(0, 32, bit_body, jnp.full(shape, -2 ** 31, I32))


def _dsa_score_body(qi_ref, tail_ref, wi_ref, key_ref, thr_ref, key_scr, *, n_sel):
    i = pl.program_id(1)
    j = pl.program_id(2)

    @pl.when(j <= i)
    def _():
        t = tail_ref[0]
        lane = lax.broadcasted_iota(I32, t.shape, 1)
        klo = jnp.where(lane < IDX_DIM, t, 0.0)
        khi = pltpu.roll(klo, IDX_DIM, 1)
        qstk = jnp.concatenate([qi_ref[0, :, p * LANES:(p + 1) * LANES] for p in range(IDX_HEADS // 2)],
                               axis=0).astype(BF16)
        dn = (((1,), (1,)), ((), ()))
        s_lo = lax.dot_general(klo.astype(BF16), qstk, dn, preferred_element_type=F32)
        s_hi = lax.dot_general(khi.astype(BF16), qstk, dn, preferred_element_type=F32)
        sc = jnp.zeros((ATT_TILE, ATT_TILE), F32)
        for p in range(IDX_HEADS // 2):
            sl = slice(p * ATT_TILE, (p + 1) * ATT_TILE)
            sc = sc + jnp.maximum(s_lo[:, sl] * IDX_SCALE, 0.0) * wi_ref[0, 2 * p:2 * p + 1, :]
            sc = sc + jnp.maximum(s_hi[:, sl] * IDX_SCALE, 0.0) * wi_ref[0, 2 * p + 1:2 * p + 2, :]
        sc = sc * IDX_W_SCALE
        kpos = j * ATT_TILE + lax.broadcasted_iota(I32, sc.shape, 0)
        qpos = i * ATT_TILE + lax.broadcasted_iota(I32, sc.shape, 1)
        key = _sortable_key(jnp.where(kpos <= qpos, sc, NEG))
        key_ref[0] = key
        key_scr[pl.ds(pl.multiple_of(j * ATT_TILE, ATT_TILE), ATT_TILE), :] = key

    @pl.when(j > i)
    def _():
        key_ref[0] = jnp.full((ATT_TILE, ATT_TILE), -2 ** 31, I32)

    @pl.when(j == i)
    def _():
        def count_ge(cand):
            def chunk(c, cnt):
                kc = key_scr[pl.ds(pl.multiple_of(c * ATT_TILE, ATT_TILE), ATT_TILE), :]
                hit = jnp.where(kc >= cand, 1, 0).astype(I32)
                return cnt + jnp.sum(hit.reshape(ATT_TILE // 8, 8, ATT_TILE), axis=0)
            cnt8 = lax.fori_loop(0, i + 1, chunk, jnp.zeros((8, ATT_TILE), I32))
            return jnp.sum(cnt8, axis=0, keepdims=True)
        thr_ref[0] = _kth_largest_key(count_ge, n_sel, (1, ATT_TILE))


def dsa_score_call(z3, tail3, wi_t, n_sel):
    b, s, _ = z3.shape
    nt = s // ATT_TILE
    return pl.pallas_call(
        functools.partial(_dsa_score_body, n_sel=n_sel),
        grid=(b, nt, nt),
        in_specs=[pl.BlockSpec((1, ATT_TILE, IDX_Q_DIM), lambda bb, i, j: (bb, i, IN_DIM_A // IDX_Q_DIM)),
                  pl.BlockSpec((1, ATT_TILE, LANES), lambda bb, i, j: (bb, jnp.minimum(j, i), 0)),
                  pl.BlockSpec((1, IDX_HEADS, ATT_TILE), lambda bb, i, j: (bb, 0, i))],
        out_specs=[pl.BlockSpec((1, ATT_TILE, ATT_TILE), lambda bb, i, j: (bb, j, i)),
                   pl.BlockSpec((1, 1, ATT_TILE), lambda bb, i, j: (bb, 0, i))],
        out_shape=[jax.ShapeDtypeStruct((b, s, s), I32), jax.ShapeDtypeStruct((b, 1, s), I32)],
        scratch_shapes=[pltpu.VMEM((s, ATT_TILE), I32)],
        compiler_params=_params(("arbitrary", "arbitrary", "arbitrary")),
        name="dsa_score",
    )(z3, tail3, wi_t)


def _router_body(x_ref, g_ref, sc_ref, sh_ref, wr_ref, br_ref, h_ref, r_ref):
    h = _modulated_norm(x_ref[...], g_ref[...], sc_ref[0], sh_ref[0])
    h_ref[...] = h.astype(h_ref.dtype)
    lg = jnp.dot(h, wr_ref[...], precision=lax.Precision.HIGHEST, preferred_element_type=F32) + br_ref[...]
    lane = lax.broadcasted_iota(I32, lg.shape, 1)
    gl = jnp.where(lane < N_GROUPS, lg, -jnp.inf)
    gmax = jnp.max(gl, axis=-1, keepdims=True)
    g_sel = jnp.min(jnp.where(gl == gmax, lane, LANES), axis=-1, keepdims=True)
    g_w = 1.0 / jnp.sum(jnp.exp(gl - gmax), axis=-1, keepdims=True)
    lo = N_GROUPS + EXPERTS_PER_GROUP * g_sel
    el = jnp.where((lane >= lo) & (lane < lo + EXPERTS_PER_GROUP), lg, -jnp.inf)
    e1 = jnp.max(el, axis=-1, keepdims=True)
    i1 = jnp.min(jnp.where(el == e1, lane, LANES), axis=-1, keepdims=True)
    el2 = jnp.where(lane == i1, -jnp.inf, el)
    e2 = jnp.max(el2, axis=-1, keepdims=True)
    i2 = jnp.min(jnp.where(el2 == e2, lane, LANES), axis=-1, keepdims=True)
    v2 = jnp.exp(e2 - e1)
    w1 = g_w / (1.0 + v2)
    w2 = g_w * v2 / (1.0 + v2)
    r_ref[...] = jnp.where(lane == 0, (i1 - N_GROUPS).astype(F32),
                           jnp.where(lane == 1, (i2 - N_GROUPS).astype(F32),
                                     jnp.where(lane == 2, w1, jnp.where(lane == 3, w2, 0.0))))


def router_call(x2d, g, sc, sh, w_route, b_route, *, tm, rows_per_mod):
    m, d = x2d.shape
    r = sc.shape[1]
    tiles_per_mod = rows_per_mod // tm
    mod_spec = pl.BlockSpec((1, r, d), lambda i: (i // tiles_per_mod, 0, 0))
    return pl.pallas_call(
        _router_body,
        grid=(m // tm,),
        in_specs=[pl.BlockSpec((tm, d), lambda i: (i, 0)),
                  pl.BlockSpec((1, d), lambda i: (0, 0)),
                  mod_spec, mod_spec,
                  pl.BlockSpec((d, LANES), lambda i: (0, 0)),
                  pl.BlockSpec((1, LANES), lambda i: (0, 0))],
        out_specs=[pl.BlockSpec((tm, d), lambda i: (i, 0)), pl.BlockSpec((tm, LANES), lambda i: (i, 0))],
        out_shape=[jax.ShapeDtypeStruct((m, d), F32), jax.ShapeDtypeStruct((m, LANES), F32)],
        compiler_params=_params(("arbitrary",)),
        name="router",
    )(x2d, g.reshape(1, d), sc, sh, w_route, b_route)


def _ffn_body(be_ref, nu_ref, x_ref, wg_ref, wu_ref, wd_ref, o_ref):
    i = pl.program_id(0)

    @pl.when(i < nu_ref[0])
    def _():
        x = x_ref[...].astype(BF16)
        acc = jnp.zeros(o_ref.shape, F32)
        for c in range(D_EXPERT // FF_CHUNK):
            sl = slice(c * FF_CHUNK, (c + 1) * FF_CHUNK)
            a = jnp.dot(x, wg_ref[0, :, sl].astype(BF16), preferred_element_type=F32)
            u = jnp.dot(x, wu_ref[0, :, sl].astype(BF16), preferred_element_type=F32)
            act = a * (1.0 / (1.0 + jnp.exp(-a))) * u
            acc = acc + jnp.dot(act.astype(BF16), wd_ref[0, sl, :].astype(BF16), preferred_element_type=F32)
        o_ref[...] = acc

    @pl.when(i >= nu_ref[0])
    def _():
        o_ref[...] = jnp.zeros(o_ref.shape, F32)


def ffn_call(blk_e, n_used, buf, w_gate, w_up, w_down):
    rows, d = buf.shape
    nblk = rows // MOE_BLOCK
    last = lambda i, nu: jnp.minimum(i, nu[0] - 1)
    grid_spec = pltpu.PrefetchScalarGridSpec(
        num_scalar_prefetch=2,
        grid=(nblk,),
        in_specs=[pl.BlockSpec((MOE_BLOCK, d), lambda i, be, nu: (last(i, nu), 0)),
                  pl.BlockSpec((1, d, D_EXPERT), lambda i, be, nu: (be[last(i, nu)], 0, 0)),
                  pl.BlockSpec((1, d, D_EXPERT), lambda i, be, nu: (be[last(i, nu)], 0, 0)),
                  pl.BlockSpec((1, D_EXPERT, d), lambda i, be, nu: (be[last(i, nu)], 0, 0))],
        out_specs=pl.BlockSpec((MOE_BLOCK, d), lambda i, be, nu: (i, 0)),
    )
    return pl.pallas_call(
        _ffn_body,
        grid_spec=grid_spec,
        out_shape=jax.ShapeDtypeStruct((rows, d), F32),
        compiler_params=_params(("arbitrary",)),
        name="expert_ffn",
    )(blk_e, n_used, buf, w_gate, w_up, w_down)


def _row_dma_wait(ref_hbm, dst, n_rows, sem):
    pltpu.make_async_copy(ref_hbm.at[pl.ds(0, n_rows)], dst, sem).wait()


def _dispatch_body(dest_ref, h_ref, buf_in_ref, buf_ref, sems, *, chunk):
    del buf_in_ref
    c = pl.program_id(0)
    slot = c % 2

    def start(r, carry):
        tok = lax.shift_right_logical(c * chunk + r, 1)
        pltpu.make_async_copy(h_ref.at[pl.ds(tok, 1)], buf_ref.at[pl.ds(dest_ref[0, 0, r], 1)],
                              sems.at[slot]).start()
        return carry

    lax.fori_loop(0, chunk, start, 0)

    @pl.when(c > 0)
    def _():
        _row_dma_wait(buf_ref, buf_ref.at[pl.ds(0, chunk)], chunk, sems.at[1 - slot])

    @pl.when(c == pl.num_programs(0) - 1)
    def _():
        _row_dma_wait(buf_ref, buf_ref.at[pl.ds(0, chunk)], chunk, sems.at[slot])


def dispatch_call(h, dest, buf):
    m, d = h.shape
    a = dest.shape[0]
    chunk = math.gcd(a, 512)
    return pl.pallas_call(
        functools.partial(_dispatch_body, chunk=chunk),
        grid=(a // chunk,),
        in_specs=[pl.BlockSpec((1, 1, chunk), lambda c: (c, 0, 0), memory_space=pltpu.SMEM),
                  pl.BlockSpec(memory_space=pl.ANY),
                  pl.BlockSpec(memory_space=pl.ANY)],
        out_specs=pl.BlockSpec(memory_space=pl.ANY),
        out_shape=jax.ShapeDtypeStruct(buf.shape, buf.dtype),
        scratch_shapes=[pltpu.SemaphoreType.DMA((2,))],
        input_output_aliases={2: 0},
        compiler_params=_params(("arbitrary",)),
        name="moe_dispatch",
    )(dest.reshape(a // chunk, 1, chunk), h, buf)


def _combine_body(dcur_ref, dnxt_ref, x_ref, rt_ref, gt_ref, ob_ref, o_ref, rbuf, sems, *, tm):
    t = pl.program_id(0)
    slot = t % 2

    def issue(dref, sl):
        def start(r, carry):
            dst = (r & 1) * tm + lax.shift_right_logical(r, 1)
            pltpu.make_async_copy(ob_ref.at[pl.ds(dref[0, 0, r], 1)], rbuf.at[sl, pl.ds(dst, 1)],
                                  sems.at[sl]).start()
            return carry
        lax.fori_loop(0, 2 * tm, start, 0)

    @pl.when(t == 0)
    def _():
        issue(dcur_ref, 0)

    @pl.when(t + 1 < pl.num_programs(0))
    def _():
        issue(dnxt_ref, 1 - slot)

    _row_dma_wait(ob_ref, rbuf.at[slot], 2 * tm, sems.at[slot])
    w0 = rt_ref[:, 2:3]
    w1 = rt_ref[:, 3:4]
    o_ref[...] = x_ref[...] + gt_ref[0] * (w0 * rbuf[slot, 0:tm] + w1 * rbuf[slot, tm:2 * tm])


def combine_call(x2d, out_buf, dest, route, gt, *, tm, rows_per_mod):
    m, d = x2d.shape
    r = gt.shape[1]
    nt = m // tm
    tiles_per_mod = rows_per_mod // tm
    dest3 = dest.reshape(nt, 1, 2 * tm)
    return pl.pallas_call(
        functools.partial(_combine_body, tm=tm),
        grid=(nt,),
        in_specs=[pl.BlockSpec((1, 1, 2 * tm), lambda t: (t, 0, 0), memory_space=pltpu.SMEM),
                  pl.BlockSpec((1, 1, 2 * tm), lambda t: (jnp.minimum(t + 1, nt - 1), 0, 0),
                               memory_space=pltpu.SMEM),
                  pl.BlockSpec((tm, d), lambda t: (t, 0)),
                  pl.BlockSpec((tm, LANES), lambda t: (t, 0)),
                  pl.BlockSpec((1, r, d), lambda t: (t // tiles_per_mod, 0, 0)),
                  pl.BlockSpec(memory_space=pl.ANY)],
        out_specs=pl.BlockSpec((tm, d), lambda t: (t, 0)),
        out_shape=jax.ShapeDtypeStruct((m, d), F32),
        scratch_shapes=[pltpu.VMEM((2, 2 * tm, d), F32), pltpu.SemaphoreType.DMA((2,))],
        compiler_params=_params(("arbitrary",)),
        name="moe_combine",
    )(dest3, dest3, x2d, route, gt, out_buf)


def moe_dispatch(eid):
    a = eid.size
    flat_e = eid.reshape(a)
    order = jnp.argsort(flat_e, stable=True)
    rank = jnp.argsort(order)
    counts = jnp.bincount(flat_e, length=N_EXPERTS)
    padded = (counts + MOE_BLOCK - 1) // MOE_BLOCK * MOE_BLOCK
    pend = jnp.cumsum(padded)
    pstart = pend - padded
    cstart = jnp.cumsum(counts) - counts
    dest = (pstart[flat_e] + rank - cstart[flat_e]).astype(I32)
    nblk = -(-(a + N_EXPERTS * (MOE_BLOCK - 1)) // MOE_BLOCK)
    blk_e = jnp.minimum(jnp.searchsorted(pend, jnp.arange(nblk) * MOE_BLOCK, side='right'),
                        N_EXPERTS - 1).astype(I32)
    n_used = (pend[-1] // MOE_BLOCK).astype(I32).reshape(1)
    return dest, blk_e, n_used


def moe_layer(xs, mods, rows_per_mods, g_ffn, w_grp, b_grp, w_rt, b_rt, w_gate, w_up, w_down, tms):
    d = xs[0].shape[1]
    n_route = N_GROUPS + N_EXPERTS
    w_route = jnp.pad(jnp.concatenate([w_grp, w_rt], axis=1), ((0, 0), (0, LANES - n_route)))
    b_route = jnp.pad(jnp.concatenate([b_grp, b_rt]), (0, LANES - n_route)).reshape(1, LANES)
    hs, routes = [], []
    for x, (sc, sh, _), rpm, tm in zip(xs, mods, rows_per_mods, tms):
        h, rt = router_call(x, g_ffn, sc, sh, w_route, b_route, tm=tm, rows_per_mod=rpm)
        hs.append(h)
        routes.append(rt)
    route = jnp.concatenate(routes) if len(routes) > 1 else routes[0]
    dest, blk_e, n_used = moe_dispatch(route[:, :2].astype(I32))
    buf = jnp.zeros((blk_e.shape[0] * MOE_BLOCK, d), F32)
    off = 0
    for h in hs:
        buf = dispatch_call(h, dest[2 * off:2 * (off + h.shape[0])], buf)
        off += h.shape[0]
    out_buf = ffn_call(blk_e, n_used, buf, w_gate, w_up, w_down)
    outs, off = [], 0
    for x, (_, _, gt), rt, rpm, tm in zip(xs, mods, routes, rows_per_mods, tms):
        m = x.shape[0]
        outs.append(combine_call(x, out_buf, dest[2 * off:2 * (off + m)], rt, gt,
                                 tm=min(tm, MOE_BLOCK), rows_per_mod=rpm))
        off += m
    return outs


MEAN_PAGES = 8
SCORE_PAGES = 16


def _page_spec(block, layer, slot_of):
    def index_map(*args):
        ids, pages = args[:2], args[2]
        return (layer, pages[slot_of(*ids)]) + (0,) * (len(block) - 2)
    return pl.BlockSpec(block, index_map)


def _block_mean_body(pt_ref, *refs):
    pages, o_ref = refs[:MEAN_PAGES], refs[MEAN_PAGES]
    per_blk = MOBA_BLOCK // PAGE_SIZE
    for blk in range(MEAN_PAGES // per_blk):
        tot = sum(jnp.sum(pages[blk * per_blk + p][0, 0], axis=0) for p in range(per_blk))
        o_ref[0, 0, blk] = tot * (1.0 / MOBA_BLOCK)


def paged_block_mean_call(cache, layer, pt_flat, nb, n_pages):
    per_step = MEAN_PAGES * PAGE_SIZE // MOBA_BLOCK
    specs = [_page_spec((1, 1, PAGE_SIZE, N_KV_HEADS, HEAD_DIM), layer, functools.partial(
        lambda p, b, g: b * n_pages + g * MEAN_PAGES + p, p)) for p in range(MEAN_PAGES)]
    out = pl.pallas_call(
        _block_mean_body,
        grid_spec=pltpu.PrefetchScalarGridSpec(
            num_scalar_prefetch=1, grid=(nb, n_pages // MEAN_PAGES), in_specs=specs,
            out_specs=pl.BlockSpec((1, 1, per_step, N_KV_HEADS, HEAD_DIM), lambda b, g, pt: (b, g, 0, 0, 0))),
        out_shape=jax.ShapeDtypeStruct((nb, n_pages // MEAN_PAGES, per_step, N_KV_HEADS, HEAD_DIM), F32),
        compiler_params=_params(("arbitrary", "arbitrary")),
        name="paged_block_mean",
    )(pt_flat, *([cache] * MEAN_PAGES))
    return out.reshape(nb, n_pages * PAGE_SIZE // MOBA_BLOCK, N_KV_HEADS, HEAD_DIM)


def _sample_select_body(q_ref, kbar_ref, o_ref, *, own):
    q = q_ref[0]
    row = lax.broadcasted_iota(I32, (N_HEADS, LANES), 0)
    lane = lax.broadcasted_iota(I32, (N_HEADS, LANES), 1)
    gate = jnp.zeros((N_HEADS, LANES), F32)
    for kv in range(N_KV_HEADS):
        gk = lax.dot_general(q, kbar_ref[0, :, kv, :], (((1,), (1,)), ((), ())),
                             precision=lax.Precision.HIGHEST, preferred_element_type=F32)
        gate = jnp.where(row // GQA_GROUP == kv, gk, gate)
    gate = jnp.where(lane < own, gate, NEG)
    out = jnp.zeros((N_HEADS, LANES), I32)
    for c in range(MOBA_TOPK):
        mx = jnp.max(gate, axis=-1, keepdims=True)
        idx = jnp.min(jnp.where(gate == mx, lane, LANES), axis=-1, keepdims=True)
        out = jnp.where(lane == c, idx, out)
        gate = jnp.where(lane == idx, -jnp.inf, gate)
    o_ref[0] = out


def sample_moba_select_call(q3, kbar_pad, own):
    nb = q3.shape[0]
    return pl.pallas_call(
        functools.partial(_sample_select_body, own=own),
        grid=(nb,),
        in_specs=[pl.BlockSpec((1, N_HEADS, HEAD_DIM), lambda b: (b, 0, 0)),
                  pl.BlockSpec((1, LANES, N_KV_HEADS, HEAD_DIM), lambda b: (b, 0, 0, 0))],
        out_specs=pl.BlockSpec((1, N_HEADS, LANES), lambda b: (b, 0, 0)),
        out_shape=jax.ShapeDtypeStruct((nb, N_HEADS, LANES), I32),
        compiler_params=_params(("arbitrary",)),
        name="sample_moba_select",
    )(q3, kbar_pad)


SEL_PAGES = MOBA_TOPK * (MOBA_BLOCK // PAGE_SIZE)


def _sample_moba_attn_body(pg_ref, blk_ref, q_ref, kn_ref, vn_ref, prev_ref, far_ref, own_ref, kc_ref, vc_ref,
                           o_ref, kbuf, vbuf, sems, *, own, layer):
    b = pl.program_id(0)
    per_blk = MOBA_BLOCK // PAGE_SIZE
    copies = []
    for h in range(N_HEADS):
        kv = h // GQA_GROUP
        for s in range(SEL_PAGES):
            page = pg_ref[(b * N_HEADS + h) * SEL_PAGES + s]
            rows = pl.ds(s * PAGE_SIZE, PAGE_SIZE)
            copies.append(pltpu.make_async_copy(kc_ref.at[layer, page, :, kv, :], kbuf.at[h, rows], sems.at[0]))
            copies.append(pltpu.make_async_copy(vc_ref.at[layer, page, :, kv, :], vbuf.at[h, rows], sems.at[1]))
    for cp in copies:
        cp.start()
    for cp in copies:
        cp.wait()

    for h in range(N_HEADS):
        kv = h // GQA_GROUP
        q = q_ref[0, h:h + 1, :]
        q8 = jnp.broadcast_to(q, (8, HEAD_DIM)).astype(BF16)
        s = lax.dot_general(q8, kbuf[h].astype(BF16), (((1,), (1,)), ((), ())),
                            preferred_element_type=F32)
        far = jnp.broadcast_to(far_ref[h, :, 0:1], (1, MOBA_BLOCK))
        bias = jnp.concatenate(
            [jnp.where(blk_ref[(b * N_HEADS + h) * MOBA_TOPK + n] == own - 1, prev_ref[h], far)
             for n in range(MOBA_TOPK)], axis=1)
        s = s * ATTN_SCALE + bias
        s_own = (jnp.sum(q * kn_ref[0, kv:kv + 1, :], axis=-1, keepdims=True) * ATTN_SCALE
                 + own_ref[h, :, 0:1])
        m = jnp.maximum(jnp.max(s, axis=-1, keepdims=True), s_own)
        p = jnp.exp(s - m)
        p_own = jnp.exp(s_own - m)
        l = jnp.sum(p, axis=-1, keepdims=True) + p_own
        acc = jnp.dot(p.astype(BF16), vbuf[h].astype(BF16), preferred_element_type=F32)
        o_ref[0, h:h + 1, :] = ((acc + p_own * vn_ref[0, kv:kv + 1, :]) / l)[0:1]


def sample_moba_attn_call(q3, k_new, v_new, cache_k, cache_v, layer, pages, blks, prev_row, far, own_bias, own):
    nb = q3.shape[0]
    whole = lambda shape: pl.BlockSpec(shape, lambda b, pg, bl: (0,) * len(shape))
    out = pl.pallas_call(
        functools.partial(_sample_moba_attn_body, own=own, layer=layer),
        grid_spec=pltpu.PrefetchScalarGridSpec(
            num_scalar_prefetch=2, grid=(nb,),
            in_specs=[pl.BlockSpec((1, N_HEADS, HEAD_DIM), lambda b, pg, bl: (b, 0, 0)),
                      pl.BlockSpec((1, N_KV_HEADS, HEAD_DIM), lambda b, pg, bl: (b, 0, 0)),
                      pl.BlockSpec((1, N_KV_HEADS, HEAD_DIM), lambda b, pg, bl: (b, 0, 0)),
                      whole((N_HEADS, 1, MOBA_BLOCK)), whole((N_HEADS, 1, LANES)), whole((N_HEADS, 1, LANES)),
                      pl.BlockSpec(memory_space=pl.ANY), pl.BlockSpec(memory_space=pl.ANY)],
            out_specs=pl.BlockSpec((1, N_HEADS, HEAD_DIM), lambda b, pg, bl: (b, 0, 0)),
            scratch_shapes=[pltpu.VMEM((N_HEADS, SEL_PAGES * PAGE_SIZE, HEAD_DIM), F32),
                            pltpu.VMEM((N_HEADS, SEL_PAGES * PAGE_SIZE, HEAD_DIM), F32),
                            pltpu.SemaphoreType.DMA((2,))]),
        out_shape=jax.ShapeDtypeStruct((nb, N_HEADS, HEAD_DIM), F32),
        compiler_params=_params(("arbitrary",)),
        name="sample_moba_attn",
    )(pages, blks, q3, k_new, v_new, prev_row, far, own_bias, cache_k, cache_v)
    return out.reshape(nb, Q_DIM)


def _indexer_score(qi, wi_col, keys_pad):
    s = lax.dot_general(qi.astype(BF16), keys_pad.astype(BF16), (((1,), (1,)), ((), ())),
                        preferred_element_type=F32)
    return jnp.sum(jnp.maximum(s * IDX_SCALE, 0.0) * wi_col, axis=0, keepdims=True) * IDX_W_SCALE


def _sample_score_body(pt_ref, qi_ref, wi_ref, *refs):
    pages, o_ref = refs[:SCORE_PAGES], refs[SCORE_PAGES]
    pad = jnp.zeros((PAGE_SIZE, LANES - IDX_DIM), F32)
    for p in range(SCORE_PAGES):
        keys = jnp.concatenate([pages[p][0, 0], pad], axis=1)
        o_ref[0, p:p + 1, :] = _indexer_score(qi_ref[0], wi_ref[0], keys)


def sample_dsa_score_call(cache_kidx, layer, pt_flat, qi_pad, wi_col, n_pages):
    nb = qi_pad.shape[0]
    specs = [_page_spec((1, 1, PAGE_SIZE, IDX_DIM), layer, functools.partial(
        lambda p, b, g: b * n_pages + g * SCORE_PAGES + p, p)) for p in range(SCORE_PAGES)]
    return pl.pallas_call(
        _sample_score_body,
        grid_spec=pltpu.PrefetchScalarGridSpec(
            num_scalar_prefetch=1, grid=(nb, n_pages // SCORE_PAGES),
            in_specs=[pl.BlockSpec((1, IDX_HEADS, LANES), lambda b, g, pt: (b, 0, 0)),
                      pl.BlockSpec((1, IDX_HEADS, 1), lambda b, g, pt: (b, 0, 0))] + specs,
            out_specs=pl.BlockSpec((1, SCORE_PAGES, PAGE_SIZE), lambda b, g, pt: (b, g, 0))),
        out_shape=jax.ShapeDtypeStruct((nb, n_pages, PAGE_SIZE), F32),
        compiler_params=_params(("arbitrary", "arbitrary")),
        name="sample_dsa_score",
    )(pt_flat, qi_pad, wi_col, *([cache_kidx] * SCORE_PAGES))


def _sample_dsa_select_body(sc_ref, qi_ref, wi_ref, ki_ref, m_ref, mnew_ref, *, n_sel):
    keys = _sortable_key(sc_ref[0])
    key_new = _sortable_key(_indexer_score(qi_ref[0], wi_ref[0], jnp.broadcast_to(ki_ref[0], (8, LANES))))[:, 0:1]

    def count_ge(cand):
        hit = jnp.where(keys >= cand, 1, 0).astype(I32)
        tot = jnp.sum(jnp.sum(hit, axis=0, keepdims=True), axis=1, keepdims=True)
        return tot + jnp.where(key_new >= cand, 1, 0).astype(I32)

    thr = _kth_largest_key(count_ge, n_sel, (1, 1))
    m_ref[0] = jnp.where(keys >= thr, 1.0, 0.0)
    mnew_ref[0] = jnp.broadcast_to(jnp.where(key_new >= thr, 1.0, 0.0), (1, LANES))


def sample_dsa_select_call(scores, qi_pad, wi_col, ki_pad, n_sel):
    nb, n_pages, _ = scores.shape
    return pl.pallas_call(
        functools.partial(_sample_dsa_select_body, n_sel=n_sel),
        grid=(nb,),
        in_specs=[pl.BlockSpec((1, n_pages, PAGE_SIZE), lambda b: (b, 0, 0)),
                  pl.BlockSpec((1, IDX_HEADS, LANES), lambda b: (b, 0, 0)),
                  pl.BlockSpec((1, IDX_HEADS, 1), lambda b: (b, 0, 0)),
                  pl.BlockSpec((1, 1, LANES), lambda b: (b, 0, 0))],
        out_specs=[pl.BlockSpec((1, n_pages, PAGE_SIZE), lambda b: (b, 0, 0)),
                   pl.BlockSpec((1, 1, LANES), lambda b: (b, 0, 0))],
        out_shape=[jax.ShapeDtypeStruct((nb, n_pages, PAGE_SIZE), F32), jax.ShapeDtypeStruct((nb, 1, LANES), F32)],
        compiler_params=_params(("arbitrary",)),
        name="sample_dsa_select",
    )(scores, qi_pad, wi_col, ki_pad)


def _row_copy(cache_hbm, layer, dst_vmem, row, slot, sem):
    page = lax.shift_right_logical(row, PAGE_SIZE.bit_length() - 1)
    off = row & (PAGE_SIZE - 1)
    return pltpu.make_async_copy(cache_hbm.at[layer, page, pl.ds(off, 1)], dst_vmem.at[pl.ds(slot, 1)], sem)


def _sample_dsa_attn_body(rows_ref, q_ref, valid_ref, bucket_ref, tab_ref, kn_ref, vn_ref, new_ref,
                          kc_ref, vc_ref, o_ref, kbuf, vbuf, sems, *, n_sel, layer):
    b = pl.program_id(0)

    def start(r, carry):
        row = rows_ref[b * n_sel + r]
        _row_copy(kc_ref, layer, kbuf, row, r, sems.at[0]).start()
        _row_copy(vc_ref, layer, vbuf, row, r, sems.at[1]).start()
        return carry

    def wait(r, carry):
        _row_copy(kc_ref, layer, kbuf, 0, r, sems.at[0]).wait()
        _row_copy(vc_ref, layer, vbuf, 0, r, sems.at[1]).wait()
        return carry

    lax.fori_loop(0, n_sel, start, 0)
    lax.fori_loop(0, n_sel, wait, 0)

    onehot = jnp.where(lax.broadcasted_iota(I32, (LANES, n_sel), 0) == bucket_ref[0], 1.0, 0.0)
    bias = jnp.dot(tab_ref[...], onehot, precision=lax.Precision.HIGHEST, preferred_element_type=F32)
    valid = valid_ref[0] > 0.5
    new_ok = new_ref[0, :, 0:1] > 0.5
    for kv in range(N_KV_HEADS):
        q8 = q_ref[0, kv]
        s = lax.dot_general(q8.astype(BF16), kbuf[:, kv, :].astype(BF16), (((1,), (1,)), ((), ())),
                            preferred_element_type=F32)
        bias8 = jnp.concatenate([bias[kv * GQA_GROUP:(kv + 1) * GQA_GROUP],
                                 jnp.zeros((8 - GQA_GROUP, n_sel), F32)], axis=0)
        s = jnp.where(valid, s * ATTN_SCALE + bias8, NEG)
        own_bias = jnp.concatenate([tab_ref[kv * GQA_GROUP:(kv + 1) * GQA_GROUP, 0:1],
                                    jnp.zeros((8 - GQA_GROUP, 1), F32)], axis=0)
        s_new = jnp.sum(q8 * kn_ref[0, kv:kv + 1, :], axis=-1, keepdims=True) * ATTN_SCALE + own_bias
        s_new = jnp.where(new_ok, s_new, NEG)
        m = jnp.maximum(jnp.max(s, axis=-1, keepdims=True), s_new)
        p = jnp.exp(s - m)
        p_new = jnp.exp(s_new - m)
        l = jnp.sum(p, axis=-1, keepdims=True) + p_new
        acc = jnp.dot(p.astype(BF16), vbuf[:, kv, :].astype(BF16), preferred_element_type=F32)
        o_ref[0, kv] = (acc + p_new * vn_ref[0, kv:kv + 1, :]) / l


def sample_dsa_attn_call(rows, q8, valid, bucket, table_t, k_new, v_new, new_sel, cache_k, cache_v, layer, n_sel):
    nb = q8.shape[0]
    return pl.pallas_call(
        functools.partial(_sample_dsa_attn_body, n_sel=n_sel, layer=layer),
        grid_spec=pltpu.PrefetchScalarGridSpec(
            num_scalar_prefetch=1, grid=(nb,),
            in_specs=[pl.BlockSpec((1, N_KV_HEADS, 8, HEAD_DIM), lambda b, rw: (b, 0, 0, 0)),
                      pl.BlockSpec((1, 1, n_sel), lambda b, rw: (b, 0, 0)),
                      pl.BlockSpec((1, 1, n_sel), lambda b, rw: (b, 0, 0)),
                      pl.BlockSpec((N_HEADS, LANES), lambda b, rw: (0, 0)),
                      pl.BlockSpec((1, N_KV_HEADS, HEAD_DIM), lambda b, rw: (b, 0, 0)),
                      pl.BlockSpec((1, N_KV_HEADS, HEAD_DIM), lambda b, rw: (b, 0, 0)),
                      pl.BlockSpec((1, 1, LANES), lambda b, rw: (b, 0, 0)),
                      pl.BlockSpec(memory_space=pl.ANY),
                      pl.BlockSpec(memory_space=pl.ANY)],
            out_specs=pl.BlockSpec((1, N_KV_HEADS, 8, HEAD_DIM), lambda b, rw: (b, 0, 0, 0)),
            scratch_shapes=[pltpu.VMEM((n_sel, N_KV_HEADS, HEAD_DIM), F32),
                            pltpu.VMEM((n_sel, N_KV_HEADS, HEAD_DIM), F32),
                            pltpu.SemaphoreType.DMA((2,))]),
        out_shape=jax.ShapeDtypeStruct((nb, N_KV_HEADS, 8, HEAD_DIM), F32),
        compiler_params=_params(("arbitrary",)),
        name="sample_dsa_attn",
    )(rows, q8, valid, bucket, table_t, k_new, v_new, new_sel, cache_k, cache_v)


def _rel_bucket(dist):
    n = jnp.maximum(dist, 0)
    exact = REL_BUCKETS // 2
    logr = jnp.log(jnp.maximum(n, 1).astype(F32) / exact) / math.log(REL_MAX_DIST / exact)
    large = jnp.minimum(exact + (logr * (REL_BUCKETS - exact)).astype(I32), REL_BUCKETS - 1)
    return jnp.where(n < exact, n, large)


def _head_gain(qn, kn, n_cols):
    return jnp.concatenate([jnp.tile(qn, N_HEADS), jnp.tile(kn, N_KV_HEADS),
                            jnp.ones((n_cols - Q_DIM - KV_DIM,), F32)]).reshape(1, n_cols)


def _project(x2d, g, sc, sh, w, qn, kn, *, tm, rows_per_mod):
    n_w = w.shape[1]
    n_main = IN_DIM_A if n_w == IN_DIM_A else IN_DIM_A + IDX_Q_DIM
    z, k_out, v_out = norm_proj_call(x2d, g, sc, sh, w, _head_gain(qn, kn, n_main), n_cols=n_main,
                                     n_norm_cols=Q_DIM + KV_DIM, tm=tm, rows_per_mod=rows_per_mod, kv_out=True)
    if n_w == IN_DIM_A:
        return z, k_out, v_out, None
    w_tail = jnp.pad(w[:, n_main:], ((0, 0), (0, LANES - (n_w - n_main))))
    tail, = norm_proj_call(x2d, g, sc, sh, w_tail, jnp.ones((1, LANES), F32), n_cols=LANES, n_norm_cols=0,
                           tm=tm, rows_per_mod=rows_per_mod, tn=LANES)
    return z, k_out, v_out, tail


def moba_sample_attention(zs, k_new, v_new, cache_k, cache_v, layer, page_table, prev_row, far, own_bias):
    nb, n_pages = page_table.shape
    own = n_pages * PAGE_SIZE // MOBA_BLOCK
    kbar = paged_block_mean_call(cache_k, layer, page_table.reshape(-1), nb, n_pages)
    kbar_pad = jnp.pad(kbar, ((0, 0), (0, LANES - own), (0, 0), (0, 0)))
    q3 = zs[:, :Q_DIM].reshape(nb, N_HEADS, HEAD_DIM)
    sel = sample_moba_select_call(q3, kbar_pad, own)[:, :, :MOBA_TOPK]
    per_blk = MOBA_BLOCK // PAGE_SIZE
    logical = sel[..., None] * per_blk + jnp.arange(per_blk)
    pages = page_table[jnp.arange(nb)[:, None, None, None], logical]
    return sample_moba_attn_call(q3, k_new, v_new, cache_k, cache_v, layer, pages.reshape(-1), sel.reshape(-1),
                                 prev_row, far, own_bias, own)


def dsa_sample_attention(zs, tail_s, k_new, v_new, cache_k, cache_v, cache_kidx, layer, page_table, table_t):
    nb, n_pages = page_table.shape
    past = n_pages * PAGE_SIZE
    n_sel = min(DSA_TOPK, (past + 1) // 4)
    qi_pad = jnp.pad(zs[:, IN_DIM_A:].reshape(nb, IDX_HEADS, IDX_DIM), ((0, 0), (0, 0), (0, LANES - IDX_DIM)))
    wi_col = tail_s[:, IDX_DIM:IDX_DIM + IDX_HEADS].reshape(nb, IDX_HEADS, 1)
    ki_pad = jnp.pad(tail_s[:, :IDX_DIM], ((0, 0), (0, LANES - IDX_DIM))).reshape(nb, 1, LANES)
    scores = sample_dsa_score_call(cache_kidx, layer, page_table.reshape(-1), qi_pad, wi_col, n_pages)
    picked, new_sel = sample_dsa_select_call(scores, qi_pad, wi_col, ki_pad, n_sel)
    chosen = jnp.concatenate([picked.reshape(nb, past) > 0.5, new_sel[:, 0, :1] > 0.5], axis=1)
    idx = jnp.argsort(jnp.logical_not(chosen), axis=1, stable=True)[:, :n_sel].astype(I32)
    pos = jnp.minimum(idx, past - 1)
    rows = page_table[jnp.arange(nb)[:, None], pos // PAGE_SIZE] * PAGE_SIZE + pos % PAGE_SIZE
    valid = (idx < past).astype(F32)
    bucket = _rel_bucket(past - pos).astype(I32)
    q8 = jnp.pad(zs[:, :Q_DIM].reshape(nb, N_KV_HEADS, GQA_GROUP, HEAD_DIM),
                 ((0, 0), (0, 0), (0, 8 - GQA_GROUP), (0, 0)))
    o8 = sample_dsa_attn_call(rows.reshape(-1), q8, valid[:, None], bucket[:, None], table_t,
                              k_new, v_new, new_sel, cache_k, cache_v, layer, n_sel)
    return o8[:, :, :GQA_GROUP].reshape(nb, Q_DIM)


def kernel(x_prompt, x_sample, cache_k_a, cache_v_a, cache_k_b, cache_v_b, cache_kidx_b, page_table, c_prompt, c_sample, rel_table, ada_w, ada_b, norm_attn, norm_ffn, q_norm, k_norm, w_in_a, w_in_b, w_o, w_grp, b_grp, w_rt, b_rt, w_gate, w_up, w_down):
    b, s, d = x_prompt.shape
    nb = x_sample.shape[0]
    depth = ada_w.shape[0]
    n_pages = page_table.shape[1]
    past = n_pages * PAGE_SIZE
    own = past // MOBA_BLOCK
    assert x_sample.shape[1] == 1 and past % MOBA_BLOCK == 0 and MOBA_TOPK <= own <= LANES
    assert s % ROW_TILE == 0 and nb % 8 == 0

    n_mod_rows = -(-(b + nb) // 8) * 8
    c_all = jnp.concatenate([c_prompt, c_sample, jnp.zeros((n_mod_rows - b - nb, d), F32)])
    mod = adaln_call(c_all, ada_w, ada_b)

    bias = bias_tiles_call(rel_table)
    lane_rep = lambda v: jnp.broadcast_to(v[:, None, None], (N_HEADS, 1, LANES))
    far = lane_rep(rel_table[REL_BUCKETS - 1])
    own_bias = lane_rep(rel_table[0])
    prev_row = bias[:, 1, 0:1, :]
    table_t = jnp.pad(rel_table.T, ((0, 0), (0, LANES - REL_BUCKETS)))

    xp = x_prompt.reshape(b * s, d)
    xs = x_sample.reshape(nb, d)
    kv_p = {"ka": [], "va": [], "kb": [], "vb": [], "ib": []}
    kv_s = {"ka": [], "va": [], "kb": [], "vb": [], "ib": []}

    for i in range(depth):
        j = i // 2
        mp = mod[i, :b].reshape(b, 1, 6, d)
        ms = mod[i, b:b + nb].reshape(1, nb, 6, d)
        sh1, sc1, gt1, sh2, sc2, gt2 = (mp[:, :, k] for k in range(6))
        sh1s, sc1s, gt1s, sh2s, sc2s, gt2s = (ms[:, :, k] for k in range(6))
        moba = i % 2 == 0
        w_in = w_in_a[j] if moba else w_in_b[j]
        z, k_p, v_p, tail = _project(xp, norm_attn[i], sc1, sh1, w_in, q_norm[i], k_norm[i],
                                     tm=ROW_TILE, rows_per_mod=s)
        zs, k_s, v_s, tail_s = _project(xs, norm_attn[i], sc1s, sh1s, w_in, q_norm[i], k_norm[i],
                                        tm=nb, rows_per_mod=nb)
        z3 = z.reshape(b, s, -1)
        tag = "a" if moba else "b"
        kv_p["k" + tag].append(k_p.reshape(b, s, N_KV_HEADS, HEAD_DIM))
        kv_p["v" + tag].append(v_p.reshape(b, s, N_KV_HEADS, HEAD_DIM))
        kv_s["k" + tag].append(k_s.reshape(nb, 1, N_KV_HEADS, HEAD_DIM))
        kv_s["v" + tag].append(v_s.reshape(nb, 1, N_KV_HEADS, HEAD_DIM))

        if moba:
            sel = moba_select_call(z3, s // MOBA_BLOCK)
            o = attn_call(z3, sel, jnp.zeros((b, 1, s), I32), bias, far, dsa=False)

            o_s = moba_sample_attention(zs, k_s, v_s, cache_k_a, cache_v_a, j, page_table, prev_row, far, own_bias)
        else:
            tail3 = tail.reshape(b, s, LANES)
            kv_p["ib"].append(tail3[:, :, :IDX_DIM])
            kv_s["ib"].append(tail_s[:, None, :IDX_DIM])
            wi_t = jnp.swapaxes(tail3[:, :, IDX_DIM:IDX_DIM + IDX_HEADS], 1, 2)
            key_t, thr = dsa_score_call(z3, tail3, wi_t, min(DSA_TOPK, s // 4))
            o = attn_call(z3, key_t, thr, bias, far, dsa=True)

            o_s = dsa_sample_attention(zs, tail_s, k_s, v_s, cache_k_b, cache_v_b, cache_kidx_b, j, page_table,
                                       table_t)

        xp = linear_residual_call(o.reshape(b * s, Q_DIM), w_o[i], xp, gt1, tm=ROW_TILE, rows_per_mod=s)
        xs = linear_residual_call(o_s.astype(BF16), w_o[i], xs, gt1s, tm=nb, rows_per_mod=nb)
        xp, xs = moe_layer([xp, xs], [(sc2, sh2, gt2), (sc2s, sh2s, gt2s)], [s, nb], norm_ffn[i],
                           w_grp[i], b_grp[i], w_rt[i], b_rt[i], w_gate[i], w_up[i], w_down[i], tms=[512, nb])

    stack = lambda lst: jnp.stack(lst)
    return (xp.reshape(b, s, d), xs.reshape(nb, 1, d),
            stack(kv_p["ka"]), stack(kv_p["va"]), stack(kv_p["kb"]), stack(kv_p["vb"]), stack(kv_p["ib"]),
            stack(kv_s["ka"]), stack(kv_s["va"]), stack(kv_s["kb"]), stack(kv_s["vb"]), stack(kv_s["ib"]))
```

```python
import functools
import math

import numpy as np
import jax
import jax.numpy as jnp
from jax import lax
from jax.experimental import pallas as pl
from jax.experimental.pallas import tpu as pltpu

F32 = jnp.float32
BF16 = jnp.bfloat16
I32 = jnp.int32

D_MODEL = 2048
N_HEADS = 16
HEAD_DIM = 128
N_KV_HEADS = 4
GQA_GROUP = N_HEADS // N_KV_HEADS
Q_DIM = N_HEADS * HEAD_DIM
KV_DIM = N_KV_HEADS * HEAD_DIM
ATTN_SCALE = HEAD_DIM ** -0.5
MOBA_BLOCK = 256
MOBA_TOPK = 3
IDX_HEADS = 16
IDX_DIM = 64
IDX_SCALE = IDX_DIM ** -0.5
IDX_W_SCALE = IDX_HEADS ** -0.5
DSA_TOPK = 256
REL_BUCKETS = 32
REL_MAX_DIST = 128
N_GROUPS = 4
EXPERTS_PER_GROUP = 8
N_EXPERTS = N_GROUPS * EXPERTS_PER_GROUP
D_EXPERT = 768
PAGE_SIZE = 128
IN_DIM_A = Q_DIM + 2 * KV_DIM
IDX_Q_DIM = IDX_HEADS * IDX_DIM
RMS_EPS = 1e-6
NEG = -1e30

LANES = 128
ATT_TILE = 256
ROW_TILE = 1024
COL_TILE = 512
MOE_BLOCK = 256
FF_CHUNK = 256
VMEM_LIMIT = 56 * 1024 * 1024


def _params(sem):
    return pltpu.CompilerParams(dimension_semantics=sem, vmem_limit_bytes=VMEM_LIMIT)


def _adaln_body(c_ref, w_ref, b_ref, o_ref):
    c = c_ref[...]
    a = c * (1.0 / (1.0 + jnp.exp(-c)))
    o_ref[0] = jnp.dot(a, w_ref[0], preferred_element_type=F32) + b_ref[0]


def adaln_call(c_all, ada_w, ada_b):
    n_layers, d, n = ada_w.shape
    r = c_all.shape[0]
    tn = 1024
    return pl.pallas_call(
        _adaln_body,
        grid=(n_layers, n // tn),
        in_specs=[pl.BlockSpec((r, d), lambda l, j: (0, 0)),
                  pl.BlockSpec((1, d, tn), lambda l, j: (l, 0, j)),
                  pl.BlockSpec((1, 1, tn), lambda l, j: (l, 0, j))],
        out_specs=pl.BlockSpec((1, r, tn), lambda l, j: (l, 0, j)),
        out_shape=jax.ShapeDtypeStruct((n_layers, r, n), F32),
        compiler_params=_params(("arbitrary", "arbitrary")),
        name="adaln",
    )(c_all, ada_w, ada_b.reshape(n_layers, 1, n))


def _modulated_norm(x, g, sc, sh):
    ms = jnp.mean(x * x, axis=-1, keepdims=True)
    return (x * lax.rsqrt(ms + RMS_EPS) * g) * (1.0 + sc) + sh


def _norm_proj_body(x_ref, g_ref, sc_ref, sh_ref, w_ref, gain_ref, o_ref, *rest, n_norm_tiles, tn, kv_out):
    h_scr = rest[-1]
    j = pl.program_id(1)

    @pl.when(j == 0)
    def _():
        h_scr[...] = _modulated_norm(x_ref[...], g_ref[...], sc_ref[0], sh_ref[0]).astype(BF16)

    z = jnp.dot(h_scr[...], w_ref[0].astype(BF16), preferred_element_type=F32)

    @pl.when(j < n_norm_tiles)
    def _():
        for k in range(tn // HEAD_DIM):
            sl = slice(k * HEAD_DIM, (k + 1) * HEAD_DIM)
            zk = z[:, sl]
            ms = jnp.mean(zk * zk, axis=-1, keepdims=True)
            o_ref[:, sl] = zk * lax.rsqrt(ms + RMS_EPS) * gain_ref[:, sl]

    @pl.when(j >= n_norm_tiles)
    def _():
        o_ref[...] = z

    if kv_out:
        for ref, tile in ((rest[0], Q_DIM // tn), (rest[1], Q_DIM // tn + 1)):
            @pl.when(j == tile)
            def _(ref=ref):
                for kv in range(N_KV_HEADS):
                    ref[:, kv, :] = o_ref[:, kv * HEAD_DIM:(kv + 1) * HEAD_DIM]


def norm_proj_call(x2d, g, sc, sh, w, layer, gain, *, n_cols, n_norm_cols, tm, rows_per_mod, tn=COL_TILE,
                   kv_out=False):
    m, d = x2d.shape
    assert m % tm == 0 and n_cols % tn == 0 and n_norm_cols % tn == 0
    assert not kv_out or tn == KV_DIM
    r = sc.shape[1]
    tiles_per_mod = rows_per_mod // tm
    mod_spec = pl.BlockSpec((1, r, d), lambda i, j: (i // tiles_per_mod, 0, 0))
    out_specs = [pl.BlockSpec((tm, tn), lambda i, j: (i, j))]
    out_shape = [jax.ShapeDtypeStruct((m, n_cols), F32)]
    if kv_out:
        out_specs += [pl.BlockSpec((tm, N_KV_HEADS, HEAD_DIM), lambda i, j: (i, 0, 0))] * 2
        out_shape += [jax.ShapeDtypeStruct((m, N_KV_HEADS, HEAD_DIM), F32)] * 2
    return pl.pallas_call(
        functools.partial(_norm_proj_body, n_norm_tiles=n_norm_cols // tn, tn=tn, kv_out=kv_out),
        grid=(m // tm, n_cols // tn),
        in_specs=[pl.BlockSpec((tm, d), lambda i, j: (i, 0)),
                  pl.BlockSpec((1, d), lambda i, j: (0, 0)),
                  mod_spec, mod_spec,
                  pl.BlockSpec((1, d, tn), lambda i, j: (layer, 0, j)),
                  pl.BlockSpec((1, tn), lambda i, j: (0, j))],
        out_specs=out_specs,
        out_shape=out_shape,
        scratch_shapes=[pltpu.VMEM((tm, d), BF16)],
        compiler_params=_params(("arbitrary", "arbitrary")),
        name="norm_proj",
    )(x2d, g.reshape(1, d), sc, sh, w, gain)


def _linear_residual_body(a_ref, w_ref, x_ref, gt_ref, o_ref):
    y = jnp.dot(a_ref[...], w_ref[0].astype(BF16), preferred_element_type=F32)
    o_ref[...] = x_ref[...] + gt_ref[0] * y


def linear_residual_call(a, w, layer, x2d, gt, *, tm, rows_per_mod, tn=COL_TILE):
    m, k = a.shape
    n = w.shape[2]
    r = gt.shape[1]
    tiles_per_mod = rows_per_mod // tm
    return pl.pallas_call(
        _linear_residual_body,
        grid=(m // tm, n // tn),
        in_specs=[pl.BlockSpec((tm, k), lambda i, j: (i, 0)),
                  pl.BlockSpec((1, k, tn), lambda i, j: (layer, 0, j)),
                  pl.BlockSpec((tm, tn), lambda i, j: (i, j)),
                  pl.BlockSpec((1, r, tn), lambda i, j: (i // tiles_per_mod, 0, j))],
        out_specs=pl.BlockSpec((tm, tn), lambda i, j: (i, j)),
        out_shape=jax.ShapeDtypeStruct((m, n), F32),
        compiler_params=_params(("arbitrary", "arbitrary")),
        name="linear_residual",
    )(a, w, x2d, gt)


def _rel_bucket_np(dist):
    n = np.maximum(dist, 0)
    exact = REL_BUCKETS // 2
    logr = (np.log(np.maximum(n, 1).astype(np.float32) / np.float32(exact))
            / np.float32(math.log(REL_MAX_DIST / exact))).astype(np.float32)
    large = np.minimum(exact + (logr * np.float32(REL_BUCKETS - exact)).astype(np.int32), REL_BUCKETS - 1)
    return np.where(n < exact, n, large).astype(np.int32)


def _bias_body(bk_ref, tab_ref, o_ref):
    h = pl.program_id(0)
    bk = bk_ref[...]
    acc = jnp.zeros(bk.shape, F32)
    for b in range(REL_BUCKETS):
        acc = jnp.where(bk == b, tab_ref[b, h], acc)
    o_ref[0] = jnp.where(bk < 0, NEG, acc)


def bias_tiles_call(rel_table):
    t = np.arange(ATT_TILE)
    d = t[:, None] - t[None, :]
    buckets = np.stack([np.where(d >= 0, _rel_bucket_np(d), -1), _rel_bucket_np(d + ATT_TILE)]).astype(np.int32)
    return pl.pallas_call(
        _bias_body,
        grid=(N_HEADS,),
        in_specs=[pl.BlockSpec((2, ATT_TILE, ATT_TILE), lambda h: (0, 0, 0)),
                  pl.BlockSpec(memory_space=pltpu.SMEM)],
        out_specs=pl.BlockSpec((1, 2, ATT_TILE, ATT_TILE), lambda h: (h, 0, 0, 0)),
        out_shape=jax.ShapeDtypeStruct((N_HEADS, 2, ATT_TILE, ATT_TILE), F32),
        compiler_params=_params(("arbitrary",)),
        name="bias_tiles",
    )(jnp.asarray(buckets), rel_table)


def _moba_select_body(q_ref, k_ref, o_ref, kbar_scr, code_scr):
    i = pl.program_id(1)
    nblk = kbar_scr.shape[0]

    @pl.when(i == 0)
    def _():
        k = k_ref[0]
        kbar_scr[...] = jnp.mean(k.reshape(nblk, MOBA_BLOCK, KV_DIM), axis=1)
        code_scr[...] = jnp.zeros(code_scr.shape, F32)

    blk = lax.broadcasted_iota(I32, (nblk, ATT_TILE), 0)
    for h in range(N_HEADS):
        kv = h // GQA_GROUP
        gate = lax.dot_general(kbar_scr[:, kv * HEAD_DIM:(kv + 1) * HEAD_DIM].astype(BF16),
                               q_ref[0, :, h * HEAD_DIM:(h + 1) * HEAD_DIM].astype(BF16),
                               (((1,), (1,)), ((), ())), preferred_element_type=F32)
        gate = jnp.where(blk < i, gate, NEG)
        code = jnp.zeros((1, ATT_TILE), F32)
        for _ in range(MOBA_TOPK):
            mx = jnp.max(gate, axis=0, keepdims=True)
            idx = jnp.min(jnp.where(gate == mx, blk, nblk), axis=0, keepdims=True)
            code = code + jnp.where(idx < i, jnp.exp2(idx.astype(F32)), 0.0)
            gate = jnp.where(blk == idx, -jnp.inf, gate)
        code_scr[h:h + 1, :] = code
    o_ref[0] = code_scr[...].T


def moba_select_call(z3, n_blocks):
    b, s, _ = z3.shape
    return pl.pallas_call(
        _moba_select_body,
        grid=(b, s // ATT_TILE),
        in_specs=[pl.BlockSpec((1, ATT_TILE, Q_DIM), lambda bb, i: (bb, i, 0)),
                  pl.BlockSpec((1, s, KV_DIM), lambda bb, i: (bb, 0, Q_DIM // KV_DIM))],
        out_specs=pl.BlockSpec((1, ATT_TILE, LANES), lambda bb, i: (bb, i, 0)),
        out_shape=jax.ShapeDtypeStruct((b, s, LANES), F32),
        scratch_shapes=[pltpu.VMEM((n_blocks, KV_DIM), F32), pltpu.VMEM((LANES, ATT_TILE), F32)],
        compiler_params=_params(("arbitrary", "arbitrary")),
        name="moba_select",
    )(z3, z3)


def _flash_update(kv, q4, kb, vb, bias4, mask3, m_scr, l_scr, acc_scr):
    s = lax.dot_general(q4, kb, (((1,), (1,)), ((), ())), preferred_element_type=F32)
    s = s.reshape(GQA_GROUP, ATT_TILE, ATT_TILE) * ATTN_SCALE + bias4
    if mask3 is not None:
        s = jnp.where(mask3, s, NEG)
    m_prev = m_scr[kv]
    m_new = jnp.maximum(m_prev, jnp.max(s, axis=-1, keepdims=True))
    alpha = jnp.exp(m_prev - m_new)
    p = jnp.exp(s - jnp.concatenate([m_new] * (ATT_TILE // LANES), axis=-1))
    pv = jnp.dot(p.reshape(GQA_GROUP * ATT_TILE, ATT_TILE).astype(BF16), vb, preferred_element_type=F32)
    pv = pv.reshape(GQA_GROUP, ATT_TILE, HEAD_DIM + LANES)
    acc_scr[kv] = alpha * acc_scr[kv] + pv[:, :, :HEAD_DIM]
    l_scr[kv] = alpha * l_scr[kv] + pv[:, :, HEAD_DIM:]
    m_scr[kv] = m_new


def _attn_body(q_ref, k_ref, v_ref, sel_ref, thr_ref, bias_ref, far_ref, o_ref, m_scr, l_scr, acc_scr, code_scr,
               *, dsa):
    i = pl.program_id(1)
    j = pl.program_id(2)

    @pl.when(j == 0)
    def _():
        m_scr[...] = jnp.full(m_scr.shape, NEG, F32)
        l_scr[...] = jnp.zeros(l_scr.shape, F32)
        acc_scr[...] = jnp.zeros(acc_scr.shape, F32)
        if not dsa:
            code = sel_ref[0].astype(I32)
            for h in range(N_HEADS):
                code_scr[h] = jnp.broadcast_to(code[:, h:h + 1], (ATT_TILE, LANES))

    def step(which):
        if dsa:
            mask_all = (jnp.where(sel_ref[0] >= thr_ref[0], 1.0, 0.0).T > 0.5)[None]
        ones = jnp.ones((ATT_TILE, LANES), BF16)
        for kv in range(N_KV_HEADS):
            heads = range(kv * GQA_GROUP, (kv + 1) * GQA_GROUP)
            q4 = jnp.concatenate([q_ref[0, :, h * HEAD_DIM:(h + 1) * HEAD_DIM] for h in heads],
                                 axis=0).astype(BF16)
            kb = k_ref[0, :, kv * HEAD_DIM:(kv + 1) * HEAD_DIM].astype(BF16)
            vb = jnp.concatenate([v_ref[0, :, kv * HEAD_DIM:(kv + 1) * HEAD_DIM].astype(BF16), ones], axis=1)
            if which == 2:
                bias4 = far_ref[kv * GQA_GROUP:(kv + 1) * GQA_GROUP, :, 0:1]
            else:
                bias4 = bias_ref[kv * GQA_GROUP:(kv + 1) * GQA_GROUP, which]
            if dsa:
                mask3 = mask_all
            elif which == 0:
                mask3 = None
            else:
                bits = jnp.stack([jnp.right_shift(code_scr[h], j) & 1 for h in heads])
                mask3 = jnp.concatenate([bits] * (ATT_TILE // LANES), axis=-1) > 0
            _flash_update(kv, q4, kb, vb, bias4, mask3, m_scr, l_scr, acc_scr)

    @pl.when(j == i)
    def _():
        step(0)

    @pl.when(j == i - 1)
    def _():
        step(1)

    @pl.when(j < i - 1)
    def _():
        step(2)

    @pl.when(j == i)
    def _():
        for kv in range(N_KV_HEADS):
            out = acc_scr[kv] / l_scr[kv]
            for g in range(GQA_GROUP):
                h = kv * GQA_GROUP + g
                o_ref[0, :, h * HEAD_DIM:(h + 1) * HEAD_DIM] = out[g].astype(o_ref.dtype)


def attn_call(z3, sel, thr, bias, far, *, dsa):
    b, s, _ = z3.shape
    nt = s // ATT_TILE
    if dsa:
        sel_spec = pl.BlockSpec((1, ATT_TILE, ATT_TILE), lambda bb, i, j: (bb, jnp.minimum(j, i), i))
    else:
        sel_spec = pl.BlockSpec((1, ATT_TILE, LANES), lambda bb, i, j: (bb, i, 0))
    kcol, vcol = Q_DIM // KV_DIM, Q_DIM // KV_DIM + 1
    return pl.pallas_call(
        functools.partial(_attn_body, dsa=dsa),
        grid=(b, nt, nt),
        in_specs=[pl.BlockSpec((1, ATT_TILE, Q_DIM), lambda bb, i, j: (bb, i, 0)),
                  pl.BlockSpec((1, ATT_TILE, KV_DIM), lambda bb, i, j: (bb, jnp.minimum(j, i), kcol)),
                  pl.BlockSpec((1, ATT_TILE, KV_DIM), lambda bb, i, j: (bb, jnp.minimum(j, i), vcol)),
                  sel_spec,
                  pl.BlockSpec((1, 1, ATT_TILE), lambda bb, i, j: (bb, 0, i)),
                  pl.BlockSpec((N_HEADS, 2, ATT_TILE, ATT_TILE), lambda bb, i, j: (0, 0, 0, 0)),
                  pl.BlockSpec((N_HEADS, 1, LANES), lambda bb, i, j: (0, 0, 0))],
        out_specs=pl.BlockSpec((1, ATT_TILE, Q_DIM), lambda bb, i, j: (bb, i, 0)),
        out_shape=jax.ShapeDtypeStruct((b, s, Q_DIM), BF16),
        scratch_shapes=[pltpu.VMEM((N_KV_HEADS, GQA_GROUP, ATT_TILE, LANES), F32),
                        pltpu.VMEM((N_KV_HEADS, GQA_GROUP, ATT_TILE, LANES), F32),
                        pltpu.VMEM((N_KV_HEADS, GQA_GROUP, ATT_TILE, HEAD_DIM), F32),
                        pltpu.VMEM((1 if dsa else N_HEADS, ATT_TILE, LANES), I32)],
        compiler_params=_params(("arbitrary", "arbitrary", "arbitrary")),
        name="dsa_attn" if dsa else "moba_attn",
    )(z3, z3, z3, sel, thr, bias, far)


def _sortable_key(x):
    bits = pltpu.bitcast(x, I32)
    return bits ^ (jnp.right_shift(bits, 31) & 0x7FFFFFFF)


def _kth_largest_key(count_ge, k, shape):
    def bit_body(bi, t):
        cand = t ^ jnp.left_shift(jnp.int32(1), 31 - bi)
        return jnp.where(count_ge(cand) >= k, cand, t)
    return lax.fori_loop(0, 32, bit_body, jnp.full(shape, -2 ** 31, I32))


def _dsa_score_body(qi_ref, tail_ref, wi_ref, key_ref, thr_ref, key_scr, *, n_sel):
    i = pl.program_id(1)
    j = pl.program_id(2)

    @pl.when(j <= i)
    def _():
        t = tail_ref[0]
        lane = lax.broadcasted_iota(I32, t.shape, 1)
        klo = jnp.where(lane < IDX_DIM, t, 0.0)
        khi = pltpu.roll(klo, IDX_DIM, 1)
        qstk = jnp.concatenate([qi_ref[0, :, p * LANES:(p + 1) * LANES] for p in range(IDX_HEADS // 2)],
                               axis=0).astype(BF16)
        dn = (((1,), (1,)), ((), ()))
        s_lo = lax.dot_general(klo.astype(BF16), qstk, dn, preferred_element_type=F32)
        s_hi = lax.dot_general(khi.astype(BF16), qstk, dn, preferred_element_type=F32)
        sc = jnp.zeros((ATT_TILE, ATT_TILE), F32)
        for p in range(IDX_HEADS // 2):
            sl = slice(p * ATT_TILE, (p + 1) * ATT_TILE)
            sc = sc + jnp.maximum(s_lo[:, sl] * IDX_SCALE, 0.0) * wi_ref[0, 2 * p:2 * p + 1, :]
            sc = sc + jnp.maximum(s_hi[:, sl] * IDX_SCALE, 0.0) * wi_ref[0, 2 * p + 1:2 * p + 2, :]
        sc = sc * IDX_W_SCALE
        kpos = j * ATT_TILE + lax.broadcasted_iota(I32, sc.shape, 0)
        qpos = i * ATT_TILE + lax.broadcasted_iota(I32, sc.shape, 1)
        key = _sortable_key(jnp.where(kpos <= qpos, sc, NEG))
        key_ref[0] = key
        key_scr[pl.ds(pl.multiple_of(j * ATT_TILE, ATT_TILE), ATT_TILE), :] = key

    @pl.when(j > i)
    def _():
        key_ref[0] = jnp.full((ATT_TILE, ATT_TILE), -2 ** 31, I32)

    @pl.when(j == i)
    def _():
        def count_ge(cand):
            def chunk(c, cnt):
                kc = key_scr[pl.ds(pl.multiple_of(c * ATT_TILE, ATT_TILE), ATT_TILE), :]
                hit = jnp.where(kc >= cand, 1, 0).astype(I32)
                return cnt + jnp.sum(hit.reshape(ATT_TILE // 8, 8, ATT_TILE), axis=0)
            cnt8 = lax.fori_loop(0, i + 1, chunk, jnp.zeros((8, ATT_TILE), I32))
            return jnp.sum(cnt8, axis=0, keepdims=True)
        thr_ref[0] = _kth_largest_key(count_ge, n_sel, (1, ATT_TILE))


def dsa_score_call(z3, tail3, wi_t, n_sel):
    b, s, _ = z3.shape
    nt = s // ATT_TILE
    return pl.pallas_call(
        functools.partial(_dsa_score_body, n_sel=n_sel),
        grid=(b, nt, nt),
        in_specs=[pl.BlockSpec((1, ATT_TILE, IDX_Q_DIM), lambda bb, i, j: (bb, i, IN_DIM_A // IDX_Q_DIM)),
                  pl.BlockSpec((1, ATT_TILE, LANES), lambda bb, i, j: (bb, jnp.minimum(j, i), 0)),
                  pl.BlockSpec((1, IDX_HEADS, ATT_TILE), lambda bb, i, j: (bb, 0, i))],
        out_specs=[pl.BlockSpec((1, ATT_TILE, ATT_TILE), lambda bb, i, j: (bb, j, i)),
                   pl.BlockSpec((1, 1, ATT_TILE), lambda bb, i, j: (bb, 0, i))],
        out_shape=[jax.ShapeDtypeStruct((b, s, s), I32), jax.ShapeDtypeStruct((b, 1, s), I32)],
        scratch_shapes=[pltpu.VMEM((s, ATT_TILE), I32)],
        compiler_params=_params(("arbitrary", "arbitrary", "arbitrary")),
        name="dsa_score",
    )(z3, tail3, wi_t)


def _router_body(x_ref, g_ref, sc_ref, sh_ref, wr_ref, br_ref, h_ref, r_ref):
    h = _modulated_norm(x_ref[...], g_ref[...], sc_ref[0], sh_ref[0])
    h_ref[...] = h.astype(h_ref.dtype)
    lg = jnp.dot(h.astype(BF16), wr_ref[...].astype(BF16), preferred_element_type=F32) + br_ref[...]
    lane = lax.broadcasted_iota(I32, lg.shape, 1)
    gl = jnp.where(lane < N_GROUPS, lg, -jnp.inf)
    gmax = jnp.max(gl, axis=-1, keepdims=True)
    g_sel = jnp.min(jnp.where(gl == gmax, lane, LANES), axis=-1, keepdims=True)
    g_w = 1.0 / jnp.sum(jnp.exp(gl - gmax), axis=-1, keepdims=True)
    lo = N_GROUPS + EXPERTS_PER_GROUP * g_sel
    el = jnp.where((lane >= lo) & (lane < lo + EXPERTS_PER_GROUP), lg, -jnp.inf)
    e1 = jnp.max(el, axis=-1, keepdims=True)
    i1 = jnp.min(jnp.where(el == e1, lane, LANES), axis=-1, keepdims=True)
    el2 = jnp.where(lane == i1, -jnp.inf, el)
    e2 = jnp.max(el2, axis=-1, keepdims=True)
    i2 = jnp.min(jnp.where(el2 == e2, lane, LANES), axis=-1, keepdims=True)
    v2 = jnp.exp(e2 - e1)
    w1 = g_w / (1.0 + v2)
    w2 = g_w * v2 / (1.0 + v2)
    r_ref[...] = jnp.where(lane == 0, (i1 - N_GROUPS).astype(F32),
                           jnp.where(lane == 1, (i2 - N_GROUPS).astype(F32),
                                     jnp.where(lane == 2, w1, jnp.where(lane == 3, w2, 0.0))))


def router_call(x2d, g, sc, sh, w_route, b_route, *, tm, rows_per_mod):
    m, d = x2d.shape
    r = sc.shape[1]
    tiles_per_mod = rows_per_mod // tm
    mod_spec = pl.BlockSpec((1, r, d), lambda i: (i // tiles_per_mod, 0, 0))
    return pl.pallas_call(
        _router_body,
        grid=(m // tm,),
        in_specs=[pl.BlockSpec((tm, d), lambda i: (i, 0)),
                  pl.BlockSpec((1, d), lambda i: (0, 0)),
                  mod_spec, mod_spec,
                  pl.BlockSpec((d, LANES), lambda i: (0, 0)),
                  pl.BlockSpec((1, LANES), lambda i: (0, 0))],
        out_specs=[pl.BlockSpec((tm, d), lambda i: (i, 0)), pl.BlockSpec((tm, LANES), lambda i: (i, 0))],
        out_shape=[jax.ShapeDtypeStruct((m, d), F32), jax.ShapeDtypeStruct((m, LANES), F32)],
        compiler_params=_params(("arbitrary",)),
        name="router",
    )(x2d, g.reshape(1, d), sc, sh, w_route, b_route)


def _row_dma_wait(ref_hbm, dst, n_rows, sem):
    pltpu.make_async_copy(ref_hbm.at[pl.ds(0, n_rows)], dst, sem).wait()


def _ffn_body(be_ref, nu_ref, cur_ref, nxt_ref, h_ref, wg_ref, wu_ref, wd_ref, o_ref, xbuf, sems):
    i = pl.program_id(0)
    nblk = pl.num_programs(0)
    nu = nu_ref[0]
    slot = i % 2

    def gather(src_ref, sl):
        for r in range(MOE_BLOCK):
            pltpu.make_async_copy(h_ref.at[pl.ds(src_ref[0, 0, r], 1)], xbuf.at[sl, pl.ds(r, 1)],
                                  sems.at[sl]).start()

    def gather_wait(sl):
        _row_dma_wait(h_ref, xbuf.at[sl], MOE_BLOCK, sems.at[sl])

    @pl.when(i == 0)
    def _():
        gather(cur_ref, 0)

    @pl.when(i < nu)
    def _():
        gather(nxt_ref, 1 - slot)
        gather_wait(slot)
        x = xbuf[slot].astype(BF16)
        acc = jnp.zeros(o_ref.shape, F32)
        for c in range(D_EXPERT // FF_CHUNK):
            sl = slice(c * FF_CHUNK, (c + 1) * FF_CHUNK)
            a = jnp.dot(x, wg_ref[0, 0, :, sl].astype(BF16), preferred_element_type=F32)
            u = jnp.dot(x, wu_ref[0, 0, :, sl].astype(BF16), preferred_element_type=F32)
            act = a * (1.0 / (1.0 + jnp.exp(-a))) * u
            acc = acc + jnp.dot(act.astype(BF16), wd_ref[0, 0, sl, :].astype(BF16), preferred_element_type=F32)
        o_ref[...] = acc

    @pl.when(i >= nu)
    def _():
        o_ref[...] = jnp.zeros(o_ref.shape, F32)

    @pl.when(i == nu)
    def _():
        gather_wait(slot)

    @pl.when((i == nblk - 1) & (nu == nblk))
    def _():
        gather_wait(1 - slot)


def ffn_call(blk_e, n_used, src_tok, h_all, w_gate, w_up, w_down, layer):
    d = h_all.shape[1]
    nblk = blk_e.shape[0]
    last = lambda i, nu: jnp.minimum(i, nu[0] - 1)
    src3 = src_tok.reshape(nblk, 1, MOE_BLOCK)
    w_in_spec = pl.BlockSpec((1, 1, d, D_EXPERT), lambda i, be, nu: (layer, be[last(i, nu)], 0, 0))
    grid_spec = pltpu.PrefetchScalarGridSpec(
        num_scalar_prefetch=2,
        grid=(nblk,),
        in_specs=[pl.BlockSpec((1, 1, MOE_BLOCK), lambda i, be, nu: (i, 0, 0), memory_space=pltpu.SMEM),
                  pl.BlockSpec((1, 1, MOE_BLOCK), lambda i, be, nu: (jnp.minimum(i + 1, nblk - 1), 0, 0),
                               memory_space=pltpu.SMEM),
                  pl.BlockSpec(memory_space=pl.ANY),
                  w_in_spec, w_in_spec,
                  pl.BlockSpec((1, 1, D_EXPERT, d), lambda i, be, nu: (layer, be[last(i, nu)], 0, 0))],
        out_specs=pl.BlockSpec((MOE_BLOCK, d), lambda i, be, nu: (i, 0)),
        scratch_shapes=[pltpu.VMEM((2, MOE_BLOCK, d), F32), pltpu.SemaphoreType.DMA((2,))],
    )
    return pl.pallas_call(
        _ffn_body,
        grid_spec=grid_spec,
        out_shape=jax.ShapeDtypeStruct((nblk * MOE_BLOCK, d), F32),
        compiler_params=_params(("arbitrary",)),
        name="expert_ffn",
    )(blk_e, n_used, src3, src3, h_all, w_gate, w_up, w_down)


def _combine_body(dcur_ref, dnxt_ref, x_ref, rt_ref, gt_ref, ob_ref, o_ref, rbuf, sems, *, tm):
    t = pl.program_id(0)
    slot = t % 2

    def issue(dref, sl):
        def start(r, carry):
            dst = (r & 1) * tm + lax.shift_right_logical(r, 1)
            pltpu.make_async_copy(ob_ref.at[pl.ds(dref[0, 0, r], 1)], rbuf.at[sl, pl.ds(dst, 1)],
                                  sems.at[sl]).start()
            return carry
        lax.fori_loop(0, 2 * tm, start, 0)

    @pl.when(t == 0)
    def _():
        issue(dcur_ref, 0)

    @pl.when(t + 1 < pl.num_programs(0))
    def _():
        issue(dnxt_ref, 1 - slot)

    _row_dma_wait(ob_ref, rbuf.at[slot], 2 * tm, sems.at[slot])
    w0 = rt_ref[:, 2:3]
    w1 = rt_ref[:, 3:4]
    o_ref[...] = x_ref[...] + gt_ref[0] * (w0 * rbuf[slot, 0:tm] + w1 * rbuf[slot, tm:2 * tm])


def combine_call(x2d, out_buf, dest, route, gt, *, tm, rows_per_mod):
    m, d = x2d.shape
    r = gt.shape[1]
    nt = m // tm
    tiles_per_mod = rows_per_mod // tm
    dest3 = dest.reshape(nt, 1, 2 * tm)
    return pl.pallas_call(
        functools.partial(_combine_body, tm=tm),
        grid=(nt,),
        in_specs=[pl.BlockSpec((1, 1, 2 * tm), lambda t: (t, 0, 0), memory_space=pltpu.SMEM),
                  pl.BlockSpec((1, 1, 2 * tm), lambda t: (jnp.minimum(t + 1, nt - 1), 0, 0),
                               memory_space=pltpu.SMEM),
                  pl.BlockSpec((tm, d), lambda t: (t, 0)),
                  pl.BlockSpec((tm, LANES), lambda t: (t, 0)),
                  pl.BlockSpec((1, r, d), lambda t: (t // tiles_per_mod, 0, 0)),
                  pl.BlockSpec(memory_space=pl.ANY)],
        out_specs=pl.BlockSpec((tm, d), lambda t: (t, 0)),
        out_shape=jax.ShapeDtypeStruct((m, d), F32),
        scratch_shapes=[pltpu.VMEM((2, 2 * tm, d), F32), pltpu.SemaphoreType.DMA((2,))],
        compiler_params=_params(("arbitrary",)),
        name="moe_combine",
    )(dest3, dest3, x2d, route, gt, out_buf)


def moe_dispatch(eid):
    a = eid.size
    flat_e = eid.reshape(a)
    order = jnp.argsort(flat_e, stable=True)
    rank = jnp.argsort(order)
    counts = jnp.bincount(flat_e, length=N_EXPERTS)
    padded = (counts + MOE_BLOCK - 1) // MOE_BLOCK * MOE_BLOCK
    pend = jnp.cumsum(padded)
    pstart = pend - padded
    cstart = jnp.cumsum(counts) - counts
    dest = (pstart[flat_e] + rank - cstart[flat_e]).astype(I32)
    nblk = -(-(a + N_EXPERTS * (MOE_BLOCK - 1)) // MOE_BLOCK)
    blk_e = jnp.minimum(jnp.searchsorted(pend, jnp.arange(nblk) * MOE_BLOCK, side='right'),
                        N_EXPERTS - 1).astype(I32)
    n_used = (pend[-1] // MOE_BLOCK).astype(I32).reshape(1)
    row = jnp.arange(nblk * MOE_BLOCK)
    row_e = jnp.repeat(blk_e, MOE_BLOCK)
    k = row - pstart[row_e]
    filled = (k >= 0) & (k < counts[row_e])
    src_tok = jnp.where(filled, order[jnp.clip(cstart[row_e] + k, 0, a - 1)] // 2, 0).astype(I32)
    return dest, src_tok, blk_e, n_used


def moe_layer(xs, mods, rows_per_mods, g_ffn, w_grp, b_grp, w_rt, b_rt, w_gate, w_up, w_down, layer, tms):
    n_route = N_GROUPS + N_EXPERTS
    w_route = jnp.pad(jnp.concatenate([w_grp, w_rt], axis=1), ((0, 0), (0, LANES - n_route)))
    b_route = jnp.pad(jnp.concatenate([b_grp, b_rt]), (0, LANES - n_route)).reshape(1, LANES)
    hs, routes = [], []
    for x, (sc, sh, _), rpm, tm in zip(xs, mods, rows_per_mods, tms):
        h, rt = router_call(x, g_ffn, sc, sh, w_route, b_route, tm=tm, rows_per_mod=rpm)
        hs.append(h)
        routes.append(rt)
    route = jnp.concatenate(routes) if len(routes) > 1 else routes[0]
    dest, src_tok, blk_e, n_used = moe_dispatch(route[:, :2].astype(I32))
    h_all = jnp.concatenate(hs) if len(hs) > 1 else hs[0]
    out_buf = ffn_call(blk_e, n_used, src_tok, h_all, w_gate, w_up, w_down, layer)
    outs, off = [], 0
    for x, (_, _, gt), rt, rpm, tm in zip(xs, mods, routes, rows_per_mods, tms):
        m = x.shape[0]
        outs.append(combine_call(x, out_buf, dest[2 * off:2 * (off + m)], rt, gt,
                                 tm=min(tm, MOE_BLOCK), rows_per_mod=rpm))
        off += m
    return outs


MEAN_PAGES = 8
SCORE_PAGES = 16


def _page_spec(block, layer, slot_of):
    def index_map(*args):
        ids, pages = args[:2], args[2]
        return (layer, pages[slot_of(*ids)]) + (0,) * (len(block) - 2)
    return pl.BlockSpec(block, index_map)


def _block_mean_body(pt_ref, *refs):
    pages, o_ref = refs[:MEAN_PAGES], refs[MEAN_PAGES]
    per_blk = MOBA_BLOCK // PAGE_SIZE
    for blk in range(MEAN_PAGES // per_blk):
        tot = sum(jnp.sum(pages[blk * per_blk + p][0, 0], axis=0) for p in range(per_blk))
        o_ref[0, 0, blk] = tot * (1.0 / MOBA_BLOCK)


def paged_block_mean_call(cache, layer, pt_flat, nb, n_pages):
    per_step = MEAN_PAGES * PAGE_SIZE // MOBA_BLOCK
    specs = [_page_spec((1, 1, PAGE_SIZE, N_KV_HEADS, HEAD_DIM), layer, functools.partial(
        lambda p, b, g: b * n_pages + g * MEAN_PAGES + p, p)) for p in range(MEAN_PAGES)]
    out = pl.pallas_call(
        _block_mean_body,
        grid_spec=pltpu.PrefetchScalarGridSpec(
            num_scalar_prefetch=1, grid=(nb, n_pages // MEAN_PAGES), in_specs=specs,
            out_specs=pl.BlockSpec((1, 1, per_step, N_KV_HEADS, HEAD_DIM), lambda b, g, pt: (b, g, 0, 0, 0))),
        out_shape=jax.ShapeDtypeStruct((nb, n_pages // MEAN_PAGES, per_step, N_KV_HEADS, HEAD_DIM), F32),
        compiler_params=_params(("arbitrary", "arbitrary")),
        name="paged_block_mean",
    )(pt_flat, *([cache] * MEAN_PAGES))
    return out.reshape(nb, n_pages * PAGE_SIZE // MOBA_BLOCK, N_KV_HEADS, HEAD_DIM)


def _sample_select_body(q_ref, kbar_ref, o_ref, *, own):
    q = q_ref[0]
    row = lax.broadcasted_iota(I32, (N_HEADS, LANES), 0)
    lane = lax.broadcasted_iota(I32, (N_HEADS, LANES), 1)
    gate = jnp.zeros((N_HEADS, LANES), F32)
    for kv in range(N_KV_HEADS):
        gk = lax.dot_general(q.astype(BF16), kbar_ref[0, :, kv, :].astype(BF16), (((1,), (1,)), ((), ())),
                             preferred_element_type=F32)
        gate = jnp.where(row // GQA_GROUP == kv, gk, gate)
    gate = jnp.where(lane < own, gate, NEG)
    out = jnp.zeros((N_HEADS, LANES), I32)
    for c in range(MOBA_TOPK):
        mx = jnp.max(gate, axis=-1, keepdims=True)
        idx = jnp.min(jnp.where(gate == mx, lane, LANES), axis=-1, keepdims=True)
        out = jnp.where(lane == c, idx, out)
        gate = jnp.where(lane == idx, -jnp.inf, gate)
    o_ref[0] = out


def sample_moba_select_call(q3, kbar_pad, own):
    nb = q3.shape[0]
    return pl.pallas_call(
        functools.partial(_sample_select_body, own=own),
        grid=(nb,),
        in_specs=[pl.BlockSpec((1, N_HEADS, HEAD_DIM), lambda b: (b, 0, 0)),
                  pl.BlockSpec((1, LANES, N_KV_HEADS, HEAD_DIM), lambda b: (b, 0, 0, 0))],
        out_specs=pl.BlockSpec((1, N_HEADS, LANES), lambda b: (b, 0, 0)),
        out_shape=jax.ShapeDtypeStruct((nb, N_HEADS, LANES), I32),
        compiler_params=_params(("arbitrary",)),
        name="sample_moba_select",
    )(q3, kbar_pad)


SEL_PAGES = MOBA_TOPK * (MOBA_BLOCK // PAGE_SIZE)


def _sample_moba_attn_body(pg_ref, blk_ref, q_ref, kn_ref, vn_ref, prev_ref, far_ref, own_ref, kc_ref, vc_ref,
                           o_ref, kbuf, vbuf, sems, *, own, layer):
    b = pl.program_id(0)
    per_blk = MOBA_BLOCK // PAGE_SIZE
    copies = []
    for h in range(N_HEADS):
        kv = h // GQA_GROUP
        for s in range(SEL_PAGES):
            page = pg_ref[(b * N_HEADS + h) * SEL_PAGES + s]
            rows = pl.ds(s * PAGE_SIZE, PAGE_SIZE)
            copies.append(pltpu.make_async_copy(kc_ref.at[layer, page, :, kv, :], kbuf.at[h, rows], sems.at[0]))
            copies.append(pltpu.make_async_copy(vc_ref.at[layer, page, :, kv, :], vbuf.at[h, rows], sems.at[1]))
    for cp in copies:
        cp.start()
    for cp in copies:
        cp.wait()

    for h in range(N_HEADS):
        kv = h // GQA_GROUP
        q = q_ref[0, h:h + 1, :]
        q8 = jnp.broadcast_to(q, (8, HEAD_DIM)).astype(BF16)
        s = lax.dot_general(q8, kbuf[h].astype(BF16), (((1,), (1,)), ((), ())),
                            preferred_element_type=F32)
        far = jnp.broadcast_to(far_ref[h, :, 0:1], (1, MOBA_BLOCK))
        bias = jnp.concatenate(
            [jnp.where(blk_ref[(b * N_HEADS + h) * MOBA_TOPK + n] == own - 1, prev_ref[h], far)
             for n in range(MOBA_TOPK)], axis=1)
        s = s * ATTN_SCALE + bias
        s_own = (jnp.sum(q * kn_ref[0, kv:kv + 1, :], axis=-1, keepdims=True) * ATTN_SCALE
                 + own_ref[h, :, 0:1])
        m = jnp.maximum(jnp.max(s, axis=-1, keepdims=True), s_own)
        p = jnp.exp(s - m)
        p_own = jnp.exp(s_own - m)
        l = jnp.sum(p, axis=-1, keepdims=True) + p_own
        acc = jnp.dot(p.astype(BF16), vbuf[h].astype(BF16), preferred_element_type=F32)
        o_ref[0, h:h + 1, :] = ((acc + p_own * vn_ref[0, kv:kv + 1, :]) / l)[0:1]


def sample_moba_attn_call(q3, k_new, v_new, cache_k, cache_v, layer, pages, blks, prev_row, far, own_bias, own):
    nb = q3.shape[0]
    whole = lambda shape: pl.BlockSpec(shape, lambda b, pg, bl: (0,) * len(shape))
    out = pl.pallas_call(
        functools.partial(_sample_moba_attn_body, own=own, layer=layer),
        grid_spec=pltpu.PrefetchScalarGridSpec(
            num_scalar_prefetch=2, grid=(nb,),
            in_specs=[pl.BlockSpec((1, N_HEADS, HEAD_DIM), lambda b, pg, bl: (b, 0, 0)),
                      pl.BlockSpec((1, N_KV_HEADS, HEAD_DIM), lambda b, pg, bl: (b, 0, 0)),
                      pl.BlockSpec((1, N_KV_HEADS, HEAD_DIM), lambda b, pg, bl: (b, 0, 0)),
                      whole((N_HEADS, 1, MOBA_BLOCK)), whole((N_HEADS, 1, LANES)), whole((N_HEADS, 1, LANES)),
                      pl.BlockSpec(memory_space=pl.ANY), pl.BlockSpec(memory_space=pl.ANY)],
            out_specs=pl.BlockSpec((1, N_HEADS, HEAD_DIM), lambda b, pg, bl: (b, 0, 0)),
            scratch_shapes=[pltpu.VMEM((N_HEADS, SEL_PAGES * PAGE_SIZE, HEAD_DIM), F32),
                            pltpu.VMEM((N_HEADS, SEL_PAGES * PAGE_SIZE, HEAD_DIM), F32),
                            pltpu.SemaphoreType.DMA((2,))]),
        out_shape=jax.ShapeDtypeStruct((nb, N_HEADS, HEAD_DIM), F32),
        compiler_params=_params(("arbitrary",)),
        name="sample_moba_attn",
    )(pages, blks, q3, k_new, v_new, prev_row, far, own_bias, cache_k, cache_v)
    return out.reshape(nb, Q_DIM)


def _indexer_score(qi, wi_col, keys_pad):
    s = lax.dot_general(qi.astype(BF16), keys_pad.astype(BF16), (((1,), (1,)), ((), ())),
                        preferred_element_type=F32)
    return jnp.sum(jnp.maximum(s * IDX_SCALE, 0.0) * wi_col, axis=0, keepdims=True) * IDX_W_SCALE


def _sample_score_body(pt_ref, qi_ref, wi_ref, *refs):
    pages, o_ref = refs[:SCORE_PAGES], refs[SCORE_PAGES]
    pad = jnp.zeros((PAGE_SIZE, LANES - IDX_DIM), F32)
    for p in range(SCORE_PAGES):
        keys = jnp.concatenate([pages[p][0, 0], pad], axis=1)
        o_ref[0, p:p + 1, :] = _indexer_score(qi_ref[0], wi_ref[0], keys)


def sample_dsa_score_call(cache_kidx, layer, pt_flat, qi_pad, wi_col, n_pages):
    nb = qi_pad.shape[0]
    specs = [_page_spec((1, 1, PAGE_SIZE, IDX_DIM), layer, functools.partial(
        lambda p, b, g: b * n_pages + g * SCORE_PAGES + p, p)) for p in range(SCORE_PAGES)]
    return pl.pallas_call(
        _sample_score_body,
        grid_spec=pltpu.PrefetchScalarGridSpec(
            num_scalar_prefetch=1, grid=(nb, n_pages // SCORE_PAGES),
            in_specs=[pl.BlockSpec((1, IDX_HEADS, LANES), lambda b, g, pt: (b, 0, 0)),
                      pl.BlockSpec((1, IDX_HEADS, 1), lambda b, g, pt: (b, 0, 0))] + specs,
            out_specs=pl.BlockSpec((1, SCORE_PAGES, PAGE_SIZE), lambda b, g, pt: (b, g, 0))),
        out_shape=jax.ShapeDtypeStruct((nb, n_pages, PAGE_SIZE), F32),
        compiler_params=_params(("arbitrary", "arbitrary")),
        name="sample_dsa_score",
    )(pt_flat, qi_pad, wi_col, *([cache_kidx] * SCORE_PAGES))


def _sample_dsa_select_body(sc_ref, qi_ref, wi_ref, ki_ref, m_ref, mnew_ref, *, n_sel):
    keys = _sortable_key(sc_ref[0])
    key_new = _sortable_key(_indexer_score(qi_ref[0], wi_ref[0], jnp.broadcast_to(ki_ref[0], (8, LANES))))[:, 0:1]

    def count_ge(cand):
        hit = jnp.where(keys >= cand, 1, 0).astype(I32)
        tot = jnp.sum(jnp.sum(hit, axis=0, keepdims=True), axis=1, keepdims=True)
        return tot + jnp.where(key_new >= cand, 1, 0).astype(I32)

    thr = _kth_largest_key(count_ge, n_sel, (1, 1))
    m_ref[0] = jnp.where(keys >= thr, 1.0, 0.0)
    mnew_ref[0] = jnp.broadcast_to(jnp.where(key_new >= thr, 1.0, 0.0), (1, LANES))


def sample_dsa_select_call(scores, qi_pad, wi_col, ki_pad, n_sel):
    nb, n_pages, _ = scores.shape
    return pl.pallas_call(
        functools.partial(_sample_dsa_select_body, n_sel=n_sel),
        grid=(nb,),
        in_specs=[pl.BlockSpec((1, n_pages, PAGE_SIZE), lambda b: (b, 0, 0)),
                  pl.BlockSpec((1, IDX_HEADS, LANES), lambda b: (b, 0, 0)),
                  pl.BlockSpec((1, IDX_HEADS, 1), lambda b: (b, 0, 0)),
                  pl.BlockSpec((1, 1, LANES), lambda b: (b, 0, 0))],
        out_specs=[pl.BlockSpec((1, n_pages, PAGE_SIZE), lambda b: (b, 0, 0)),
                   pl.BlockSpec((1, 1, LANES), lambda b: (b, 0, 0))],
        out_shape=[jax.ShapeDtypeStruct((nb, n_pages, PAGE_SIZE), F32), jax.ShapeDtypeStruct((nb, 1, LANES), F32)],
        compiler_params=_params(("arbitrary",)),
        name="sample_dsa_select",
    )(scores, qi_pad, wi_col, ki_pad)


def _row_copy(cache_hbm, layer, dst_vmem, row, slot, sem):
    page = lax.shift_right_logical(row, PAGE_SIZE.bit_length() - 1)
    off = row & (PAGE_SIZE - 1)
    return pltpu.make_async_copy(cache_hbm.at[layer, page, pl.ds(off, 1)], dst_vmem.at[pl.ds(slot, 1)], sem)


def _sample_dsa_attn_body(rows_ref, q_ref, valid_ref, bucket_ref, tab_ref, kn_ref, vn_ref, new_ref,
                          kc_ref, vc_ref, o_ref, kbuf, vbuf, sems, *, n_sel, layer):
    b = pl.program_id(0)

    def start(r, carry):
        row = rows_ref[b * n_sel + r]
        _row_copy(kc_ref, layer, kbuf, row, r, sems.at[0]).start()
        _row_copy(vc_ref, layer, vbuf, row, r, sems.at[1]).start()
        return carry

    def wait(r, carry):
        _row_copy(kc_ref, layer, kbuf, 0, r, sems.at[0]).wait()
        _row_copy(vc_ref, layer, vbuf, 0, r, sems.at[1]).wait()
        return carry

    lax.fori_loop(0, n_sel, start, 0)
    lax.fori_loop(0, n_sel, wait, 0)

    onehot = jnp.where(lax.broadcasted_iota(I32, (LANES, n_sel), 0) == bucket_ref[0], 1.0, 0.0)
    bias = jnp.dot(tab_ref[...], onehot, precision=lax.Precision.HIGHEST, preferred_element_type=F32)
    valid = valid_ref[0] > 0.5
    new_ok = new_ref[0, :, 0:1] > 0.5
    for kv in range(N_KV_HEADS):
        q8 = q_ref[0, kv]
        s = lax.dot_general(q8.astype(BF16), kbuf[:, kv, :].astype(BF16), (((1,), (1,)), ((), ())),
                            preferred_element_type=F32)
        bias8 = jnp.concatenate([bias[kv * GQA_GROUP:(kv + 1) * GQA_GROUP],
                                 jnp.zeros((8 - GQA_GROUP, n_sel), F32)], axis=0)
        s = jnp.where(valid, s * ATTN_SCALE + bias8, NEG)
        own_bias = jnp.concatenate([tab_ref[kv * GQA_GROUP:(kv + 1) * GQA_GROUP, 0:1],
                                    jnp.zeros((8 - GQA_GROUP, 1), F32)], axis=0)
        s_new = jnp.sum(q8 * kn_ref[0, kv:kv + 1, :], axis=-1, keepdims=True) * ATTN_SCALE + own_bias
        s_new = jnp.where(new_ok, s_new, NEG)
        m = jnp.maximum(jnp.max(s, axis=-1, keepdims=True), s_new)
        p = jnp.exp(s - m)
        p_new = jnp.exp(s_new - m)
        l = jnp.sum(p, axis=-1, keepdims=True) + p_new
        acc = jnp.dot(p.astype(BF16), vbuf[:, kv, :].astype(BF16), preferred_element_type=F32)
        o_ref[0, kv] = (acc + p_new * vn_ref[0, kv:kv + 1, :]) / l


def sample_dsa_attn_call(rows, q8, valid, bucket, table_t, k_new, v_new, new_sel, cache_k, cache_v, layer, n_sel):
    nb = q8.shape[0]
    return pl.pallas_call(
        functools.partial(_sample_dsa_attn_body, n_sel=n_sel, layer=layer),
        grid_spec=pltpu.PrefetchScalarGridSpec(
            num_scalar_prefetch=1, grid=(nb,),
            in_specs=[pl.BlockSpec((1, N_KV_HEADS, 8, HEAD_DIM), lambda b, rw: (b, 0, 0, 0)),
                      pl.BlockSpec((1, 1, n_sel), lambda b, rw: (b, 0, 0)),
                      pl.BlockSpec((1, 1, n_sel), lambda b, rw: (b, 0, 0)),
                      pl.BlockSpec((N_HEADS, LANES), lambda b, rw: (0, 0)),
                      pl.BlockSpec((1, N_KV_HEADS, HEAD_DIM), lambda b, rw: (b, 0, 0)),
                      pl.BlockSpec((1, N_KV_HEADS, HEAD_DIM), lambda b, rw: (b, 0, 0)),
                      pl.BlockSpec((1, 1, LANES), lambda b, rw: (b, 0, 0)),
                      pl.BlockSpec(memory_space=pl.ANY),
                      pl.BlockSpec(memory_space=pl.ANY)],
            out_specs=pl.BlockSpec((1, N_KV_HEADS, 8, HEAD_DIM), lambda b, rw: (b, 0, 0, 0)),
            scratch_shapes=[pltpu.VMEM((n_sel, N_KV_HEADS, HEAD_DIM), F32),
                            pltpu.VMEM((n_sel, N_KV_HEADS, HEAD_DIM), F32),
                            pltpu.SemaphoreType.DMA((2,))]),
        out_shape=jax.ShapeDtypeStruct((nb, N_KV_HEADS, 8, HEAD_DIM), F32),
        compiler_params=_params(("arbitrary",)),
        name="sample_dsa_attn",
    )(rows, q8, valid, bucket, table_t, k_new, v_new, new_sel, cache_k, cache_v)


def _rel_bucket(dist):
    n = jnp.maximum(dist, 0)
    exact = REL_BUCKETS // 2
    logr = jnp.log(jnp.maximum(n, 1).astype(F32) / exact) / math.log(REL_MAX_DIST / exact)
    large = jnp.minimum(exact + (logr * (REL_BUCKETS - exact)).astype(I32), REL_BUCKETS - 1)
    return jnp.where(n < exact, n, large)


def _head_gain(qn, kn, n_cols):
    return jnp.concatenate([jnp.tile(qn, N_HEADS), jnp.tile(kn, N_KV_HEADS),
                            jnp.ones((n_cols - Q_DIM - KV_DIM,), F32)]).reshape(1, n_cols)


def _project(x2d, g, sc, sh, w, layer, qn, kn, *, tm, rows_per_mod):
    n_w = w.shape[2]
    n_main = IN_DIM_A if n_w == IN_DIM_A else IN_DIM_A + IDX_Q_DIM
    z, k_out, v_out = norm_proj_call(x2d, g, sc, sh, w, layer, _head_gain(qn, kn, n_main), n_cols=n_main,
                                     n_norm_cols=Q_DIM + KV_DIM, tm=tm, rows_per_mod=rows_per_mod, kv_out=True)
    if n_w == IN_DIM_A:
        return z, k_out, v_out, None
    w_tail = jnp.pad(w[layer, :, n_main:], ((0, 0), (0, LANES - (n_w - n_main))))[None]
    tail, = norm_proj_call(x2d, g, sc, sh, w_tail, 0, jnp.ones((1, LANES), F32), n_cols=LANES, n_norm_cols=0,
                           tm=tm, rows_per_mod=rows_per_mod, tn=LANES)
    return z, k_out, v_out, tail


def moba_sample_attention(zs, k_new, v_new, cache_k, cache_v, layer, page_table, prev_row, far, own_bias):
    nb, n_pages = page_table.shape
    own = n_pages * PAGE_SIZE // MOBA_BLOCK
    kbar = paged_block_mean_call(cache_k, layer, page_table.reshape(-1), nb, n_pages)
    kbar_pad = jnp.pad(kbar, ((0, 0), (0, LANES - own), (0, 0), (0, 0)))
    q3 = zs[:, :Q_DIM].reshape(nb, N_HEADS, HEAD_DIM)
    sel = sample_moba_select_call(q3, kbar_pad, own)[:, :, :MOBA_TOPK]
    per_blk = MOBA_BLOCK // PAGE_SIZE
    logical = sel[..., None] * per_blk + jnp.arange(per_blk)
    pages = page_table[jnp.arange(nb)[:, None, None, None], logical]
    return sample_moba_attn_call(q3, k_new, v_new, cache_k, cache_v, layer, pages.reshape(-1), sel.reshape(-1),
                                 prev_row, far, own_bias, own)


def dsa_sample_attention(zs, tail_s, k_new, v_new, cache_k, cache_v, cache_kidx, layer, page_table, table_t):
    nb, n_pages = page_table.shape
    past = n_pages * PAGE_SIZE
    n_sel = min(DSA_TOPK, (past + 1) // 4)
    qi_pad = jnp.pad(zs[:, IN_DIM_A:].reshape(nb, IDX_HEADS, IDX_DIM), ((0, 0), (0, 0), (0, LANES - IDX_DIM)))
    wi_col = tail_s[:, IDX_DIM:IDX_DIM + IDX_HEADS].reshape(nb, IDX_HEADS, 1)
    ki_pad = jnp.pad(tail_s[:, :IDX_DIM], ((0, 0), (0, LANES - IDX_DIM))).reshape(nb, 1, LANES)
    scores = sample_dsa_score_call(cache_kidx, layer, page_table.reshape(-1), qi_pad, wi_col, n_pages)
    picked, new_sel = sample_dsa_select_call(scores, qi_pad, wi_col, ki_pad, n_sel)
    chosen = jnp.concatenate([picked.reshape(nb, past) > 0.5, new_sel[:, 0, :1] > 0.5], axis=1)
    idx = jnp.argsort(jnp.logical_not(chosen), axis=1, stable=True)[:, :n_sel].astype(I32)
    pos = jnp.minimum(idx, past - 1)
    rows = page_table[jnp.arange(nb)[:, None], pos // PAGE_SIZE] * PAGE_SIZE + pos % PAGE_SIZE
    valid = (idx < past).astype(F32)
    bucket = _rel_bucket(past - pos).astype(I32)
    q8 = jnp.pad(zs[:, :Q_DIM].reshape(nb, N_KV_HEADS, GQA_GROUP, HEAD_DIM),
                 ((0, 0), (0, 0), (0, 8 - GQA_GROUP), (0, 0)))
    o8 = sample_dsa_attn_call(rows.reshape(-1), q8, valid[:, None], bucket[:, None], table_t,
                              k_new, v_new, new_sel, cache_k, cache_v, layer, n_sel)
    return o8[:, :, :GQA_GROUP].reshape(nb, Q_DIM)


def kernel(x_prompt, x_sample, cache_k_a, cache_v_a, cache_k_b, cache_v_b, cache_kidx_b, page_table, c_prompt, c_sample, rel_table, ada_w, ada_b, norm_attn, norm_ffn, q_norm, k_norm, w_in_a, w_in_b, w_o, w_grp, b_grp, w_rt, b_rt, w_gate, w_up, w_down):
    b, s, d = x_prompt.shape
    nb = x_sample.shape[0]
    depth = ada_w.shape[0]
    n_pages = page_table.shape[1]
    past = n_pages * PAGE_SIZE
    own = past // MOBA_BLOCK
    assert x_sample.shape[1] == 1 and past % MOBA_BLOCK == 0 and MOBA_TOPK <= own <= LANES
    assert s % ROW_TILE == 0 and nb % 8 == 0

    n_mod_rows = -(-(b + nb) // 8) * 8
    c_all = jnp.concatenate([c_prompt, c_sample, jnp.zeros((n_mod_rows - b - nb, d), F32)])
    mod = adaln_call(c_all, ada_w, ada_b)

    bias = bias_tiles_call(rel_table)
    lane_rep = lambda v: jnp.broadcast_to(v[:, None, None], (N_HEADS, 1, LANES))
    far = lane_rep(rel_table[REL_BUCKETS - 1])
    own_bias = lane_rep(rel_table[0])
    prev_row = bias[:, 1, 0:1, :]
    table_t = jnp.pad(rel_table.T, ((0, 0), (0, LANES - REL_BUCKETS)))

    xp = x_prompt.reshape(b * s, d)
    xs = x_sample.reshape(nb, d)
    kv_p = {"ka": [], "va": [], "kb": [], "vb": [], "ib": []}
    kv_s = {"ka": [], "va": [], "kb": [], "vb": [], "ib": []}

    for i in range(depth):
        j = i // 2
        mp = mod[i, :b].reshape(b, 1, 6, d)
        ms = mod[i, b:b + nb].reshape(1, nb, 6, d)
        sh1, sc1, gt1, sh2, sc2, gt2 = (mp[:, :, k] for k in range(6))
        sh1s, sc1s, gt1s, sh2s, sc2s, gt2s = (ms[:, :, k] for k in range(6))
        moba = i % 2 == 0
        w_in = w_in_a if moba else w_in_b
        z, k_p, v_p, tail = _project(xp, norm_attn[i], sc1, sh1, w_in, j, q_norm[i], k_norm[i],
                                     tm=ROW_TILE, rows_per_mod=s)
        zs, k_s, v_s, tail_s = _project(xs, norm_attn[i], sc1s, sh1s, w_in, j, q_norm[i], k_norm[i],
                                        tm=nb, rows_per_mod=nb)
        z3 = z.reshape(b, s, -1)
        tag = "a" if moba else "b"
        kv_p["k" + tag].append(k_p.reshape(b, s, N_KV_HEADS, HEAD_DIM))
        kv_p["v" + tag].append(v_p.reshape(b, s, N_KV_HEADS, HEAD_DIM))
        kv_s["k" + tag].append(k_s.reshape(nb, 1, N_KV_HEADS, HEAD_DIM))
        kv_s["v" + tag].append(v_s.reshape(nb, 1, N_KV_HEADS, HEAD_DIM))

        if moba:
            sel = moba_select_call(z3, s // MOBA_BLOCK)
            o = attn_call(z3, sel, jnp.zeros((b, 1, s), I32), bias, far, dsa=False)

            o_s = moba_sample_attention(zs, k_s, v_s, cache_k_a, cache_v_a, j, page_table, prev_row, far, own_bias)
        else:
            tail3 = tail.reshape(b, s, LANES)
            kv_p["ib"].append(tail3[:, :, :IDX_DIM])
            kv_s["ib"].append(tail_s[:, None, :IDX_DIM])
            wi_t = jnp.swapaxes(tail3[:, :, IDX_DIM:IDX_DIM + IDX_HEADS], 1, 2)
            key_t, thr = dsa_score_call(z3, tail3, wi_t, min(DSA_TOPK, s // 4))
            o = attn_call(z3, key_t, thr, bias, far, dsa=True)

            o_s = dsa_sample_attention(zs, tail_s, k_s, v_s, cache_k_b, cache_v_b, cache_kidx_b, j, page_table,
                                       table_t)

        xp = linear_residual_call(o.reshape(b * s, Q_DIM), w_o, i, xp, gt1, tm=ROW_TILE, rows_per_mod=s)
        xs = linear_residual_call(o_s.astype(BF16), w_o, i, xs, gt1s, tm=nb, rows_per_mod=nb)
        xp, xs = moe_layer([xp, xs], [(sc2, sh2, gt2), (sc2s, sh2s, gt2s)], [s, nb], norm_ffn[i],
                           w_grp[i], b_grp[i], w_rt[i], b_rt[i], w_gate, w_up, w_down, i, tms=[512, nb])

    stack = lambda lst: jnp.stack(lst)
    return (xp.reshape(b, s, d), xs.reshape(nb, 1, d),
            stack(kv_p["ka"]), stack(kv_p["va"]), stack(kv_p["kb"]), stack(kv_p["vb"]), stack(kv_p["ib"]),
            stack(kv_s["ka"]), stack(kv_s["va"]), stack(kv_s["kb"]), stack(kv_s["vb"]), stack(kv_s["ib"]))
```

```python
import functools
import math

import numpy as np
import jax
import jax.numpy as jnp
from jax import lax
from jax.experimental import pallas as pl
from jax.experimental.pallas import tpu as pltpu

F32 = jnp.float32
BF16 = jnp.bfloat16
I32 = jnp.int32

D_MODEL = 2048
N_HEADS = 16
HEAD_DIM = 128
N_KV_HEADS = 4
GQA_GROUP = N_HEADS // N_KV_HEADS
Q_DIM = N_HEADS * HEAD_DIM
KV_DIM = N_KV_HEADS * HEAD_DIM
ATTN_SCALE = HEAD_DIM ** -0.5
MOBA_BLOCK = 256
MOBA_TOPK = 3
IDX_HEADS = 16
IDX_DIM = 64
IDX_SCALE = IDX_DIM ** -0.5
IDX_W_SCALE = IDX_HEADS ** -0.5
DSA_TOPK = 256
REL_BUCKETS = 32
REL_MAX_DIST = 128
N_GROUPS = 4
EXPERTS_PER_GROUP = 8
N_EXPERTS = N_GROUPS * EXPERTS_PER_GROUP
D_EXPERT = 768
PAGE_SIZE = 128
IN_DIM_A = Q_DIM + 2 * KV_DIM
IDX_Q_DIM = IDX_HEADS * IDX_DIM
RMS_EPS = 1e-6
NEG = -1e30
LOG2E = math.log2(math.e)

LANES = 128
ATT_TILE = 256
ROW_TILE = 1024
COL_TILE = 512
MOE_BLOCK = 256
FF_CHUNK = 256
VMEM_LIMIT = 56 * 1024 * 1024


def _params(sem):
    return pltpu.CompilerParams(dimension_semantics=sem, vmem_limit_bytes=VMEM_LIMIT)


def _adaln_body(c_ref, w_ref, b_ref, o_ref):
    c = c_ref[...]
    a = c * (1.0 / (1.0 + jnp.exp(-c)))
    o_ref[0] = jnp.dot(a, w_ref[0], preferred_element_type=F32) + b_ref[0]


def adaln_call(c_all, ada_w, ada_b):
    n_layers, d, n = ada_w.shape
    r = c_all.shape[0]
    tn = 1024
    return pl.pallas_call(
        _adaln_body,
        grid=(n_layers, n // tn),
        in_specs=[pl.BlockSpec((r, d), lambda l, j: (0, 0)),
                  pl.BlockSpec((1, d, tn), lambda l, j: (l, 0, j)),
                  pl.BlockSpec((1, 1, tn), lambda l, j: (l, 0, j))],
        out_specs=pl.BlockSpec((1, r, tn), lambda l, j: (l, 0, j)),
        out_shape=jax.ShapeDtypeStruct((n_layers, r, n), F32),
        compiler_params=_params(("arbitrary", "arbitrary")),
        name="adaln",
    )(c_all, ada_w, ada_b.reshape(n_layers, 1, n))


def _modulated_norm(x, g, sc, sh):
    ms = jnp.mean(x * x, axis=-1, keepdims=True)
    return (x * lax.rsqrt(ms + RMS_EPS) * g) * (1.0 + sc) + sh


def _norm_proj_body(x_ref, g_ref, sc_ref, sh_ref, w_ref, gain_ref, o_ref, *rest, n_norm_tiles, tn, kv_out):
    h_scr = rest[-1]
    j = pl.program_id(1)

    @pl.when(j == 0)
    def _():
        h_scr[...] = _modulated_norm(x_ref[...], g_ref[...], sc_ref[0], sh_ref[0]).astype(BF16)

    z = jnp.dot(h_scr[...], w_ref[0].astype(BF16), preferred_element_type=F32)

    @pl.when(j < n_norm_tiles)
    def _():
        for k in range(tn // HEAD_DIM):
            sl = slice(k * HEAD_DIM, (k + 1) * HEAD_DIM)
            zk = z[:, sl]
            ms = jnp.mean(zk * zk, axis=-1, keepdims=True)
            o_ref[:, sl] = zk * lax.rsqrt(ms + RMS_EPS) * gain_ref[:, sl]

    @pl.when(j >= n_norm_tiles)
    def _():
        o_ref[...] = z

    if kv_out:
        for ref, tile in ((rest[0], Q_DIM // tn), (rest[1], Q_DIM // tn + 1)):
            @pl.when(j == tile)
            def _(ref=ref):
                for kv in range(N_KV_HEADS):
                    ref[:, kv, :] = o_ref[:, kv * HEAD_DIM:(kv + 1) * HEAD_DIM]


def norm_proj_call(x2d, g, sc, sh, w, layer, gain, *, n_cols, n_norm_cols, tm, rows_per_mod, tn=COL_TILE,
                   kv_out=False):
    m, d = x2d.shape
    assert m % tm == 0 and n_cols % tn == 0 and n_norm_cols % tn == 0
    assert not kv_out or tn == KV_DIM
    r = sc.shape[1]
    tiles_per_mod = rows_per_mod // tm
    mod_spec = pl.BlockSpec((1, r, d), lambda i, j: (i // tiles_per_mod, 0, 0))
    out_specs = [pl.BlockSpec((tm, tn), lambda i, j: (i, j))]
    out_shape = [jax.ShapeDtypeStruct((m, n_cols), F32)]
    if kv_out:
        out_specs += [pl.BlockSpec((tm, N_KV_HEADS, HEAD_DIM), lambda i, j: (i, 0, 0))] * 2
        out_shape += [jax.ShapeDtypeStruct((m, N_KV_HEADS, HEAD_DIM), F32)] * 2
    return pl.pallas_call(
        functools.partial(_norm_proj_body, n_norm_tiles=n_norm_cols // tn, tn=tn, kv_out=kv_out),
        grid=(m // tm, n_cols // tn),
        in_specs=[pl.BlockSpec((tm, d), lambda i, j: (i, 0)),
                  pl.BlockSpec((1, d), lambda i, j: (0, 0)),
                  mod_spec, mod_spec,
                  pl.BlockSpec((1, d, tn), lambda i, j: (layer, 0, j)),
                  pl.BlockSpec((1, tn), lambda i, j: (0, j))],
        out_specs=out_specs,
        out_shape=out_shape,
        scratch_shapes=[pltpu.VMEM((tm, d), BF16)],
        compiler_params=_params(("arbitrary", "arbitrary")),
        name="norm_proj",
    )(x2d, g.reshape(1, d), sc, sh, w, gain)


def _linear_residual_body(a_ref, w_ref, x_ref, gt_ref, o_ref):
    y = jnp.dot(a_ref[...], w_ref[0].astype(BF16), preferred_element_type=F32)
    o_ref[...] = x_ref[...] + gt_ref[0] * y


def linear_residual_call(a, w, layer, x2d, gt, *, tm, rows_per_mod, tn=COL_TILE):
    m, k = a.shape
    n = w.shape[2]
    r = gt.shape[1]
    tiles_per_mod = rows_per_mod // tm
    return pl.pallas_call(
        _linear_residual_body,
        grid=(m // tm, n // tn),
        in_specs=[pl.BlockSpec((tm, k), lambda i, j: (i, 0)),
                  pl.BlockSpec((1, k, tn), lambda i, j: (layer, 0, j)),
                  pl.BlockSpec((tm, tn), lambda i, j: (i, j)),
                  pl.BlockSpec((1, r, tn), lambda i, j: (i // tiles_per_mod, 0, j))],
        out_specs=pl.BlockSpec((tm, tn), lambda i, j: (i, j)),
        out_shape=jax.ShapeDtypeStruct((m, n), F32),
        compiler_params=_params(("arbitrary", "arbitrary")),
        name="linear_residual",
    )(a, w, x2d, gt)


def _rel_bucket_np(dist):
    n = np.maximum(dist, 0)
    exact = REL_BUCKETS // 2
    logr = (np.log(np.maximum(n, 1).astype(np.float32) / np.float32(exact))
            / np.float32(math.log(REL_MAX_DIST / exact))).astype(np.float32)
    large = np.minimum(exact + (logr * np.float32(REL_BUCKETS - exact)).astype(np.int32), REL_BUCKETS - 1)
    return np.where(n < exact, n, large).astype(np.int32)


def _bias_body(bk_ref, tab_ref, o_ref):
    h = pl.program_id(0)
    bk = bk_ref[...]
    acc = jnp.zeros(bk.shape, F32)
    for b in range(REL_BUCKETS):
        acc = jnp.where(bk == b, tab_ref[b, h], acc)
    o_ref[0] = jnp.where(bk < 0, NEG, acc * LOG2E)


def bias_tiles_call(rel_table):
    t = np.arange(ATT_TILE)
    d = t[:, None] - t[None, :]
    buckets = np.stack([np.where(d >= 0, _rel_bucket_np(d), -1), _rel_bucket_np(d + ATT_TILE)]).astype(np.int32)
    return pl.pallas_call(
        _bias_body,
        grid=(N_HEADS,),
        in_specs=[pl.BlockSpec((2, ATT_TILE, ATT_TILE), lambda h: (0, 0, 0)),
                  pl.BlockSpec(memory_space=pltpu.SMEM)],
        out_specs=pl.BlockSpec((1, 2, ATT_TILE, ATT_TILE), lambda h: (h, 0, 0, 0)),
        out_shape=jax.ShapeDtypeStruct((N_HEADS, 2, ATT_TILE, ATT_TILE), F32),
        compiler_params=_params(("arbitrary",)),
        name="bias_tiles",
    )(jnp.asarray(buckets), rel_table)


def _moba_select_body(q_ref, k_ref, o_ref, kbar_scr, code_scr):
    i = pl.program_id(1)
    nblk = kbar_scr.shape[0]

    @pl.when(i == 0)
    def _():
        k = k_ref[0]
        kbar_scr[...] = jnp.mean(k.reshape(nblk, MOBA_BLOCK, KV_DIM), axis=1)
        code_scr[...] = jnp.zeros(code_scr.shape, F32)

    blk = lax.broadcasted_iota(I32, (nblk, ATT_TILE), 0)
    for h in range(N_HEADS):
        kv = h // GQA_GROUP
        gate = lax.dot_general(kbar_scr[:, kv * HEAD_DIM:(kv + 1) * HEAD_DIM].astype(BF16),
                               q_ref[0, :, h * HEAD_DIM:(h + 1) * HEAD_DIM].astype(BF16),
                               (((1,), (1,)), ((), ())), preferred_element_type=F32)
        gate = jnp.where(blk < i, gate, NEG)
        code = jnp.zeros((1, ATT_TILE), F32)
        for _ in range(MOBA_TOPK):
            mx = jnp.max(gate, axis=0, keepdims=True)
            idx = jnp.min(jnp.where(gate == mx, blk, nblk), axis=0, keepdims=True)
            code = code + jnp.where(idx < i, jnp.exp2(idx.astype(F32)), 0.0)
            gate = jnp.where(blk == idx, -jnp.inf, gate)
        code_scr[h:h + 1, :] = code
    o_ref[0] = code_scr[...].T


def moba_select_call(z3, n_blocks):
    b, s, _ = z3.shape
    return pl.pallas_call(
        _moba_select_body,
        grid=(b, s // ATT_TILE),
        in_specs=[pl.BlockSpec((1, ATT_TILE, Q_DIM), lambda bb, i: (bb, i, 0)),
                  pl.BlockSpec((1, s, KV_DIM), lambda bb, i: (bb, 0, Q_DIM // KV_DIM))],
        out_specs=pl.BlockSpec((1, ATT_TILE, LANES), lambda bb, i: (bb, i, 0)),
        out_shape=jax.ShapeDtypeStruct((b, s, LANES), F32),
        scratch_shapes=[pltpu.VMEM((n_blocks, KV_DIM), F32), pltpu.VMEM((LANES, ATT_TILE), F32)],
        compiler_params=_params(("arbitrary", "arbitrary")),
        name="moba_select",
    )(z3, z3)


def _flash_update(kv, q4, kb, vb, bias4, mask3, m_scr, l_scr, acc_scr):
    s = lax.dot_general(q4, kb, (((1,), (1,)), ((), ())), preferred_element_type=F32)
    s = s.reshape(GQA_GROUP, ATT_TILE, ATT_TILE) * (ATTN_SCALE * LOG2E) + bias4
    if mask3 is not None:
        s = jnp.where(mask3, s, NEG)
    m_prev = m_scr[kv]
    m_new = jnp.maximum(m_prev, jnp.max(s, axis=-1, keepdims=True))
    alpha = jnp.exp2(m_prev - m_new)
    p = jnp.exp2(s - jnp.concatenate([m_new] * (ATT_TILE // LANES), axis=-1))
    pv = jnp.dot(p.reshape(GQA_GROUP * ATT_TILE, ATT_TILE).astype(BF16), vb, preferred_element_type=F32)
    pv = pv.reshape(GQA_GROUP, ATT_TILE, HEAD_DIM + LANES)
    acc_scr[kv] = alpha * acc_scr[kv] + pv[:, :, :HEAD_DIM]
    l_scr[kv] = alpha * l_scr[kv] + pv[:, :, HEAD_DIM:]
    m_scr[kv] = m_new


def _attn_body(q_ref, k_ref, v_ref, sel_ref, thr_ref, bias_ref, far_ref, o_ref, m_scr, l_scr, acc_scr, code_scr,
               q_scr, *, dsa):
    i = pl.program_id(1)
    j = pl.program_id(2)

    @pl.when(j == 0)
    def _():
        m_scr[...] = jnp.full(m_scr.shape, NEG, F32)
        l_scr[...] = jnp.zeros(l_scr.shape, F32)
        acc_scr[...] = jnp.zeros(acc_scr.shape, F32)
        for h in range(N_HEADS):
            rows = pl.ds((h % GQA_GROUP) * ATT_TILE, ATT_TILE)
            q_scr[h // GQA_GROUP, rows, :] = q_ref[0, :, h * HEAD_DIM:(h + 1) * HEAD_DIM].astype(BF16)
        if not dsa:
            code = sel_ref[0].astype(I32)
            for h in range(N_HEADS):
                code_scr[h] = jnp.broadcast_to(code[:, h:h + 1], (ATT_TILE, LANES))

    def step(which):
        if dsa:
            mask_all = (jnp.where(sel_ref[0] >= thr_ref[0], 1.0, 0.0).T > 0.5)[None]
        ones = jnp.ones((ATT_TILE, LANES), BF16)
        for kv in range(N_KV_HEADS):
            heads = range(kv * GQA_GROUP, (kv + 1) * GQA_GROUP)
            q4 = q_scr[kv]
            kb = k_ref[0, :, kv * HEAD_DIM:(kv + 1) * HEAD_DIM].astype(BF16)
            vb = jnp.concatenate([v_ref[0, :, kv * HEAD_DIM:(kv + 1) * HEAD_DIM].astype(BF16), ones], axis=1)
            if which == 2:
                bias4 = far_ref[kv * GQA_GROUP:(kv + 1) * GQA_GROUP, :, 0:1]
            else:
                bias4 = bias_ref[kv * GQA_GROUP:(kv + 1) * GQA_GROUP, which]
            if dsa:
                mask3 = mask_all
            elif which == 0:
                mask3 = None
            else:
                bits = jnp.stack([jnp.right_shift(code_scr[h], j) & 1 for h in heads])
                mask3 = jnp.concatenate([bits] * (ATT_TILE // LANES), axis=-1) > 0
            _flash_update(kv, q4, kb, vb, bias4, mask3, m_scr, l_scr, acc_scr)

    @pl.when(j == i)
    def _():
        step(0)

    @pl.when(j == i - 1)
    def _():
        step(1)

    @pl.when(j < i - 1)
    def _():
        step(2)

    @pl.when(j == i)
    def _():
        for kv in range(N_KV_HEADS):
            out = acc_scr[kv] / l_scr[kv]
            for g in range(GQA_GROUP):
                h = kv * GQA_GROUP + g
                o_ref[0, :, h * HEAD_DIM:(h + 1) * HEAD_DIM] = out[g].astype(o_ref.dtype)


def attn_call(z3, sel, thr, bias, far, *, dsa):
    b, s, _ = z3.shape
    nt = s // ATT_TILE
    if dsa:
        sel_spec = pl.BlockSpec((1, ATT_TILE, ATT_TILE), lambda bb, i, j: (bb, jnp.minimum(j, i), i))
    else:
        sel_spec = pl.BlockSpec((1, ATT_TILE, LANES), lambda bb, i, j: (bb, i, 0))
    kcol, vcol = Q_DIM // KV_DIM, Q_DIM // KV_DIM + 1
    return pl.pallas_call(
        functools.partial(_attn_body, dsa=dsa),
        grid=(b, nt, nt),
        in_specs=[pl.BlockSpec((1, ATT_TILE, Q_DIM), lambda bb, i, j: (bb, i, 0)),
                  pl.BlockSpec((1, ATT_TILE, KV_DIM), lambda bb, i, j: (bb, jnp.minimum(j, i), kcol)),
                  pl.BlockSpec((1, ATT_TILE, KV_DIM), lambda bb, i, j: (bb, jnp.minimum(j, i), vcol)),
                  sel_spec,
                  pl.BlockSpec((1, 1, ATT_TILE), lambda bb, i, j: (bb, 0, i)),
                  pl.BlockSpec((N_HEADS, 2, ATT_TILE, ATT_TILE), lambda bb, i, j: (0, 0, 0, 0)),
                  pl.BlockSpec((N_HEADS, 1, LANES), lambda bb, i, j: (0, 0, 0))],
        out_specs=pl.BlockSpec((1, ATT_TILE, Q_DIM), lambda bb, i, j: (bb, i, 0)),
        out_shape=jax.ShapeDtypeStruct((b, s, Q_DIM), BF16),
        scratch_shapes=[pltpu.VMEM((N_KV_HEADS, GQA_GROUP, ATT_TILE, LANES), F32),
                        pltpu.VMEM((N_KV_HEADS, GQA_GROUP, ATT_TILE, LANES), F32),
                        pltpu.VMEM((N_KV_HEADS, GQA_GROUP, ATT_TILE, HEAD_DIM), F32),
                        pltpu.VMEM((1 if dsa else N_HEADS, ATT_TILE, LANES), I32),
                        pltpu.VMEM((N_KV_HEADS, GQA_GROUP * ATT_TILE, HEAD_DIM), BF16)],
        compiler_params=_params(("arbitrary", "arbitrary", "arbitrary")),
        name="dsa_attn" if dsa else "moba_attn",
    )(z3, z3, z3, sel, thr, bias, far)


def _sortable_key(x):
    bits = pltpu.bitcast(x, I32)
    return bits ^ (jnp.right_shift(bits, 31) & 0x7FFFFFFF)


def _kth_largest_key(count_ge, k, shape):
    def bit_body(bi, t):
        cand = t ^ jnp.left_shift(jnp.int32(1), 31 - bi)
        return jnp.where(count_ge(cand) >= k, cand, t)
    return lax.fori_loop(0, 32, bit_body, jnp.full(shape, -2 ** 31, I32))


def _dsa_score_body(qi_ref, tail_ref, wi_ref, key_ref, thr_ref, key_scr, *, n_sel):
    i = pl.program_id(1)
    j = pl.program_id(2)

    @pl.when(j <= i)
    def _():
        t = tail_ref[0]
        lane = lax.broadcasted_iota(I32, t.shape, 1)
        klo = jnp.where(lane < IDX_DIM, t, 0.0)
        khi = pltpu.roll(klo, IDX_DIM, 1)
        qstk = jnp.concatenate([qi_ref[0, :, p * LANES:(p + 1) * LANES] for p in range(IDX_HEADS // 2)],
                               axis=0).astype(BF16)
        dn = (((1,), (1,)), ((), ()))
        s_lo = lax.dot_general(klo.astype(BF16), qstk, dn, preferred_element_type=F32)
        s_hi = lax.dot_general(khi.astype(BF16), qstk, dn, preferred_element_type=F32)
        sc = jnp.zeros((ATT_TILE, ATT_TILE), F32)
        for p in range(IDX_HEADS // 2):
            sl = slice(p * ATT_TILE, (p + 1) * ATT_TILE)
            sc = sc + jnp.maximum(s_lo[:, sl], 0.0) * wi_ref[0, 2 * p:2 * p + 1, :]
            sc = sc + jnp.maximum(s_hi[:, sl], 0.0) * wi_ref[0, 2 * p + 1:2 * p + 2, :]
        kpos = j * ATT_TILE + lax.broadcasted_iota(I32, sc.shape, 0)
        qpos = i * ATT_TILE + lax.broadcasted_iota(I32, sc.shape, 1)
        key = _sortable_key(jnp.where(kpos <= qpos, sc, NEG))
        key_ref[0] = key
        key_scr[pl.ds(pl.multiple_of(j * ATT_TILE, ATT_TILE), ATT_TILE), :] = key

    @pl.when(j > i)
    def _():
        key_ref[0] = jnp.full((ATT_TILE, ATT_TILE), -2 ** 31, I32)

    @pl.when(j == i)
    def _():
        def count_ge(cand):
            def chunk(c, cnt):
                kc = key_scr[pl.ds(pl.multiple_of(c * ATT_TILE, ATT_TILE), ATT_TILE), :]
                hit = jnp.where(kc >= cand, 1, 0).astype(I32)
                return cnt + jnp.sum(hit.reshape(ATT_TILE // 8, 8, ATT_TILE), axis=0)
            cnt8 = lax.fori_loop(0, i + 1, chunk, jnp.zeros((8, ATT_TILE), I32))
            return jnp.sum(cnt8, axis=0, keepdims=True)
        thr_ref[0] = _kth_largest_key(count_ge, n_sel, (1, ATT_TILE))


def dsa_score_call(z3, tail3, wi_t, n_sel):
    b, s, _ = z3.shape
    nt = s // ATT_TILE
    return pl.pallas_call(
        functools.partial(_dsa_score_body, n_sel=n_sel),
        grid=(b, nt, nt),
        in_specs=[pl.BlockSpec((1, ATT_TILE, IDX_Q_DIM), lambda bb, i, j: (bb, i, IN_DIM_A // IDX_Q_DIM)),
                  pl.BlockSpec((1, ATT_TILE, LANES), lambda bb, i, j: (bb, jnp.minimum(j, i), 0)),
                  pl.BlockSpec((1, IDX_HEADS, ATT_TILE), lambda bb, i, j: (bb, 0, i))],
        out_specs=[pl.BlockSpec((1, ATT_TILE, ATT_TILE), lambda bb, i, j: (bb, j, i)),
                   pl.BlockSpec((1, 1, ATT_TILE), lambda bb, i, j: (bb, 0, i))],
        out_shape=[jax.ShapeDtypeStruct((b, s, s), I32), jax.ShapeDtypeStruct((b, 1, s), I32)],
        scratch_shapes=[pltpu.VMEM((s, ATT_TILE), I32)],
        compiler_params=_params(("arbitrary", "arbitrary", "arbitrary")),
        name="dsa_score",
    )(z3, tail3, wi_t)


def _router_body(x_ref, g_ref, sc_ref, sh_ref, wr_ref, br_ref, h_ref, r_ref):
    h = _modulated_norm(x_ref[...], g_ref[...], sc_ref[0], sh_ref[0])
    h_ref[...] = h.astype(h_ref.dtype)
    lg = jnp.dot(h.astype(BF16), wr_ref[...].astype(BF16), preferred_element_type=F32) + br_ref[...]
    lane = lax.broadcasted_iota(I32, lg.shape, 1)
    gl = jnp.where(lane < N_GROUPS, lg, -jnp.inf)
    gmax = jnp.max(gl, axis=-1, keepdims=True)
    g_sel = jnp.min(jnp.where(gl == gmax, lane, LANES), axis=-1, keepdims=True)
    g_w = 1.0 / jnp.sum(jnp.exp(gl - gmax), axis=-1, keepdims=True)
    lo = N_GROUPS + EXPERTS_PER_GROUP * g_sel
    el = jnp.where((lane >= lo) & (lane < lo + EXPERTS_PER_GROUP), lg, -jnp.inf)
    e1 = jnp.max(el, axis=-1, keepdims=True)
    i1 = jnp.min(jnp.where(el == e1, lane, LANES), axis=-1, keepdims=True)
    el2 = jnp.where(lane == i1, -jnp.inf, el)
    e2 = jnp.max(el2, axis=-1, keepdims=True)
    i2 = jnp.min(jnp.where(el2 == e2, lane, LANES), axis=-1, keepdims=True)
    v2 = jnp.exp(e2 - e1)
    w1 = g_w / (1.0 + v2)
    w2 = g_w * v2 / (1.0 + v2)
    r_ref[...] = jnp.where(lane == 0, (i1 - N_GROUPS).astype(F32),
                           jnp.where(lane == 1, (i2 - N_GROUPS).astype(F32),
                                     jnp.where(lane == 2, w1, jnp.where(lane == 3, w2, 0.0))))


def router_call(x2d, g, sc, sh, w_route, b_route, *, tm, rows_per_mod):
    m, d = x2d.shape
    r = sc.shape[1]
    tiles_per_mod = rows_per_mod // tm
    mod_spec = pl.BlockSpec((1, r, d), lambda i: (i // tiles_per_mod, 0, 0))
    return pl.pallas_call(
        _router_body,
        grid=(m // tm,),
        in_specs=[pl.BlockSpec((tm, d), lambda i: (i, 0)),
                  pl.BlockSpec((1, d), lambda i: (0, 0)),
                  mod_spec, mod_spec,
                  pl.BlockSpec((d, LANES), lambda i: (0, 0)),
                  pl.BlockSpec((1, LANES), lambda i: (0, 0))],
        out_specs=[pl.BlockSpec((tm, d), lambda i: (i, 0)), pl.BlockSpec((tm, LANES), lambda i: (i, 0))],
        out_shape=[jax.ShapeDtypeStruct((m, d), F32), jax.ShapeDtypeStruct((m, LANES), F32)],
        compiler_params=_params(("arbitrary",)),
        name="router",
    )(x2d, g.reshape(1, d), sc, sh, w_route, b_route)


def _row_dma_wait(ref_hbm, dst, n_rows, sem):
    pltpu.make_async_copy(ref_hbm.at[pl.ds(0, n_rows)], dst, sem).wait()


def _ffn_body(be_ref, nu_ref, cur_ref, nxt_ref, h_ref, wg_ref, wu_ref, wd_ref, o_ref, xbuf, sems):
    i = pl.program_id(0)
    nblk = pl.num_programs(0)
    nu = nu_ref[0]
    slot = i % 2

    def gather(src_ref, sl):
        for r in range(MOE_BLOCK):
            pltpu.make_async_copy(h_ref.at[pl.ds(src_ref[0, 0, r], 1)], xbuf.at[sl, pl.ds(r, 1)],
                                  sems.at[sl]).start()

    def gather_wait(sl):
        _row_dma_wait(h_ref, xbuf.at[sl], MOE_BLOCK, sems.at[sl])

    @pl.when(i == 0)
    def _():
        gather(cur_ref, 0)

    @pl.when(i < nu)
    def _():
        gather(nxt_ref, 1 - slot)
        gather_wait(slot)
        x = xbuf[slot].astype(BF16)
        acc = jnp.zeros(o_ref.shape, F32)
        for c in range(D_EXPERT // FF_CHUNK):
            sl = slice(c * FF_CHUNK, (c + 1) * FF_CHUNK)
            a = jnp.dot(x, wg_ref[0, 0, :, sl].astype(BF16), preferred_element_type=F32)
            u = jnp.dot(x, wu_ref[0, 0, :, sl].astype(BF16), preferred_element_type=F32)
            act = a * (1.0 / (1.0 + jnp.exp(-a))) * u
            acc = acc + jnp.dot(act.astype(BF16), wd_ref[0, 0, sl, :].astype(BF16), preferred_element_type=F32)
        o_ref[...] = acc

    @pl.when(i >= nu)
    def _():
        o_ref[...] = jnp.zeros(o_ref.shape, F32)

    @pl.when(i == nu)
    def _():
        gather_wait(slot)

    @pl.when((i == nblk - 1) & (nu == nblk))
    def _():
        gather_wait(1 - slot)


def ffn_call(blk_e, n_used, src_tok, h_all, w_gate, w_up, w_down, layer):
    d = h_all.shape[1]
    nblk = blk_e.shape[0]
    last = lambda i, nu: jnp.minimum(i, nu[0] - 1)
    src3 = src_tok.reshape(nblk, 1, MOE_BLOCK)
    w_in_spec = pl.BlockSpec((1, 1, d, D_EXPERT), lambda i, be, nu: (layer, be[last(i, nu)], 0, 0))
    grid_spec = pltpu.PrefetchScalarGridSpec(
        num_scalar_prefetch=2,
        grid=(nblk,),
        in_specs=[pl.BlockSpec((1, 1, MOE_BLOCK), lambda i, be, nu: (i, 0, 0), memory_space=pltpu.SMEM),
                  pl.BlockSpec((1, 1, MOE_BLOCK), lambda i, be, nu: (jnp.minimum(i + 1, nblk - 1), 0, 0),
                               memory_space=pltpu.SMEM),
                  pl.BlockSpec(memory_space=pl.ANY),
                  w_in_spec, w_in_spec,
                  pl.BlockSpec((1, 1, D_EXPERT, d), lambda i, be, nu: (layer, be[last(i, nu)], 0, 0))],
        out_specs=pl.BlockSpec((MOE_BLOCK, d), lambda i, be, nu: (i, 0)),
        scratch_shapes=[pltpu.VMEM((2, MOE_BLOCK, d), F32), pltpu.SemaphoreType.DMA((2,))],
    )
    return pl.pallas_call(
        _ffn_body,
        grid_spec=grid_spec,
        out_shape=jax.ShapeDtypeStruct((nblk * MOE_BLOCK, d), F32),
        compiler_params=_params(("arbitrary",)),
        name="expert_ffn",
    )(blk_e, n_used, src3, src3, h_all, w_gate, w_up, w_down)


def _combine_body(dcur_ref, dnxt_ref, x_ref, rt_ref, gt_ref, ob_ref, o_ref, rbuf, sems, *, tm):
    t = pl.program_id(0)
    slot = t % 2

    def issue(dref, sl):
        for r in range(2 * tm):
            dst = (r % 2) * tm + r // 2
            pltpu.make_async_copy(ob_ref.at[pl.ds(dref[0, 0, r], 1)], rbuf.at[sl, pl.ds(dst, 1)],
                                  sems.at[sl]).start()

    @pl.when(t == 0)
    def _():
        issue(dcur_ref, 0)

    @pl.when(t + 1 < pl.num_programs(0))
    def _():
        issue(dnxt_ref, 1 - slot)

    _row_dma_wait(ob_ref, rbuf.at[slot], 2 * tm, sems.at[slot])
    w0 = rt_ref[:, 2:3]
    w1 = rt_ref[:, 3:4]
    o_ref[...] = x_ref[...] + gt_ref[0] * (w0 * rbuf[slot, 0:tm] + w1 * rbuf[slot, tm:2 * tm])


def combine_call(x2d, out_buf, dest, route, gt, *, tm, rows_per_mod):
    m, d = x2d.shape
    r = gt.shape[1]
    nt = m // tm
    tiles_per_mod = rows_per_mod // tm
    dest3 = dest.reshape(nt, 1, 2 * tm)
    return pl.pallas_call(
        functools.partial(_combine_body, tm=tm),
        grid=(nt,),
        in_specs=[pl.BlockSpec((1, 1, 2 * tm), lambda t: (t, 0, 0), memory_space=pltpu.SMEM),
                  pl.BlockSpec((1, 1, 2 * tm), lambda t: (jnp.minimum(t + 1, nt - 1), 0, 0),
                               memory_space=pltpu.SMEM),
                  pl.BlockSpec((tm, d), lambda t: (t, 0)),
                  pl.BlockSpec((tm, LANES), lambda t: (t, 0)),
                  pl.BlockSpec((1, r, d), lambda t: (t // tiles_per_mod, 0, 0)),
                  pl.BlockSpec(memory_space=pl.ANY)],
        out_specs=pl.BlockSpec((tm, d), lambda t: (t, 0)),
        out_shape=jax.ShapeDtypeStruct((m, d), F32),
        scratch_shapes=[pltpu.VMEM((2, 2 * tm, d), F32), pltpu.SemaphoreType.DMA((2,))],
        compiler_params=_params(("arbitrary",)),
        name="moe_combine",
    )(dest3, dest3, x2d, route, gt, out_buf)


def moe_dispatch(eid):
    a = eid.size
    flat_e = eid.reshape(a)
    order = jnp.argsort(flat_e, stable=True)
    rank = jnp.argsort(order)
    counts = jnp.bincount(flat_e, length=N_EXPERTS)
    padded = (counts + MOE_BLOCK - 1) // MOE_BLOCK * MOE_BLOCK
    pend = jnp.cumsum(padded)
    pstart = pend - padded
    cstart = jnp.cumsum(counts) - counts
    dest = (pstart[flat_e] + rank - cstart[flat_e]).astype(I32)
    nblk = -(-(a + N_EXPERTS * (MOE_BLOCK - 1)) // MOE_BLOCK)
    blk_e = jnp.minimum(jnp.searchsorted(pend, jnp.arange(nblk) * MOE_BLOCK, side='right'),
                        N_EXPERTS - 1).astype(I32)
    n_used = (pend[-1] // MOE_BLOCK).astype(I32).reshape(1)
    row = jnp.arange(nblk * MOE_BLOCK)
    row_e = jnp.repeat(blk_e, MOE_BLOCK)
    k = row - pstart[row_e]
    filled = (k >= 0) & (k < counts[row_e])
    src_tok = jnp.where(filled, order[jnp.clip(cstart[row_e] + k, 0, a - 1)] // 2, 0).astype(I32)
    return dest, src_tok, blk_e, n_used


def moe_layer(xs, mods, rows_per_mods, g_ffn, w_grp, b_grp, w_rt, b_rt, w_gate, w_up, w_down, layer, tms):
    n_route = N_GROUPS + N_EXPERTS
    w_route = jnp.pad(jnp.concatenate([w_grp, w_rt], axis=1), ((0, 0), (0, LANES - n_route)))
    b_route = jnp.pad(jnp.concatenate([b_grp, b_rt]), (0, LANES - n_route)).reshape(1, LANES)
    hs, routes = [], []
    for x, (sc, sh, _), rpm, tm in zip(xs, mods, rows_per_mods, tms):
        h, rt = router_call(x, g_ffn, sc, sh, w_route, b_route, tm=tm, rows_per_mod=rpm)
        hs.append(h)
        routes.append(rt)
    route = jnp.concatenate(routes) if len(routes) > 1 else routes[0]
    dest, src_tok, blk_e, n_used = moe_dispatch(route[:, :2].astype(I32))
    h_all = jnp.concatenate(hs) if len(hs) > 1 else hs[0]
    out_buf = ffn_call(blk_e, n_used, src_tok, h_all, w_gate, w_up, w_down, layer)
    outs, off = [], 0
    for x, (_, _, gt), rt, rpm, tm in zip(xs, mods, routes, rows_per_mods, tms):
        m = x.shape[0]
        outs.append(combine_call(x, out_buf, dest[2 * off:2 * (off + m)], rt, gt,
                                 tm=min(tm, MOE_BLOCK), rows_per_mod=rpm))
        off += m
    return outs


MEAN_PAGES = 16
SCORE_PAGES = 16


def _page_spec(block, layer, slot_of):
    def index_map(*args):
        ids, pages = args[:2], args[2]
        return (layer, pages[slot_of(*ids)]) + (0,) * (len(block) - 2)
    return pl.BlockSpec(block, index_map)


def _block_mean_body(pt_ref, *refs):
    pages, o_ref = refs[:MEAN_PAGES], refs[MEAN_PAGES]
    per_blk = MOBA_BLOCK // PAGE_SIZE
    for blk in range(MEAN_PAGES // per_blk):
        tot = sum(jnp.sum(pages[blk * per_blk + p][0, 0], axis=0) for p in range(per_blk))
        o_ref[0, 0, blk] = tot * (1.0 / MOBA_BLOCK)


def paged_block_mean_call(cache, layer, pt_flat, nb, n_pages):
    per_step = MEAN_PAGES * PAGE_SIZE // MOBA_BLOCK
    specs = [_page_spec((1, 1, PAGE_SIZE, N_KV_HEADS, HEAD_DIM), layer, functools.partial(
        lambda p, b, g: b * n_pages + g * MEAN_PAGES + p, p)) for p in range(MEAN_PAGES)]
    out = pl.pallas_call(
        _block_mean_body,
        grid_spec=pltpu.PrefetchScalarGridSpec(
            num_scalar_prefetch=1, grid=(nb, n_pages // MEAN_PAGES), in_specs=specs,
            out_specs=pl.BlockSpec((1, 1, per_step, N_KV_HEADS, HEAD_DIM), lambda b, g, pt: (b, g, 0, 0, 0))),
        out_shape=jax.ShapeDtypeStruct((nb, n_pages // MEAN_PAGES, per_step, N_KV_HEADS, HEAD_DIM), F32),
        compiler_params=_params(("arbitrary", "arbitrary")),
        name="paged_block_mean",
    )(pt_flat, *([cache] * MEAN_PAGES))
    return out.reshape(nb, n_pages * PAGE_SIZE // MOBA_BLOCK, N_KV_HEADS, HEAD_DIM)


def _sample_select_body(q_ref, kbar_ref, o_ref, *, own):
    q = q_ref[0]
    row = lax.broadcasted_iota(I32, (N_HEADS, LANES), 0)
    lane = lax.broadcasted_iota(I32, (N_HEADS, LANES), 1)
    gate = jnp.zeros((N_HEADS, LANES), F32)
    for kv in range(N_KV_HEADS):
        gk = lax.dot_general(q.astype(BF16), kbar_ref[0, :, kv, :].astype(BF16), (((1,), (1,)), ((), ())),
                             preferred_element_type=F32)
        gate = jnp.where(row // GQA_GROUP == kv, gk, gate)
    gate = jnp.where(lane < own, gate, NEG)
    out = jnp.zeros((N_HEADS, LANES), I32)
    for c in range(MOBA_TOPK):
        mx = jnp.max(gate, axis=-1, keepdims=True)
        idx = jnp.min(jnp.where(gate == mx, lane, LANES), axis=-1, keepdims=True)
        out = jnp.where(lane == c, idx, out)
        gate = jnp.where(lane == idx, -jnp.inf, gate)
    o_ref[0] = out


def sample_moba_select_call(q3, kbar_pad, own):
    nb = q3.shape[0]
    return pl.pallas_call(
        functools.partial(_sample_select_body, own=own),
        grid=(nb,),
        in_specs=[pl.BlockSpec((1, N_HEADS, HEAD_DIM), lambda b: (b, 0, 0)),
                  pl.BlockSpec((1, LANES, N_KV_HEADS, HEAD_DIM), lambda b: (b, 0, 0, 0))],
        out_specs=pl.BlockSpec((1, N_HEADS, LANES), lambda b: (b, 0, 0)),
        out_shape=jax.ShapeDtypeStruct((nb, N_HEADS, LANES), I32),
        compiler_params=_params(("arbitrary",)),
        name="sample_moba_select",
    )(q3, kbar_pad)


SEL_PAGES = MOBA_TOPK * (MOBA_BLOCK // PAGE_SIZE)


def _sample_moba_attn_body(pg_ref, blk_ref, q_ref, kn_ref, vn_ref, prev_ref, far_ref, own_ref, kc_ref, vc_ref,
                           o_ref, kbuf, vbuf, sems, *, own, layer):
    b = pl.program_id(0)
    per_blk = MOBA_BLOCK // PAGE_SIZE
    copies = []
    for h in range(N_HEADS):
        kv = h // GQA_GROUP
        for s in range(SEL_PAGES):
            page = pg_ref[(b * N_HEADS + h) * SEL_PAGES + s]
            rows = pl.ds(s * PAGE_SIZE, PAGE_SIZE)
            copies.append(pltpu.make_async_copy(kc_ref.at[layer, page, :, kv, :], kbuf.at[h, rows], sems.at[0]))
            copies.append(pltpu.make_async_copy(vc_ref.at[layer, page, :, kv, :], vbuf.at[h, rows], sems.at[1]))
    for cp in copies:
        cp.start()
    for cp in copies:
        cp.wait()

    for h in range(N_HEADS):
        kv = h // GQA_GROUP
        q = q_ref[0, h:h + 1, :]
        q8 = jnp.broadcast_to(q, (8, HEAD_DIM)).astype(BF16)
        s = lax.dot_general(q8, kbuf[h].astype(BF16), (((1,), (1,)), ((), ())),
                            preferred_element_type=F32)
        far = jnp.broadcast_to(far_ref[h, :, 0:1], (1, MOBA_BLOCK))
        bias = jnp.concatenate(
            [jnp.where(blk_ref[(b * N_HEADS + h) * MOBA_TOPK + n] == own - 1, prev_ref[h], far)
             for n in range(MOBA_TOPK)], axis=1)
        s = s * ATTN_SCALE + bias
        s_own = (jnp.sum(q * kn_ref[0, kv:kv + 1, :], axis=-1, keepdims=True) * ATTN_SCALE
                 + own_ref[h, :, 0:1])
        m = jnp.maximum(jnp.max(s, axis=-1, keepdims=True), s_own)
        p = jnp.exp(s - m)
        p_own = jnp.exp(s_own - m)
        l = jnp.sum(p, axis=-1, keepdims=True) + p_own
        acc = jnp.dot(p.astype(BF16), vbuf[h].astype(BF16), preferred_element_type=F32)
        o_ref[0, h:h + 1, :] = ((acc + p_own * vn_ref[0, kv:kv + 1, :]) / l)[0:1]


def sample_moba_attn_call(q3, k_new, v_new, cache_k, cache_v, layer, pages, blks, prev_row, far, own_bias, own):
    nb = q3.shape[0]
    whole = lambda shape: pl.BlockSpec(shape, lambda b, pg, bl: (0,) * len(shape))
    out = pl.pallas_call(
        functools.partial(_sample_moba_attn_body, own=own, layer=layer),
        grid_spec=pltpu.PrefetchScalarGridSpec(
            num_scalar_prefetch=2, grid=(nb,),
            in_specs=[pl.BlockSpec((1, N_HEADS, HEAD_DIM), lambda b, pg, bl: (b, 0, 0)),
                      pl.BlockSpec((1, N_KV_HEADS, HEAD_DIM), lambda b, pg, bl: (b, 0, 0)),
                      pl.BlockSpec((1, N_KV_HEADS, HEAD_DIM), lambda b, pg, bl: (b, 0, 0)),
                      whole((N_HEADS, 1, MOBA_BLOCK)), whole((N_HEADS, 1, LANES)), whole((N_HEADS, 1, LANES)),
                      pl.BlockSpec(memory_space=pl.ANY), pl.BlockSpec(memory_space=pl.ANY)],
            out_specs=pl.BlockSpec((1, N_HEADS, HEAD_DIM), lambda b, pg, bl: (b, 0, 0)),
            scratch_shapes=[pltpu.VMEM((N_HEADS, SEL_PAGES * PAGE_SIZE, HEAD_DIM), F32),
                            pltpu.VMEM((N_HEADS, SEL_PAGES * PAGE_SIZE, HEAD_DIM), F32),
                            pltpu.SemaphoreType.DMA((2,))]),
        out_shape=jax.ShapeDtypeStruct((nb, N_HEADS, HEAD_DIM), F32),
        compiler_params=_params(("arbitrary",)),
        name="sample_moba_attn",
    )(pages, blks, q3, k_new, v_new, prev_row, far, own_bias, cache_k, cache_v)
    return out.reshape(nb, Q_DIM)


def _indexer_score(qi, wi_col, keys_pad):
    s = lax.dot_general(qi.astype(BF16), keys_pad.astype(BF16), (((1,), (1,)), ((), ())),
                        preferred_element_type=F32)
    return jnp.sum(jnp.maximum(s * IDX_SCALE, 0.0) * wi_col, axis=0, keepdims=True) * IDX_W_SCALE


def _sample_score_body(pt_ref, qi_ref, wi_ref, *refs):
    pages, o_ref = refs[:SCORE_PAGES], refs[SCORE_PAGES]
    pad = jnp.zeros((PAGE_SIZE, LANES - IDX_DIM), F32)
    for p in range(SCORE_PAGES):
        keys = jnp.concatenate([pages[p][0, 0], pad], axis=1)
        o_ref[0, p:p + 1, :] = _indexer_score(qi_ref[0], wi_ref[0], keys)


def sample_dsa_score_call(cache_kidx, layer, pt_flat, qi_pad, wi_col, n_pages):
    nb = qi_pad.shape[0]
    specs = [_page_spec((1, 1, PAGE_SIZE, IDX_DIM), layer, functools.partial(
        lambda p, b, g: b * n_pages + g * SCORE_PAGES + p, p)) for p in range(SCORE_PAGES)]
    return pl.pallas_call(
        _sample_score_body,
        grid_spec=pltpu.PrefetchScalarGridSpec(
            num_scalar_prefetch=1, grid=(nb, n_pages // SCORE_PAGES),
            in_specs=[pl.BlockSpec((1, IDX_HEADS, LANES), lambda b, g, pt: (b, 0, 0)),
                      pl.BlockSpec((1, IDX_HEADS, 1), lambda b, g, pt: (b, 0, 0))] + specs,
            out_specs=pl.BlockSpec((1, SCORE_PAGES, PAGE_SIZE), lambda b, g, pt: (b, g, 0))),
        out_shape=jax.ShapeDtypeStruct((nb, n_pages, PAGE_SIZE), F32),
        compiler_params=_params(("arbitrary", "arbitrary")),
        name="sample_dsa_score",
    )(pt_flat, qi_pad, wi_col, *([cache_kidx] * SCORE_PAGES))


def _sample_dsa_select_body(sc_ref, qi_ref, wi_ref, ki_ref, m_ref, mnew_ref, *, n_sel):
    keys = _sortable_key(sc_ref[0])
    key_new = _sortable_key(_indexer_score(qi_ref[0], wi_ref[0], jnp.broadcast_to(ki_ref[0], (8, LANES))))[:, 0:1]

    def count_ge(cand):
        hit = jnp.where(keys >= cand, 1, 0).astype(I32)
        tot = jnp.sum(jnp.sum(hit, axis=0, keepdims=True), axis=1, keepdims=True)
        return tot + jnp.where(key_new >= cand, 1, 0).astype(I32)

    thr = _kth_largest_key(count_ge, n_sel, (1, 1))
    m_ref[0] = jnp.where(keys >= thr, 1.0, 0.0)
    mnew_ref[0] = jnp.broadcast_to(jnp.where(key_new >= thr, 1.0, 0.0), (1, LANES))


def sample_dsa_select_call(scores, qi_pad, wi_col, ki_pad, n_sel):
    nb, n_pages, _ = scores.shape
    return pl.pallas_call(
        functools.partial(_sample_dsa_select_body, n_sel=n_sel),
        grid=(nb,),
        in_specs=[pl.BlockSpec((1, n_pages, PAGE_SIZE), lambda b: (b, 0, 0)),
                  pl.BlockSpec((1, IDX_HEADS, LANES), lambda b: (b, 0, 0)),
                  pl.BlockSpec((1, IDX_HEADS, 1), lambda b: (b, 0, 0)),
                  pl.BlockSpec((1, 1, LANES), lambda b: (b, 0, 0))],
        out_specs=[pl.BlockSpec((1, n_pages, PAGE_SIZE), lambda b: (b, 0, 0)),
                   pl.BlockSpec((1, 1, LANES), lambda b: (b, 0, 0))],
        out_shape=[jax.ShapeDtypeStruct((nb, n_pages, PAGE_SIZE), F32), jax.ShapeDtypeStruct((nb, 1, LANES), F32)],
        compiler_params=_params(("arbitrary",)),
        name="sample_dsa_select",
    )(scores, qi_pad, wi_col, ki_pad)


def _row_copy(cache_hbm, layer, dst_vmem, row, slot, sem):
    page = lax.shift_right_logical(row, PAGE_SIZE.bit_length() - 1)
    off = row & (PAGE_SIZE - 1)
    return pltpu.make_async_copy(cache_hbm.at[layer, page, pl.ds(off, 1)], dst_vmem.at[pl.ds(slot, 1)], sem)


def _sample_dsa_attn_body(rows_ref, q_ref, valid_ref, bucket_ref, tab_ref, kn_ref, vn_ref, new_ref,
                          kc_ref, vc_ref, o_ref, kbuf, vbuf, sems, *, n_sel, layer):
    b = pl.program_id(0)

    def start(r, carry):
        row = rows_ref[b * n_sel + r]
        _row_copy(kc_ref, layer, kbuf, row, r, sems.at[0]).start()
        _row_copy(vc_ref, layer, vbuf, row, r, sems.at[1]).start()
        return carry

    def wait(r, carry):
        _row_copy(kc_ref, layer, kbuf, 0, r, sems.at[0]).wait()
        _row_copy(vc_ref, layer, vbuf, 0, r, sems.at[1]).wait()
        return carry

    lax.fori_loop(0, n_sel, start, 0)
    lax.fori_loop(0, n_sel, wait, 0)

    onehot = jnp.where(lax.broadcasted_iota(I32, (LANES, n_sel), 0) == bucket_ref[0], 1.0, 0.0)
    bias = jnp.dot(tab_ref[...], onehot, precision=lax.Precision.HIGHEST, preferred_element_type=F32)
    valid = valid_ref[0] > 0.5
    new_ok = new_ref[0, :, 0:1] > 0.5
    for kv in range(N_KV_HEADS):
        q8 = q_ref[0, kv]
        s = lax.dot_general(q8.astype(BF16), kbuf[:, kv, :].astype(BF16), (((1,), (1,)), ((), ())),
                            preferred_element_type=F32)
        bias8 = jnp.concatenate([bias[kv * GQA_GROUP:(kv + 1) * GQA_GROUP],
                                 jnp.zeros((8 - GQA_GROUP, n_sel), F32)], axis=0)
        s = jnp.where(valid, s * ATTN_SCALE + bias8, NEG)
        own_bias = jnp.concatenate([tab_ref[kv * GQA_GROUP:(kv + 1) * GQA_GROUP, 0:1],
                                    jnp.zeros((8 - GQA_GROUP, 1), F32)], axis=0)
        s_new = jnp.sum(q8 * kn_ref[0, kv:kv + 1, :], axis=-1, keepdims=True) * ATTN_SCALE + own_bias
        s_new = jnp.where(new_ok, s_new, NEG)
        m = jnp.maximum(jnp.max(s, axis=-1, keepdims=True), s_new)
        p = jnp.exp(s - m)
        p_new = jnp.exp(s_new - m)
        l = jnp.sum(p, axis=-1, keepdims=True) + p_new
        acc = jnp.dot(p.astype(BF16), vbuf[:, kv, :].astype(BF16), preferred_element_type=F32)
        o_ref[0, kv] = (acc + p_new * vn_ref[0, kv:kv + 1, :]) / l


def sample_dsa_attn_call(rows, q8, valid, bucket, table_t, k_new, v_new, new_sel, cache_k, cache_v, layer, n_sel):
    nb = q8.shape[0]
    return pl.pallas_call(
        functools.partial(_sample_dsa_attn_body, n_sel=n_sel, layer=layer),
        grid_spec=pltpu.PrefetchScalarGridSpec(
            num_scalar_prefetch=1, grid=(nb,),
            in_specs=[pl.BlockSpec((1, N_KV_HEADS, 8, HEAD_DIM), lambda b, rw: (b, 0, 0, 0)),
                      pl.BlockSpec((1, 1, n_sel), lambda b, rw: (b, 0, 0)),
                      pl.BlockSpec((1, 1, n_sel), lambda b, rw: (b, 0, 0)),
                      pl.BlockSpec((N_HEADS, LANES), lambda b, rw: (0, 0)),
                      pl.BlockSpec((1, N_KV_HEADS, HEAD_DIM), lambda b, rw: (b, 0, 0)),
                      pl.BlockSpec((1, N_KV_HEADS, HEAD_DIM), lambda b, rw: (b, 0, 0)),
                      pl.BlockSpec((1, 1, LANES), lambda b, rw: (b, 0, 0)),
                      pl.BlockSpec(memory_space=pl.ANY),
                      pl.BlockSpec(memory_space=pl.ANY)],
            out_specs=pl.BlockSpec((1, N_KV_HEADS, 8, HEAD_DIM), lambda b, rw: (b, 0, 0, 0)),
            scratch_shapes=[pltpu.VMEM((n_sel, N_KV_HEADS, HEAD_DIM), F32),
                            pltpu.VMEM((n_sel, N_KV_HEADS, HEAD_DIM), F32),
                            pltpu.SemaphoreType.DMA((2,))]),
        out_shape=jax.ShapeDtypeStruct((nb, N_KV_HEADS, 8, HEAD_DIM), F32),
        compiler_params=_params(("arbitrary",)),
        name="sample_dsa_attn",
    )(rows, q8, valid, bucket, table_t, k_new, v_new, new_sel, cache_k, cache_v)


def _rel_bucket(dist):
    n = jnp.maximum(dist, 0)
    exact = REL_BUCKETS // 2
    logr = jnp.log(jnp.maximum(n, 1).astype(F32) / exact) / math.log(REL_MAX_DIST / exact)
    large = jnp.minimum(exact + (logr * (REL_BUCKETS - exact)).astype(I32), REL_BUCKETS - 1)
    return jnp.where(n < exact, n, large)


def _head_gain(qn, kn, n_cols):
    return jnp.concatenate([jnp.tile(qn, N_HEADS), jnp.tile(kn, N_KV_HEADS),
                            jnp.ones((n_cols - Q_DIM - KV_DIM,), F32)]).reshape(1, n_cols)


def _project(x2d, g, sc, sh, w, layer, qn, kn, *, tm, rows_per_mod):
    n_w = w.shape[2]
    n_main = IN_DIM_A if n_w == IN_DIM_A else IN_DIM_A + IDX_Q_DIM
    z, k_out, v_out = norm_proj_call(x2d, g, sc, sh, w, layer, _head_gain(qn, kn, n_main), n_cols=n_main,
                                     n_norm_cols=Q_DIM + KV_DIM, tm=tm, rows_per_mod=rows_per_mod, kv_out=True)
    if n_w == IN_DIM_A:
        return z, k_out, v_out, None
    w_tail = jnp.pad(w[layer, :, n_main:], ((0, 0), (0, LANES - (n_w - n_main))))[None]
    tail, = norm_proj_call(x2d, g, sc, sh, w_tail, 0, jnp.ones((1, LANES), F32), n_cols=LANES, n_norm_cols=0,
                           tm=tm, rows_per_mod=rows_per_mod, tn=LANES)
    return z, k_out, v_out, tail


def moba_sample_attention(zs, k_new, v_new, cache_k, cache_v, layer, page_table, prev_row, far, own_bias):
    nb, n_pages = page_table.shape
    own = n_pages * PAGE_SIZE // MOBA_BLOCK
    kbar = paged_block_mean_call(cache_k, layer, page_table.reshape(-1), nb, n_pages)
    kbar_pad = jnp.pad(kbar, ((0, 0), (0, LANES - own), (0, 0), (0, 0)))
    q3 = zs[:, :Q_DIM].reshape(nb, N_HEADS, HEAD_DIM)
    sel = sample_moba_select_call(q3, kbar_pad, own)[:, :, :MOBA_TOPK]
    per_blk = MOBA_BLOCK // PAGE_SIZE
    logical = sel[..., None] * per_blk + jnp.arange(per_blk)
    pages = page_table[jnp.arange(nb)[:, None, None, None], logical]
    return sample_moba_attn_call(q3, k_new, v_new, cache_k, cache_v, layer, pages.reshape(-1), sel.reshape(-1),
                                 prev_row, far, own_bias, own)


def dsa_sample_attention(zs, tail_s, k_new, v_new, cache_k, cache_v, cache_kidx, layer, page_table, table_t):
    nb, n_pages = page_table.shape
    past = n_pages * PAGE_SIZE
    n_sel = min(DSA_TOPK, (past + 1) // 4)
    qi_pad = jnp.pad(zs[:, IN_DIM_A:].reshape(nb, IDX_HEADS, IDX_DIM), ((0, 0), (0, 0), (0, LANES - IDX_DIM)))
    wi_col = tail_s[:, IDX_DIM:IDX_DIM + IDX_HEADS].reshape(nb, IDX_HEADS, 1)
    ki_pad = jnp.pad(tail_s[:, :IDX_DIM], ((0, 0), (0, LANES - IDX_DIM))).reshape(nb, 1, LANES)
    scores = sample_dsa_score_call(cache_kidx, layer, page_table.reshape(-1), qi_pad, wi_col, n_pages)
    picked, new_sel = sample_dsa_select_call(scores, qi_pad, wi_col, ki_pad, n_sel)
    hit = (picked > 0.5).astype(I32)
    in_page = jnp.cumsum(hit, axis=2)
    page_tot = in_page[:, :, -1]
    running = (in_page + (jnp.cumsum(page_tot, axis=1) - page_tot)[:, :, None]).reshape(nb, past)
    slots = jnp.arange(1, n_sel + 1, dtype=I32)
    idx = jax.vmap(lambda c: jnp.searchsorted(c, slots, side='left'))(running).astype(I32)
    pos = jnp.minimum(idx, past - 1)
    rows = page_table[jnp.arange(nb)[:, None], pos // PAGE_SIZE] * PAGE_SIZE + pos % PAGE_SIZE
    valid = (idx < past).astype(F32)
    bucket = _rel_bucket(past - pos).astype(I32)
    q8 = jnp.pad(zs[:, :Q_DIM].reshape(nb, N_KV_HEADS, GQA_GROUP, HEAD_DIM),
                 ((0, 0), (0, 0), (0, 8 - GQA_GROUP), (0, 0)))
    o8 = sample_dsa_attn_call(rows.reshape(-1), q8, valid[:, None], bucket[:, None], table_t,
                              k_new, v_new, new_sel, cache_k, cache_v, layer, n_sel)
    return o8[:, :, :GQA_GROUP].reshape(nb, Q_DIM)


def kernel(x_prompt, x_sample, cache_k_a, cache_v_a, cache_k_b, cache_v_b, cache_kidx_b, page_table, c_prompt, c_sample, rel_table, ada_w, ada_b, norm_attn, norm_ffn, q_norm, k_norm, w_in_a, w_in_b, w_o, w_grp, b_grp, w_rt, b_rt, w_gate, w_up, w_down):
    b, s, d = x_prompt.shape
    nb = x_sample.shape[0]
    depth = ada_w.shape[0]
    n_pages = page_table.shape[1]
    past = n_pages * PAGE_SIZE
    own = past // MOBA_BLOCK
    assert x_sample.shape[1] == 1 and past % MOBA_BLOCK == 0 and MOBA_TOPK <= own <= LANES
    assert s % ROW_TILE == 0 and nb % 8 == 0

    n_mod_rows = -(-(b + nb) // 8) * 8
    c_all = jnp.concatenate([c_prompt, c_sample, jnp.zeros((n_mod_rows - b - nb, d), F32)])
    mod = adaln_call(c_all, ada_w, ada_b)

    bias = bias_tiles_call(rel_table)
    lane_rep = lambda v: jnp.broadcast_to(v[:, None, None], (N_HEADS, 1, LANES))
    far = lane_rep(rel_table[REL_BUCKETS - 1])
    far2 = far * LOG2E
    own_bias = lane_rep(rel_table[0])
    prev_row = rel_table[_rel_bucket(MOBA_BLOCK - jnp.arange(MOBA_BLOCK))].T[:, None, :]
    table_t = jnp.pad(rel_table.T, ((0, 0), (0, LANES - REL_BUCKETS)))

    xp = x_prompt.reshape(b * s, d)
    xs = x_sample.reshape(nb, d)
    kv_p = {"ka": [], "va": [], "kb": [], "vb": [], "ib": []}
    kv_s = {"ka": [], "va": [], "kb": [], "vb": [], "ib": []}

    for i in range(depth):
        j = i // 2
        mp = mod[i, :b].reshape(b, 1, 6, d)
        ms = mod[i, b:b + nb].reshape(1, nb, 6, d)
        sh1, sc1, gt1, sh2, sc2, gt2 = (mp[:, :, k] for k in range(6))
        sh1s, sc1s, gt1s, sh2s, sc2s, gt2s = (ms[:, :, k] for k in range(6))
        moba = i % 2 == 0
        w_in = w_in_a if moba else w_in_b
        z, k_p, v_p, tail = _project(xp, norm_attn[i], sc1, sh1, w_in, j, q_norm[i], k_norm[i],
                                     tm=ROW_TILE, rows_per_mod=s)
        zs, k_s, v_s, tail_s = _project(xs, norm_attn[i], sc1s, sh1s, w_in, j, q_norm[i], k_norm[i],
                                        tm=nb, rows_per_mod=nb)
        z3 = z.reshape(b, s, -1)
        tag = "a" if moba else "b"
        kv_p["k" + tag].append(k_p.reshape(b, s, N_KV_HEADS, HEAD_DIM))
        kv_p["v" + tag].append(v_p.reshape(b, s, N_KV_HEADS, HEAD_DIM))
        kv_s["k" + tag].append(k_s.reshape(nb, 1, N_KV_HEADS, HEAD_DIM))
        kv_s["v" + tag].append(v_s.reshape(nb, 1, N_KV_HEADS, HEAD_DIM))

        if moba:
            sel = moba_select_call(z3, s // MOBA_BLOCK)
            o = attn_call(z3, sel, jnp.zeros((b, 1, s), I32), bias, far2, dsa=False)

            o_s = moba_sample_attention(zs, k_s, v_s, cache_k_a, cache_v_a, j, page_table, prev_row, far, own_bias)
        else:
            tail3 = tail.reshape(b, s, LANES)
            kv_p["ib"].append(tail3[:, :, :IDX_DIM])
            kv_s["ib"].append(tail_s[:, None, :IDX_DIM])
            wi_t = jnp.swapaxes(tail3[:, :, IDX_DIM:IDX_DIM + IDX_HEADS], 1, 2) * (IDX_SCALE * IDX_W_SCALE)
            key_t, thr = dsa_score_call(z3, tail3, wi_t, min(DSA_TOPK, s // 4))
            o = attn_call(z3, key_t, thr, bias, far2, dsa=True)

            o_s = dsa_sample_attention(zs, tail_s, k_s, v_s, cache_k_b, cache_v_b, cache_kidx_b, j, page_table,
                                       table_t)

        xp = linear_residual_call(o.reshape(b * s, Q_DIM), w_o, i, xp, gt1, tm=ROW_TILE, rows_per_mod=s)
        xs = linear_residual_call(o_s.astype(BF16), w_o, i, xs, gt1s, tm=nb, rows_per_mod=nb)
        xp, xs = moe_layer([xp, xs], [(sc2, sh2, gt2), (sc2s, sh2s, gt2s)], [s, nb], norm_ffn[i],
                           w_grp[i], b_grp[i], w_rt[i], b_rt[i], w_gate, w_up, w_down, i, tms=[512, nb])

    stack = lambda lst: jnp.stack(lst)
    return (xp.reshape(b, s, d), xs.reshape(nb, 1, d),
            stack(kv_p["ka"]), stack(kv_p["va"]), stack(kv_p["kb"]), stack(kv_p["vb"]), stack(kv_p["ib"]),
            stack(kv_s["ka"]), stack(kv_s["va"]), stack(kv_s["kb"]), stack(kv_s["vb"]), stack(kv_s["ib"]))
```

```python
import functools
import math

import numpy as np
import jax
import jax.numpy as jnp
from jax import lax
from jax.experimental import pallas as pl
from jax.experimental.pallas import tpu as pltpu

F32 = jnp.float32
BF16 = jnp.bfloat16
I32 = jnp.int32

D_MODEL = 2048
N_HEADS = 16
HEAD_DIM = 128
N_KV_HEADS = 4
GQA_GROUP = N_HEADS // N_KV_HEADS
Q_DIM = N_HEADS * HEAD_DIM
KV_DIM = N_KV_HEADS * HEAD_DIM
ATTN_SCALE = HEAD_DIM ** -0.5
MOBA_BLOCK = 256
MOBA_TOPK = 3
IDX_HEADS = 16
IDX_DIM = 64
IDX_SCALE = IDX_DIM ** -0.5
IDX_W_SCALE = IDX_HEADS ** -0.5
DSA_TOPK = 256
REL_BUCKETS = 32
REL_MAX_DIST = 128
N_GROUPS = 4
EXPERTS_PER_GROUP = 8
N_EXPERTS = N_GROUPS * EXPERTS_PER_GROUP
D_EXPERT = 768
PAGE_SIZE = 128
IN_DIM_A = Q_DIM + 2 * KV_DIM
IDX_Q_DIM = IDX_HEADS * IDX_DIM
RMS_EPS = 1e-6
NEG = -1e30
LOG2E = math.log2(math.e)

LANES = 128
ATT_TILE = 256
ROW_TILE = 1024
COL_TILE = 512
MOE_BLOCK = 256
FF_CHUNK = 256
VMEM_LIMIT = 56 * 1024 * 1024


def _params(sem):
    return pltpu.CompilerParams(dimension_semantics=sem, vmem_limit_bytes=VMEM_LIMIT)


def _adaln_body(c_ref, w_ref, b_ref, o_ref):
    c = c_ref[...]
    a = c * (1.0 / (1.0 + jnp.exp(-c)))
    o_ref[0] = jnp.dot(a, w_ref[0], preferred_element_type=F32) + b_ref[0]


def adaln_call(c_all, ada_w, ada_b):
    n_layers, d, n = ada_w.shape
    r = c_all.shape[0]
    tn = 1024
    return pl.pallas_call(
        _adaln_body,
        grid=(n_layers, n // tn),
        in_specs=[pl.BlockSpec((r, d), lambda l, j: (0, 0)),
                  pl.BlockSpec((1, d, tn), lambda l, j: (l, 0, j)),
                  pl.BlockSpec((1, 1, tn), lambda l, j: (l, 0, j))],
        out_specs=pl.BlockSpec((1, r, tn), lambda l, j: (l, 0, j)),
        out_shape=jax.ShapeDtypeStruct((n_layers, r, n), F32),
        compiler_params=_params(("arbitrary", "arbitrary")),
        name="adaln",
    )(c_all, ada_w, ada_b.reshape(n_layers, 1, n))


def _modulated_norm(x, g, sc, sh):
    ms = jnp.mean(x * x, axis=-1, keepdims=True)
    return (x * lax.rsqrt(ms + RMS_EPS) * g) * (1.0 + sc) + sh


def _norm_proj_body(x_ref, g_ref, sc_ref, sh_ref, w_ref, gain_ref, o_ref, *rest, n_norm_tiles, tn, kv_out):
    h_scr = rest[-1]
    j = pl.program_id(1)

    @pl.when(j == 0)
    def _():
        h_scr[...] = _modulated_norm(x_ref[...], g_ref[...], sc_ref[0], sh_ref[0]).astype(BF16)

    z = jnp.dot(h_scr[...], w_ref[0].astype(BF16), preferred_element_type=F32)

    @pl.when(j < n_norm_tiles)
    def _():
        for k in range(tn // HEAD_DIM):
            sl = slice(k * HEAD_DIM, (k + 1) * HEAD_DIM)
            zk = z[:, sl]
            ms = jnp.mean(zk * zk, axis=-1, keepdims=True)
            o_ref[:, sl] = zk * lax.rsqrt(ms + RMS_EPS) * gain_ref[:, sl]

    @pl.when(j >= n_norm_tiles)
    def _():
        o_ref[...] = z

    if kv_out:
        for ref, tile in ((rest[0], Q_DIM // tn), (rest[1], Q_DIM // tn + 1)):
            @pl.when(j == tile)
            def _(ref=ref):
                for kv in range(N_KV_HEADS):
                    ref[:, kv, :] = o_ref[:, kv * HEAD_DIM:(kv + 1) * HEAD_DIM]


def norm_proj_call(x2d, g, sc, sh, w, layer, gain, *, n_cols, n_norm_cols, tm, rows_per_mod, tn=COL_TILE,
                   kv_out=False):
    m, d = x2d.shape
    assert m % tm == 0 and n_cols % tn == 0 and n_norm_cols % tn == 0
    assert not kv_out or tn == KV_DIM
    r = sc.shape[1]
    tiles_per_mod = rows_per_mod // tm
    mod_spec = pl.BlockSpec((1, r, d), lambda i, j: (i // tiles_per_mod, 0, 0))
    out_specs = [pl.BlockSpec((tm, tn), lambda i, j: (i, j))]
    out_shape = [jax.ShapeDtypeStruct((m, n_cols), F32)]
    if kv_out:
        out_specs += [pl.BlockSpec((tm, N_KV_HEADS, HEAD_DIM), lambda i, j: (i, 0, 0))] * 2
        out_shape += [jax.ShapeDtypeStruct((m, N_KV_HEADS, HEAD_DIM), F32)] * 2
    return pl.pallas_call(
        functools.partial(_norm_proj_body, n_norm_tiles=n_norm_cols // tn, tn=tn, kv_out=kv_out),
        grid=(m // tm, n_cols // tn),
        in_specs=[pl.BlockSpec((tm, d), lambda i, j: (i, 0)),
                  pl.BlockSpec((1, d), lambda i, j: (0, 0)),
                  mod_spec, mod_spec,
                  pl.BlockSpec((1, d, tn), lambda i, j: (layer, 0, j)),
                  pl.BlockSpec((1, tn), lambda i, j: (0, j))],
        out_specs=out_specs,
        out_shape=out_shape,
        scratch_shapes=[pltpu.VMEM((tm, d), BF16)],
        compiler_params=_params(("arbitrary", "arbitrary")),
        name="norm_proj",
    )(x2d, g.reshape(1, d), sc, sh, w, gain)


def _linear_residual_body(a_ref, w_ref, x_ref, gt_ref, o_ref):
    y = jnp.dot(a_ref[...], w_ref[0].astype(BF16), preferred_element_type=F32)
    o_ref[...] = x_ref[...] + gt_ref[0] * y


def linear_residual_call(a, w, layer, x2d, gt, *, tm, rows_per_mod, tn=COL_TILE):
    m, k = a.shape
    n = w.shape[2]
    r = gt.shape[1]
    tiles_per_mod = rows_per_mod // tm
    return pl.pallas_call(
        _linear_residual_body,
        grid=(m // tm, n // tn),
        in_specs=[pl.BlockSpec((tm, k), lambda i, j: (i, 0)),
                  pl.BlockSpec((1, k, tn), lambda i, j: (layer, 0, j)),
                  pl.BlockSpec((tm, tn), lambda i, j: (i, j)),
                  pl.BlockSpec((1, r, tn), lambda i, j: (i // tiles_per_mod, 0, j))],
        out_specs=pl.BlockSpec((tm, tn), lambda i, j: (i, j)),
        out_shape=jax.ShapeDtypeStruct((m, n), F32),
        compiler_params=_params(("arbitrary", "arbitrary")),
        name="linear_residual",
    )(a, w, x2d, gt)


def _rel_bucket_np(dist):
    n = np.maximum(dist, 0)
    exact = REL_BUCKETS // 2
    logr = (np.log(np.maximum(n, 1).astype(np.float32) / np.float32(exact))
            / np.float32(math.log(REL_MAX_DIST / exact))).astype(np.float32)
    large = np.minimum(exact + (logr * np.float32(REL_BUCKETS - exact)).astype(np.int32), REL_BUCKETS - 1)
    return np.where(n < exact, n, large).astype(np.int32)


def _bias_body(bk_ref, tab_ref, o_ref):
    h = pl.program_id(0)
    bk = bk_ref[...]
    acc = jnp.zeros(bk.shape, F32)
    for b in range(REL_BUCKETS):
        acc = jnp.where(bk == b, tab_ref[b, h], acc)
    o_ref[0] = jnp.where(bk < 0, NEG, acc * LOG2E)


def bias_tiles_call(rel_table):
    t = np.arange(ATT_TILE)
    d = t[:, None] - t[None, :]
    buckets = np.stack([np.where(d >= 0, _rel_bucket_np(d), -1), _rel_bucket_np(d + ATT_TILE)]).astype(np.int32)
    return pl.pallas_call(
        _bias_body,
        grid=(N_HEADS,),
        in_specs=[pl.BlockSpec((2, ATT_TILE, ATT_TILE), lambda h: (0, 0, 0)),
                  pl.BlockSpec(memory_space=pltpu.SMEM)],
        out_specs=pl.BlockSpec((1, 2, ATT_TILE, ATT_TILE), lambda h: (h, 0, 0, 0)),
        out_shape=jax.ShapeDtypeStruct((N_HEADS, 2, ATT_TILE, ATT_TILE), F32),
        compiler_params=_params(("arbitrary",)),
        name="bias_tiles",
    )(jnp.asarray(buckets), rel_table)


def _moba_select_body(q_ref, k_ref, o_ref, kbar_scr, code_scr):
    i = pl.program_id(1)
    nblk = kbar_scr.shape[0]

    @pl.when(i == 0)
    def _():
        k = k_ref[0]
        kbar_scr[...] = jnp.mean(k.reshape(nblk, MOBA_BLOCK, KV_DIM), axis=1)
        code_scr[...] = jnp.zeros(code_scr.shape, F32)

    blk = lax.broadcasted_iota(I32, (nblk, ATT_TILE), 0)
    for h in range(N_HEADS):
        kv = h // GQA_GROUP
        gate = lax.dot_general(kbar_scr[:, kv * HEAD_DIM:(kv + 1) * HEAD_DIM].astype(BF16),
                               q_ref[0, :, h * HEAD_DIM:(h + 1) * HEAD_DIM].astype(BF16),
                               (((1,), (1,)), ((), ())), preferred_element_type=F32)
        gate = jnp.where(blk < i, gate, NEG)
        code = jnp.zeros((1, ATT_TILE), F32)
        for _ in range(MOBA_TOPK):
            mx = jnp.max(gate, axis=0, keepdims=True)
            idx = jnp.min(jnp.where(gate == mx, blk, nblk), axis=0, keepdims=True)
            code = code + jnp.where(idx < i, jnp.exp2(idx.astype(F32)), 0.0)
            gate = jnp.where(blk == idx, -jnp.inf, gate)
        code_scr[h:h + 1, :] = code
    o_ref[0] = code_scr[...].T


def moba_select_call(z3, n_blocks):
    b, s, _ = z3.shape
    return pl.pallas_call(
        _moba_select_body,
        grid=(b, s // ATT_TILE),
        in_specs=[pl.BlockSpec((1, ATT_TILE, Q_DIM), lambda bb, i: (bb, i, 0)),
                  pl.BlockSpec((1, s, KV_DIM), lambda bb, i: (bb, 0, Q_DIM // KV_DIM))],
        out_specs=pl.BlockSpec((1, ATT_TILE, LANES), lambda bb, i: (bb, i, 0)),
        out_shape=jax.ShapeDtypeStruct((b, s, LANES), F32),
        scratch_shapes=[pltpu.VMEM((n_blocks, KV_DIM), F32), pltpu.VMEM((LANES, ATT_TILE), F32)],
        compiler_params=_params(("arbitrary", "arbitrary")),
        name="moba_select",
    )(z3, z3)


def _flash_update(kv, q4, kb, vb, bias4, mask3, m_scr, l_scr, acc_scr):
    s = lax.dot_general(q4, kb, (((1,), (1,)), ((), ())), preferred_element_type=F32)
    s = s.reshape(GQA_GROUP, ATT_TILE, ATT_TILE) * (ATTN_SCALE * LOG2E) + bias4
    if mask3 is not None:
        s = jnp.where(mask3, s, NEG)
    m_prev = m_scr[kv]
    m_new = jnp.maximum(m_prev, jnp.max(s, axis=-1, keepdims=True))
    alpha = jnp.exp2(m_prev - m_new)
    p = jnp.exp2(s - jnp.concatenate([m_new] * (ATT_TILE // LANES), axis=-1))
    pv = jnp.dot(p.reshape(GQA_GROUP * ATT_TILE, ATT_TILE).astype(BF16), vb, preferred_element_type=F32)
    pv = pv.reshape(GQA_GROUP, ATT_TILE, HEAD_DIM + LANES)
    acc_scr[kv] = alpha * acc_scr[kv] + pv[:, :, :HEAD_DIM]
    l_scr[kv] = alpha * l_scr[kv] + pv[:, :, HEAD_DIM:]
    m_scr[kv] = m_new


def _attn_body(q_ref, k_ref, v_ref, sel_ref, thr_ref, bias_ref, far_ref, o_ref, m_scr, l_scr, acc_scr, code_scr,
               q_scr, *, dsa):
    i = pl.program_id(1)
    j = pl.program_id(2)

    @pl.when(j == 0)
    def _():
        m_scr[...] = jnp.full(m_scr.shape, NEG, F32)
        l_scr[...] = jnp.zeros(l_scr.shape, F32)
        acc_scr[...] = jnp.zeros(acc_scr.shape, F32)
        for h in range(N_HEADS):
            rows = pl.ds((h % GQA_GROUP) * ATT_TILE, ATT_TILE)
            q_scr[h // GQA_GROUP, rows, :] = q_ref[0, :, h * HEAD_DIM:(h + 1) * HEAD_DIM].astype(BF16)
        if not dsa:
            code = sel_ref[0].astype(I32)
            for h in range(N_HEADS):
                code_scr[h] = jnp.broadcast_to(code[:, h:h + 1], (ATT_TILE, LANES))

    def step(which):
        if dsa:
            mask_all = (jnp.where(sel_ref[0] >= thr_ref[0], 1.0, 0.0).T > 0.5)[None]
        ones = jnp.ones((ATT_TILE, LANES), BF16)
        for kv in range(N_KV_HEADS):
            heads = range(kv * GQA_GROUP, (kv + 1) * GQA_GROUP)
            q4 = q_scr[kv]
            kb = k_ref[0, :, kv * HEAD_DIM:(kv + 1) * HEAD_DIM].astype(BF16)
            vb = jnp.concatenate([v_ref[0, :, kv * HEAD_DIM:(kv + 1) * HEAD_DIM].astype(BF16), ones], axis=1)
            if which == 2:
                bias4 = far_ref[kv * GQA_GROUP:(kv + 1) * GQA_GROUP, :, 0:1]
            else:
                bias4 = bias_ref[kv * GQA_GROUP:(kv + 1) * GQA_GROUP, which]
            if dsa:
                mask3 = mask_all
            elif which == 0:
                mask3 = None
            else:
                bits = jnp.stack([jnp.right_shift(code_scr[h], j) & 1 for h in heads])
                mask3 = jnp.concatenate([bits] * (ATT_TILE // LANES), axis=-1) > 0
            _flash_update(kv, q4, kb, vb, bias4, mask3, m_scr, l_scr, acc_scr)

    @pl.when(j == i)
    def _():
        step(0)

    @pl.when(j == i - 1)
    def _():
        step(1)

    @pl.when(j < i - 1)
    def _():
        step(2)

    @pl.when(j == i)
    def _():
        for kv in range(N_KV_HEADS):
            out = acc_scr[kv] / l_scr[kv]
            for g in range(GQA_GROUP):
                h = kv * GQA_GROUP + g
                o_ref[0, :, h * HEAD_DIM:(h + 1) * HEAD_DIM] = out[g].astype(o_ref.dtype)


def attn_call(z3, sel, thr, bias, far, *, dsa):
    b, s, _ = z3.shape
    nt = s // ATT_TILE
    if dsa:
        sel_spec = pl.BlockSpec((1, ATT_TILE, ATT_TILE), lambda bb, i, j: (bb, jnp.minimum(j, i), i))
    else:
        sel_spec = pl.BlockSpec((1, ATT_TILE, LANES), lambda bb, i, j: (bb, i, 0))
    kcol, vcol = Q_DIM // KV_DIM, Q_DIM // KV_DIM + 1
    return pl.pallas_call(
        functools.partial(_attn_body, dsa=dsa),
        grid=(b, nt, nt),
        in_specs=[pl.BlockSpec((1, ATT_TILE, Q_DIM), lambda bb, i, j: (bb, i, 0)),
                  pl.BlockSpec((1, ATT_TILE, KV_DIM), lambda bb, i, j: (bb, jnp.minimum(j, i), kcol)),
                  pl.BlockSpec((1, ATT_TILE, KV_DIM), lambda bb, i, j: (bb, jnp.minimum(j, i), vcol)),
                  sel_spec,
                  pl.BlockSpec((1, 1, ATT_TILE), lambda bb, i, j: (bb, 0, i)),
                  pl.BlockSpec((N_HEADS, 2, ATT_TILE, ATT_TILE), lambda bb, i, j: (0, 0, 0, 0)),
                  pl.BlockSpec((N_HEADS, 1, LANES), lambda bb, i, j: (0, 0, 0))],
        out_specs=pl.BlockSpec((1, ATT_TILE, Q_DIM), lambda bb, i, j: (bb, i, 0)),
        out_shape=jax.ShapeDtypeStruct((b, s, Q_DIM), BF16),
        scratch_shapes=[pltpu.VMEM((N_KV_HEADS, GQA_GROUP, ATT_TILE, LANES), F32),
                        pltpu.VMEM((N_KV_HEADS, GQA_GROUP, ATT_TILE, LANES), F32),
                        pltpu.VMEM((N_KV_HEADS, GQA_GROUP, ATT_TILE, HEAD_DIM), F32),
                        pltpu.VMEM((1 if dsa else N_HEADS, ATT_TILE, LANES), I32),
                        pltpu.VMEM((N_KV_HEADS, GQA_GROUP * ATT_TILE, HEAD_DIM), BF16)],
        compiler_params=_params(("arbitrary", "arbitrary", "arbitrary")),
        name="dsa_attn" if dsa else "moba_attn",
    )(z3, z3, z3, sel, thr, bias, far)


def _sortable_key(x):
    bits = pltpu.bitcast(x, I32)
    return bits ^ (jnp.right_shift(bits, 31) & 0x7FFFFFFF)


def _kth_largest_key(count_ge, k, shape):
    def bit_body(bi, t):
        cand = t ^ jnp.left_shift(jnp.int32(1), 31 - bi)
        return jnp.where(count_ge(cand) >= k, cand, t)
    return lax.fori_loop(0, 32, bit_body, jnp.full(shape, -2 ** 31, I32))


def _dsa_score_body(qi_ref, tail_ref, wi_ref, key_ref, thr_ref, key_scr, *, n_sel):
    i = pl.program_id(1)
    j = pl.program_id(2)

    @pl.when(j <= i)
    def _():
        t = tail_ref[0]
        lane = lax.broadcasted_iota(I32, t.shape, 1)
        klo = jnp.where(lane < IDX_DIM, t, 0.0)
        khi = pltpu.roll(klo, IDX_DIM, 1)
        qstk = jnp.concatenate([qi_ref[0, :, p * LANES:(p + 1) * LANES] for p in range(IDX_HEADS // 2)],
                               axis=0).astype(BF16)
        dn = (((1,), (1,)), ((), ()))
        s_lo = lax.dot_general(klo.astype(BF16), qstk, dn, preferred_element_type=F32)
        s_hi = lax.dot_general(khi.astype(BF16), qstk, dn, preferred_element_type=F32)
        sc = jnp.zeros((ATT_TILE, ATT_TILE), F32)
        for p in range(IDX_HEADS // 2):
            sl = slice(p * ATT_TILE, (p + 1) * ATT_TILE)
            sc = sc + jnp.maximum(s_lo[:, sl], 0.0) * wi_ref[0, 2 * p:2 * p + 1, :]
            sc = sc + jnp.maximum(s_hi[:, sl], 0.0) * wi_ref[0, 2 * p + 1:2 * p + 2, :]
        kpos = j * ATT_TILE + lax.broadcasted_iota(I32, sc.shape, 0)
        qpos = i * ATT_TILE + lax.broadcasted_iota(I32, sc.shape, 1)
        key = _sortable_key(jnp.where(kpos <= qpos, sc, NEG))
        key_ref[0] = key
        key_scr[pl.ds(pl.multiple_of(j * ATT_TILE, ATT_TILE), ATT_TILE), :] = key

    @pl.when(j > i)
    def _():
        key_ref[0] = jnp.full((ATT_TILE, ATT_TILE), -2 ** 31, I32)

    @pl.when(j == i)
    def _():
        def count_ge(cand):
            def chunk(c, cnt):
                kc = key_scr[pl.ds(pl.multiple_of(c * ATT_TILE, ATT_TILE), ATT_TILE), :]
                hit = jnp.where(kc >= cand, 1, 0).astype(I32)
                return cnt + jnp.sum(hit.reshape(ATT_TILE // 8, 8, ATT_TILE), axis=0)
            cnt8 = lax.fori_loop(0, i + 1, chunk, jnp.zeros((8, ATT_TILE), I32))
            return jnp.sum(cnt8, axis=0, keepdims=True)
        thr_ref[0] = _kth_largest_key(count_ge, n_sel, (1, ATT_TILE))


def dsa_score_call(z3, tail3, wi_t, n_sel):
    b, s, _ = z3.shape
    nt = s // ATT_TILE
    return pl.pallas_call(
        functools.partial(_dsa_score_body, n_sel=n_sel),
        grid=(b, nt, nt),
        in_specs=[pl.BlockSpec((1, ATT_TILE, IDX_Q_DIM), lambda bb, i, j: (bb, i, IN_DIM_A // IDX_Q_DIM)),
                  pl.BlockSpec((1, ATT_TILE, LANES), lambda bb, i, j: (bb, jnp.minimum(j, i), 0)),
                  pl.BlockSpec((1, IDX_HEADS, ATT_TILE), lambda bb, i, j: (bb, 0, i))],
        out_specs=[pl.BlockSpec((1, ATT_TILE, ATT_TILE), lambda bb, i, j: (bb, j, i)),
                   pl.BlockSpec((1, 1, ATT_TILE), lambda bb, i, j: (bb, 0, i))],
        out_shape=[jax.ShapeDtypeStruct((b, s, s), I32), jax.ShapeDtypeStruct((b, 1, s), I32)],
        scratch_shapes=[pltpu.VMEM((s, ATT_TILE), I32)],
        compiler_params=_params(("arbitrary", "arbitrary", "arbitrary")),
        name="dsa_score",
    )(z3, tail3, wi_t)


def _router_body(x_ref, g_ref, sc_ref, sh_ref, wr_ref, br_ref, h_ref, r_ref):
    h = _modulated_norm(x_ref[...], g_ref[...], sc_ref[0], sh_ref[0])
    h_ref[...] = h.astype(h_ref.dtype)
    lg = jnp.dot(h.astype(BF16), wr_ref[...].astype(BF16), preferred_element_type=F32) + br_ref[...]
    lane = lax.broadcasted_iota(I32, lg.shape, 1)
    gl = jnp.where(lane < N_GROUPS, lg, -jnp.inf)
    gmax = jnp.max(gl, axis=-1, keepdims=True)
    g_sel = jnp.min(jnp.where(gl == gmax, lane, LANES), axis=-1, keepdims=True)
    g_w = 1.0 / jnp.sum(jnp.exp(gl - gmax), axis=-1, keepdims=True)
    lo = N_GROUPS + EXPERTS_PER_GROUP * g_sel
    el = jnp.where((lane >= lo) & (lane < lo + EXPERTS_PER_GROUP), lg, -jnp.inf)
    e1 = jnp.max(el, axis=-1, keepdims=True)
    i1 = jnp.min(jnp.where(el == e1, lane, LANES), axis=-1, keepdims=True)
    el2 = jnp.where(lane == i1, -jnp.inf, el)
    e2 = jnp.max(el2, axis=-1, keepdims=True)
    i2 = jnp.min(jnp.where(el2 == e2, lane, LANES), axis=-1, keepdims=True)
    v2 = jnp.exp(e2 - e1)
    w1 = g_w / (1.0 + v2)
    w2 = g_w * v2 / (1.0 + v2)
    r_ref[...] = jnp.where(lane == 0, (i1 - N_GROUPS).astype(F32),
                           jnp.where(lane == 1, (i2 - N_GROUPS).astype(F32),
                                     jnp.where(lane == 2, w1, jnp.where(lane == 3, w2, 0.0))))


def router_call(x2d, g, sc, sh, w_route, b_route, *, tm, rows_per_mod):
    m, d = x2d.shape
    r = sc.shape[1]
    tiles_per_mod = rows_per_mod // tm
    mod_spec = pl.BlockSpec((1, r, d), lambda i: (i // tiles_per_mod, 0, 0))
    return pl.pallas_call(
        _router_body,
        grid=(m // tm,),
        in_specs=[pl.BlockSpec((tm, d), lambda i: (i, 0)),
                  pl.BlockSpec((1, d), lambda i: (0, 0)),
                  mod_spec, mod_spec,
                  pl.BlockSpec((d, LANES), lambda i: (0, 0)),
                  pl.BlockSpec((1, LANES), lambda i: (0, 0))],
        out_specs=[pl.BlockSpec((tm, d), lambda i: (i, 0)), pl.BlockSpec((tm, LANES), lambda i: (i, 0))],
        out_shape=[jax.ShapeDtypeStruct((m, d), F32), jax.ShapeDtypeStruct((m, LANES), F32)],
        compiler_params=_params(("arbitrary",)),
        name="router",
    )(x2d, g.reshape(1, d), sc, sh, w_route, b_route)


def _row_dma_wait(ref_hbm, dst, n_rows, sem):
    pltpu.make_async_copy(ref_hbm.at[pl.ds(0, n_rows)], dst, sem).wait()


def _ffn_body(be_ref, nu_ref, cur_ref, nxt_ref, h_ref, wg_ref, wu_ref, wd_ref, o_ref, xbuf, sems):
    i = pl.program_id(0)
    nblk = pl.num_programs(0)
    nu = nu_ref[0]
    slot = i % 2

    def gather(src_ref, sl):
        for r in range(MOE_BLOCK):
            pltpu.make_async_copy(h_ref.at[pl.ds(src_ref[0, 0, r], 1)], xbuf.at[sl, pl.ds(r, 1)],
                                  sems.at[sl]).start()

    def gather_wait(sl):
        _row_dma_wait(h_ref, xbuf.at[sl], MOE_BLOCK, sems.at[sl])

    @pl.when(i == 0)
    def _():
        gather(cur_ref, 0)

    @pl.when(i < nu)
    def _():
        gather(nxt_ref, 1 - slot)
        gather_wait(slot)
        x = xbuf[slot].astype(BF16)
        acc = jnp.zeros(o_ref.shape, F32)
        for c in range(D_EXPERT // FF_CHUNK):
            sl = slice(c * FF_CHUNK, (c + 1) * FF_CHUNK)
            a = jnp.dot(x, wg_ref[0, 0, :, sl].astype(BF16), preferred_element_type=F32)
            u = jnp.dot(x, wu_ref[0, 0, :, sl].astype(BF16), preferred_element_type=F32)
            act = a * (1.0 / (1.0 + jnp.exp(-a))) * u
            acc = acc + jnp.dot(act.astype(BF16), wd_ref[0, 0, sl, :].astype(BF16), preferred_element_type=F32)
        o_ref[...] = acc

    @pl.when(i >= nu)
    def _():
        o_ref[...] = jnp.zeros(o_ref.shape, F32)

    @pl.when(i == nu)
    def _():
        gather_wait(slot)

    @pl.when((i == nblk - 1) & (nu == nblk))
    def _():
        gather_wait(1 - slot)


def ffn_call(blk_e, n_used, src_tok, h_all, w_gate, w_up, w_down, layer):
    d = h_all.shape[1]
    nblk = blk_e.shape[0]
    last = lambda i, nu: jnp.minimum(i, nu[0] - 1)
    src3 = src_tok.reshape(nblk, 1, MOE_BLOCK)
    w_in_spec = pl.BlockSpec((1, 1, d, D_EXPERT), lambda i, be, nu: (layer, be[last(i, nu)], 0, 0))
    grid_spec = pltpu.PrefetchScalarGridSpec(
        num_scalar_prefetch=2,
        grid=(nblk,),
        in_specs=[pl.BlockSpec((1, 1, MOE_BLOCK), lambda i, be, nu: (i, 0, 0), memory_space=pltpu.SMEM),
                  pl.BlockSpec((1, 1, MOE_BLOCK), lambda i, be, nu: (jnp.minimum(i + 1, nblk - 1), 0, 0),
                               memory_space=pltpu.SMEM),
                  pl.BlockSpec(memory_space=pl.ANY),
                  w_in_spec, w_in_spec,
                  pl.BlockSpec((1, 1, D_EXPERT, d), lambda i, be, nu: (layer, be[last(i, nu)], 0, 0))],
        out_specs=pl.BlockSpec((MOE_BLOCK, d), lambda i, be, nu: (i, 0)),
        scratch_shapes=[pltpu.VMEM((2, MOE_BLOCK, d), F32), pltpu.SemaphoreType.DMA((2,))],
    )
    return pl.pallas_call(
        _ffn_body,
        grid_spec=grid_spec,
        out_shape=jax.ShapeDtypeStruct((nblk * MOE_BLOCK, d), F32),
        compiler_params=_params(("arbitrary",)),
        name="expert_ffn",
    )(blk_e, n_used, src3, src3, h_all, w_gate, w_up, w_down)


def _combine_body(dcur_ref, dnxt_ref, x_ref, rt_ref, gt_ref, ob_ref, o_ref, rbuf, sems, *, tm):
    t = pl.program_id(0)
    slot = t % 2

    def issue(dref, sl):
        for r in range(2 * tm):
            dst = (r % 2) * tm + r // 2
            pltpu.make_async_copy(ob_ref.at[pl.ds(dref[0, 0, r], 1)], rbuf.at[sl, pl.ds(dst, 1)],
                                  sems.at[sl]).start()

    @pl.when(t == 0)
    def _():
        issue(dcur_ref, 0)

    @pl.when(t + 1 < pl.num_programs(0))
    def _():
        issue(dnxt_ref, 1 - slot)

    _row_dma_wait(ob_ref, rbuf.at[slot], 2 * tm, sems.at[slot])
    w0 = rt_ref[:, 2:3]
    w1 = rt_ref[:, 3:4]
    o_ref[...] = x_ref[...] + gt_ref[0] * (w0 * rbuf[slot, 0:tm] + w1 * rbuf[slot, tm:2 * tm])


def combine_call(x2d, out_buf, dest, route, gt, *, tm, rows_per_mod):
    m, d = x2d.shape
    r = gt.shape[1]
    nt = m // tm
    tiles_per_mod = rows_per_mod // tm
    dest3 = dest.reshape(nt, 1, 2 * tm)
    return pl.pallas_call(
        functools.partial(_combine_body, tm=tm),
        grid=(nt,),
        in_specs=[pl.BlockSpec((1, 1, 2 * tm), lambda t: (t, 0, 0), memory_space=pltpu.SMEM),
                  pl.BlockSpec((1, 1, 2 * tm), lambda t: (jnp.minimum(t + 1, nt - 1), 0, 0),
                               memory_space=pltpu.SMEM),
                  pl.BlockSpec((tm, d), lambda t: (t, 0)),
                  pl.BlockSpec((tm, LANES), lambda t: (t, 0)),
                  pl.BlockSpec((1, r, d), lambda t: (t // tiles_per_mod, 0, 0)),
                  pl.BlockSpec(memory_space=pl.ANY)],
        out_specs=pl.BlockSpec((tm, d), lambda t: (t, 0)),
        out_shape=jax.ShapeDtypeStruct((m, d), F32),
        scratch_shapes=[pltpu.VMEM((2, 2 * tm, d), F32), pltpu.SemaphoreType.DMA((2,))],
        compiler_params=_params(("arbitrary",)),
        name="moe_combine",
    )(dest3, dest3, x2d, route, gt, out_buf)


def moe_dispatch(eid):
    a = eid.size
    flat_e = eid.reshape(a)
    order = jnp.argsort(flat_e, stable=True)
    rank = jnp.argsort(order)
    experts = jnp.arange(N_EXPERTS, dtype=I32)
    pick = lambda table, ids: jnp.sum(jnp.where(ids[:, None] == experts[None, :], table[None, :], 0), axis=1)
    counts = jnp.sum((flat_e[:, None] == experts[None, :]).astype(I32), axis=0)
    padded = (counts + MOE_BLOCK - 1) // MOE_BLOCK * MOE_BLOCK
    pend = jnp.cumsum(padded)
    pstart = pend - padded
    cstart = jnp.cumsum(counts) - counts
    dest = (pick(pstart - cstart, flat_e) + rank).astype(I32)
    nblk = -(-(a + N_EXPERTS * (MOE_BLOCK - 1)) // MOE_BLOCK)
    blk_start = jnp.arange(nblk, dtype=I32) * MOE_BLOCK
    blk_e = jnp.minimum(jnp.sum((pend[None, :] <= blk_start[:, None]).astype(I32), axis=1), N_EXPERTS - 1)
    n_used = (pend[-1] // MOE_BLOCK).astype(I32).reshape(1)
    k = (blk_start - pick(pstart, blk_e))[:, None] + jnp.arange(MOE_BLOCK, dtype=I32)[None, :]
    filled = (k >= 0) & (k < pick(counts, blk_e)[:, None])
    sorted_pos = jnp.clip(pick(cstart, blk_e)[:, None] + k, 0, a - 1)
    src_tok = jnp.where(filled, order[sorted_pos] // 2, 0).astype(I32).reshape(nblk * MOE_BLOCK)
    return dest, src_tok, blk_e, n_used


def moe_layer(xs, mods, rows_per_mods, g_ffn, w_grp, b_grp, w_rt, b_rt, w_gate, w_up, w_down, layer, tms):
    n_route = N_GROUPS + N_EXPERTS
    w_route = jnp.pad(jnp.concatenate([w_grp, w_rt], axis=1), ((0, 0), (0, LANES - n_route)))
    b_route = jnp.pad(jnp.concatenate([b_grp, b_rt]), (0, LANES - n_route)).reshape(1, LANES)
    hs, routes = [], []
    for x, (sc, sh, _), rpm, tm in zip(xs, mods, rows_per_mods, tms):
        h, rt = router_call(x, g_ffn, sc, sh, w_route, b_route, tm=tm, rows_per_mod=rpm)
        hs.append(h)
        routes.append(rt)
    route = jnp.concatenate(routes) if len(routes) > 1 else routes[0]
    dest, src_tok, blk_e, n_used = moe_dispatch(route[:, :2].astype(I32))
    h_all = jnp.concatenate(hs) if len(hs) > 1 else hs[0]
    out_buf = ffn_call(blk_e, n_used, src_tok, h_all, w_gate, w_up, w_down, layer)
    outs, off = [], 0
    for x, (_, _, gt), rt, rpm, tm in zip(xs, mods, routes, rows_per_mods, tms):
        m = x.shape[0]
        outs.append(combine_call(x, out_buf, dest[2 * off:2 * (off + m)], rt, gt,
                                 tm=min(tm, MOE_BLOCK), rows_per_mod=rpm))
        off += m
    return outs


MEAN_PAGES = 16
SCORE_PAGES = 16


def _page_spec(block, layer, slot_of):
    def index_map(*args):
        ids, pages = args[:2], args[2]
        return (layer, pages[slot_of(*ids)]) + (0,) * (len(block) - 2)
    return pl.BlockSpec(block, index_map)


def _block_mean_body(pt_ref, *refs):
    pages, o_ref = refs[:MEAN_PAGES], refs[MEAN_PAGES]
    per_blk = MOBA_BLOCK // PAGE_SIZE
    for blk in range(MEAN_PAGES // per_blk):
        tot = sum(jnp.sum(pages[blk * per_blk + p][0, 0], axis=0) for p in range(per_blk))
        o_ref[0, 0, blk] = tot * (1.0 / MOBA_BLOCK)


def paged_block_mean_call(cache, layer, pt_flat, nb, n_pages):
    per_step = MEAN_PAGES * PAGE_SIZE // MOBA_BLOCK
    specs = [_page_spec((1, 1, PAGE_SIZE, N_KV_HEADS, HEAD_DIM), layer, functools.partial(
        lambda p, b, g: b * n_pages + g * MEAN_PAGES + p, p)) for p in range(MEAN_PAGES)]
    out = pl.pallas_call(
        _block_mean_body,
        grid_spec=pltpu.PrefetchScalarGridSpec(
            num_scalar_prefetch=1, grid=(nb, n_pages // MEAN_PAGES), in_specs=specs,
            out_specs=pl.BlockSpec((1, 1, per_step, N_KV_HEADS, HEAD_DIM), lambda b, g, pt: (b, g, 0, 0, 0))),
        out_shape=jax.ShapeDtypeStruct((nb, n_pages // MEAN_PAGES, per_step, N_KV_HEADS, HEAD_DIM), F32),
        compiler_params=_params(("arbitrary", "arbitrary")),
        name="paged_block_mean",
    )(pt_flat, *([cache] * MEAN_PAGES))
    return out.reshape(nb, n_pages * PAGE_SIZE // MOBA_BLOCK, N_KV_HEADS, HEAD_DIM)


def _sample_select_body(q_ref, kbar_ref, o_ref, *, own):
    q = q_ref[0]
    row = lax.broadcasted_iota(I32, (N_HEADS, LANES), 0)
    lane = lax.broadcasted_iota(I32, (N_HEADS, LANES), 1)
    gate = jnp.zeros((N_HEADS, LANES), F32)
    for kv in range(N_KV_HEADS):
        gk = lax.dot_general(q.astype(BF16), kbar_ref[0, :, kv, :].astype(BF16), (((1,), (1,)), ((), ())),
                             preferred_element_type=F32)
        gate = jnp.where(row // GQA_GROUP == kv, gk, gate)
    gate = jnp.where(lane < own, gate, NEG)
    out = jnp.zeros((N_HEADS, LANES), I32)
    for c in range(MOBA_TOPK):
        mx = jnp.max(gate, axis=-1, keepdims=True)
        idx = jnp.min(jnp.where(gate == mx, lane, LANES), axis=-1, keepdims=True)
        out = jnp.where(lane == c, idx, out)
        gate = jnp.where(lane == idx, -jnp.inf, gate)
    o_ref[0] = out


def sample_moba_select_call(q3, kbar_pad, own):
    nb = q3.shape[0]
    return pl.pallas_call(
        functools.partial(_sample_select_body, own=own),
        grid=(nb,),
        in_specs=[pl.BlockSpec((1, N_HEADS, HEAD_DIM), lambda b: (b, 0, 0)),
                  pl.BlockSpec((1, LANES, N_KV_HEADS, HEAD_DIM), lambda b: (b, 0, 0, 0))],
        out_specs=pl.BlockSpec((1, N_HEADS, LANES), lambda b: (b, 0, 0)),
        out_shape=jax.ShapeDtypeStruct((nb, N_HEADS, LANES), I32),
        compiler_params=_params(("arbitrary",)),
        name="sample_moba_select",
    )(q3, kbar_pad)


SEL_PAGES = MOBA_TOPK * (MOBA_BLOCK // PAGE_SIZE)


def _sample_moba_attn_body(pg_ref, blk_ref, q_ref, kn_ref, vn_ref, prev_ref, far_ref, own_ref, kc_ref, vc_ref,
                           o_ref, kbuf, vbuf, sems, *, own, layer):
    b = pl.program_id(0)
    per_blk = MOBA_BLOCK // PAGE_SIZE
    copies = []
    for h in range(N_HEADS):
        kv = h // GQA_GROUP
        for s in range(SEL_PAGES):
            page = pg_ref[(b * N_HEADS + h) * SEL_PAGES + s]
            rows = pl.ds(s * PAGE_SIZE, PAGE_SIZE)
            copies.append(pltpu.make_async_copy(kc_ref.at[layer, page, :, kv, :], kbuf.at[h, rows], sems.at[0]))
            copies.append(pltpu.make_async_copy(vc_ref.at[layer, page, :, kv, :], vbuf.at[h, rows], sems.at[1]))
    for cp in copies:
        cp.start()
    for cp in copies:
        cp.wait()

    for h in range(N_HEADS):
        kv = h // GQA_GROUP
        q = q_ref[0, h:h + 1, :]
        q8 = jnp.broadcast_to(q, (8, HEAD_DIM)).astype(BF16)
        s = lax.dot_general(q8, kbuf[h].astype(BF16), (((1,), (1,)), ((), ())),
                            preferred_element_type=F32)
        far = jnp.broadcast_to(far_ref[h, :, 0:1], (1, MOBA_BLOCK))
        bias = jnp.concatenate(
            [jnp.where(blk_ref[(b * N_HEADS + h) * MOBA_TOPK + n] == own - 1, prev_ref[h], far)
             for n in range(MOBA_TOPK)], axis=1)
        s = s * ATTN_SCALE + bias
        s_own = (jnp.sum(q * kn_ref[0, kv:kv + 1, :], axis=-1, keepdims=True) * ATTN_SCALE
                 + own_ref[h, :, 0:1])
        m = jnp.maximum(jnp.max(s, axis=-1, keepdims=True), s_own)
        p = jnp.exp(s - m)
        p_own = jnp.exp(s_own - m)
        l = jnp.sum(p, axis=-1, keepdims=True) + p_own
        acc = jnp.dot(p.astype(BF16), vbuf[h].astype(BF16), preferred_element_type=F32)
        o_ref[0, h:h + 1, :] = ((acc + p_own * vn_ref[0, kv:kv + 1, :]) / l)[0:1]


def sample_moba_attn_call(q3, k_new, v_new, cache_k, cache_v, layer, pages, blks, prev_row, far, own_bias, own):
    nb = q3.shape[0]
    whole = lambda shape: pl.BlockSpec(shape, lambda b, pg, bl: (0,) * len(shape))
    out = pl.pallas_call(
        functools.partial(_sample_moba_attn_body, own=own, layer=layer),
        grid_spec=pltpu.PrefetchScalarGridSpec(
            num_scalar_prefetch=2, grid=(nb,),
            in_specs=[pl.BlockSpec((1, N_HEADS, HEAD_DIM), lambda b, pg, bl: (b, 0, 0)),
                      pl.BlockSpec((1, N_KV_HEADS, HEAD_DIM), lambda b, pg, bl: (b, 0, 0)),
                      pl.BlockSpec((1, N_KV_HEADS, HEAD_DIM), lambda b, pg, bl: (b, 0, 0)),
                      whole((N_HEADS, 1, MOBA_BLOCK)), whole((N_HEADS, 1, LANES)), whole((N_HEADS, 1, LANES)),
                      pl.BlockSpec(memory_space=pl.ANY), pl.BlockSpec(memory_space=pl.ANY)],
            out_specs=pl.BlockSpec((1, N_HEADS, HEAD_DIM), lambda b, pg, bl: (b, 0, 0)),
            scratch_shapes=[pltpu.VMEM((N_HEADS, SEL_PAGES * PAGE_SIZE, HEAD_DIM), F32),
                            pltpu.VMEM((N_HEADS, SEL_PAGES * PAGE_SIZE, HEAD_DIM), F32),
                            pltpu.SemaphoreType.DMA((2,))]),
        out_shape=jax.ShapeDtypeStruct((nb, N_HEADS, HEAD_DIM), F32),
        compiler_params=_params(("arbitrary",)),
        name="sample_moba_attn",
    )(pages, blks, q3, k_new, v_new, prev_row, far, own_bias, cache_k, cache_v)
    return out.reshape(nb, Q_DIM)


def _indexer_score(qi, wi_col, keys_pad):
    s = lax.dot_general(qi.astype(BF16), keys_pad.astype(BF16), (((1,), (1,)), ((), ())),
                        preferred_element_type=F32)
    return jnp.sum(jnp.maximum(s * IDX_SCALE, 0.0) * wi_col, axis=0, keepdims=True) * IDX_W_SCALE


def _sample_score_body(pt_ref, qi_ref, wi_ref, *refs):
    pages, o_ref = refs[:SCORE_PAGES], refs[SCORE_PAGES]
    pad = jnp.zeros((PAGE_SIZE, LANES - IDX_DIM), F32)
    for p in range(SCORE_PAGES):
        keys = jnp.concatenate([pages[p][0, 0], pad], axis=1)
        o_ref[0, p:p + 1, :] = _indexer_score(qi_ref[0], wi_ref[0], keys)


def sample_dsa_score_call(cache_kidx, layer, pt_flat, qi_pad, wi_col, n_pages):
    nb = qi_pad.shape[0]
    specs = [_page_spec((1, 1, PAGE_SIZE, IDX_DIM), layer, functools.partial(
        lambda p, b, g: b * n_pages + g * SCORE_PAGES + p, p)) for p in range(SCORE_PAGES)]
    return pl.pallas_call(
        _sample_score_body,
        grid_spec=pltpu.PrefetchScalarGridSpec(
            num_scalar_prefetch=1, grid=(nb, n_pages // SCORE_PAGES),
            in_specs=[pl.BlockSpec((1, IDX_HEADS, LANES), lambda b, g, pt: (b, 0, 0)),
                      pl.BlockSpec((1, IDX_HEADS, 1), lambda b, g, pt: (b, 0, 0))] + specs,
            out_specs=pl.BlockSpec((1, SCORE_PAGES, PAGE_SIZE), lambda b, g, pt: (b, g, 0))),
        out_shape=jax.ShapeDtypeStruct((nb, n_pages, PAGE_SIZE), F32),
        compiler_params=_params(("arbitrary", "arbitrary")),
        name="sample_dsa_score",
    )(pt_flat, qi_pad, wi_col, *([cache_kidx] * SCORE_PAGES))


def _sample_dsa_select_body(sc_ref, qi_ref, wi_ref, ki_ref, m_ref, mnew_ref, *, n_sel):
    keys = _sortable_key(sc_ref[0])
    key_new = _sortable_key(_indexer_score(qi_ref[0], wi_ref[0], jnp.broadcast_to(ki_ref[0], (8, LANES))))[:, 0:1]

    def count_ge(cand):
        hit = jnp.where(keys >= cand, 1, 0).astype(I32)
        tot = jnp.sum(jnp.sum(hit, axis=0, keepdims=True), axis=1, keepdims=True)
        return tot + jnp.where(key_new >= cand, 1, 0).astype(I32)

    thr = _kth_largest_key(count_ge, n_sel, (1, 1))
    m_ref[0] = jnp.where(keys >= thr, 1.0, 0.0)
    mnew_ref[0] = jnp.broadcast_to(jnp.where(key_new >= thr, 1.0, 0.0), (1, LANES))


def sample_dsa_select_call(scores, qi_pad, wi_col, ki_pad, n_sel):
    nb, n_pages, _ = scores.shape
    return pl.pallas_call(
        functools.partial(_sample_dsa_select_body, n_sel=n_sel),
        grid=(nb,),
        in_specs=[pl.BlockSpec((1, n_pages, PAGE_SIZE), lambda b: (b, 0, 0)),
                  pl.BlockSpec((1, IDX_HEADS, LANES), lambda b: (b, 0, 0)),
                  pl.BlockSpec((1, IDX_HEADS, 1), lambda b: (b, 0, 0)),
                  pl.BlockSpec((1, 1, LANES), lambda b: (b, 0, 0))],
        out_specs=[pl.BlockSpec((1, n_pages, PAGE_SIZE), lambda b: (b, 0, 0)),
                   pl.BlockSpec((1, 1, LANES), lambda b: (b, 0, 0))],
        out_shape=[jax.ShapeDtypeStruct((nb, n_pages, PAGE_SIZE), F32), jax.ShapeDtypeStruct((nb, 1, LANES), F32)],
        compiler_params=_params(("arbitrary",)),
        name="sample_dsa_select",
    )(scores, qi_pad, wi_col, ki_pad)


def _row_copy(cache_hbm, layer, dst_vmem, row, slot, sem):
    page = lax.shift_right_logical(row, PAGE_SIZE.bit_length() - 1)
    off = row & (PAGE_SIZE - 1)
    return pltpu.make_async_copy(cache_hbm.at[layer, page, pl.ds(off, 1)], dst_vmem.at[pl.ds(slot, 1)], sem)


def _sample_dsa_attn_body(rows_ref, q_ref, valid_ref, bucket_ref, tab_ref, kn_ref, vn_ref, new_ref,
                          kc_ref, vc_ref, o_ref, kbuf, vbuf, sems, *, n_sel, layer):
    b = pl.program_id(0)

    def start(r, carry):
        row = rows_ref[b * n_sel + r]
        _row_copy(kc_ref, layer, kbuf, row, r, sems.at[0]).start()
        _row_copy(vc_ref, layer, vbuf, row, r, sems.at[1]).start()
        return carry

    def wait(r, carry):
        _row_copy(kc_ref, layer, kbuf, 0, r, sems.at[0]).wait()
        _row_copy(vc_ref, layer, vbuf, 0, r, sems.at[1]).wait()
        return carry

    lax.fori_loop(0, n_sel, start, 0)
    lax.fori_loop(0, n_sel, wait, 0)

    onehot = jnp.where(lax.broadcasted_iota(I32, (LANES, n_sel), 0) == bucket_ref[0], 1.0, 0.0)
    bias = jnp.dot(tab_ref[...], onehot, precision=lax.Precision.HIGHEST, preferred_element_type=F32)
    valid = valid_ref[0] > 0.5
    new_ok = new_ref[0, :, 0:1] > 0.5
    for kv in range(N_KV_HEADS):
        q8 = q_ref[0, kv]
        s = lax.dot_general(q8.astype(BF16), kbuf[:, kv, :].astype(BF16), (((1,), (1,)), ((), ())),
                            preferred_element_type=F32)
        bias8 = jnp.concatenate([bias[kv * GQA_GROUP:(kv + 1) * GQA_GROUP],
                                 jnp.zeros((8 - GQA_GROUP, n_sel), F32)], axis=0)
        s = jnp.where(valid, s * ATTN_SCALE + bias8, NEG)
        own_bias = jnp.concatenate([tab_ref[kv * GQA_GROUP:(kv + 1) * GQA_GROUP, 0:1],
                                    jnp.zeros((8 - GQA_GROUP, 1), F32)], axis=0)
        s_new = jnp.sum(q8 * kn_ref[0, kv:kv + 1, :], axis=-1, keepdims=True) * ATTN_SCALE + own_bias
        s_new = jnp.where(new_ok, s_new, NEG)
        m = jnp.maximum(jnp.max(s, axis=-1, keepdims=True), s_new)
        p = jnp.exp(s - m)
        p_new = jnp.exp(s_new - m)
        l = jnp.sum(p, axis=-1, keepdims=True) + p_new
        acc = jnp.dot(p.astype(BF16), vbuf[:, kv, :].astype(BF16), preferred_element_type=F32)
        o_ref[0, kv] = (acc + p_new * vn_ref[0, kv:kv + 1, :]) / l


def sample_dsa_attn_call(rows, q8, valid, bucket, table_t, k_new, v_new, new_sel, cache_k, cache_v, layer, n_sel):
    nb = q8.shape[0]
    return pl.pallas_call(
        functools.partial(_sample_dsa_attn_body, n_sel=n_sel, layer=layer),
        grid_spec=pltpu.PrefetchScalarGridSpec(
            num_scalar_prefetch=1, grid=(nb,),
            in_specs=[pl.BlockSpec((1, N_KV_HEADS, 8, HEAD_DIM), lambda b, rw: (b, 0, 0, 0)),
                      pl.BlockSpec((1, 1, n_sel), lambda b, rw: (b, 0, 0)),
                      pl.BlockSpec((1, 1, n_sel), lambda b, rw: (b, 0, 0)),
                      pl.BlockSpec((N_HEADS, LANES), lambda b, rw: (0, 0)),
                      pl.BlockSpec((1, N_KV_HEADS, HEAD_DIM), lambda b, rw: (b, 0, 0)),
                      pl.BlockSpec((1, N_KV_HEADS, HEAD_DIM), lambda b, rw: (b, 0, 0)),
                      pl.BlockSpec((1, 1, LANES), lambda b, rw: (b, 0, 0)),
                      pl.BlockSpec(memory_space=pl.ANY),
                      pl.BlockSpec(memory_space=pl.ANY)],
            out_specs=pl.BlockSpec((1, N_KV_HEADS, 8, HEAD_DIM), lambda b, rw: (b, 0, 0, 0)),
            scratch_shapes=[pltpu.VMEM((n_sel, N_KV_HEADS, HEAD_DIM), F32),
                            pltpu.VMEM((n_sel, N_KV_HEADS, HEAD_DIM), F32),
                            pltpu.SemaphoreType.DMA((2,))]),
        out_shape=jax.ShapeDtypeStruct((nb, N_KV_HEADS, 8, HEAD_DIM), F32),
        compiler_params=_params(("arbitrary",)),
        name="sample_dsa_attn",
    )(rows, q8, valid, bucket, table_t, k_new, v_new, new_sel, cache_k, cache_v)


def _rel_bucket(dist):
    n = jnp.maximum(dist, 0)
    exact = REL_BUCKETS // 2
    logr = jnp.log(jnp.maximum(n, 1).astype(F32) / exact) / math.log(REL_MAX_DIST / exact)
    large = jnp.minimum(exact + (logr * (REL_BUCKETS - exact)).astype(I32), REL_BUCKETS - 1)
    return jnp.where(n < exact, n, large)


def _head_gain(qn, kn, n_cols):
    return jnp.concatenate([jnp.tile(qn, N_HEADS), jnp.tile(kn, N_KV_HEADS),
                            jnp.ones((n_cols - Q_DIM - KV_DIM,), F32)]).reshape(1, n_cols)


def _project(x2d, g, sc, sh, w, layer, qn, kn, *, tm, rows_per_mod):
    n_w = w.shape[2]
    n_main = IN_DIM_A if n_w == IN_DIM_A else IN_DIM_A + IDX_Q_DIM
    z, k_out, v_out = norm_proj_call(x2d, g, sc, sh, w, layer, _head_gain(qn, kn, n_main), n_cols=n_main,
                                     n_norm_cols=Q_DIM + KV_DIM, tm=tm, rows_per_mod=rows_per_mod, kv_out=True)
    if n_w == IN_DIM_A:
        return z, k_out, v_out, None
    w_tail = jnp.pad(w[layer, :, n_main:], ((0, 0), (0, LANES - (n_w - n_main))))[None]
    tail, = norm_proj_call(x2d, g, sc, sh, w_tail, 0, jnp.ones((1, LANES), F32), n_cols=LANES, n_norm_cols=0,
                           tm=tm, rows_per_mod=rows_per_mod, tn=LANES)
    return z, k_out, v_out, tail


def moba_sample_attention(zs, k_new, v_new, cache_k, cache_v, layer, page_table, prev_row, far, own_bias):
    nb, n_pages = page_table.shape
    own = n_pages * PAGE_SIZE // MOBA_BLOCK
    kbar = paged_block_mean_call(cache_k, layer, page_table.reshape(-1), nb, n_pages)
    kbar_pad = jnp.pad(kbar, ((0, 0), (0, LANES - own), (0, 0), (0, 0)))
    q3 = zs[:, :Q_DIM].reshape(nb, N_HEADS, HEAD_DIM)
    sel = sample_moba_select_call(q3, kbar_pad, own)[:, :, :MOBA_TOPK]
    per_blk = MOBA_BLOCK // PAGE_SIZE
    logical = sel[..., None] * per_blk + jnp.arange(per_blk)
    pages = page_table[jnp.arange(nb)[:, None, None, None], logical]
    return sample_moba_attn_call(q3, k_new, v_new, cache_k, cache_v, layer, pages.reshape(-1), sel.reshape(-1),
                                 prev_row, far, own_bias, own)


def dsa_sample_attention(zs, tail_s, k_new, v_new, cache_k, cache_v, cache_kidx, layer, page_table, table_t):
    nb, n_pages = page_table.shape
    past = n_pages * PAGE_SIZE
    n_sel = min(DSA_TOPK, (past + 1) // 4)
    qi_pad = jnp.pad(zs[:, IN_DIM_A:].reshape(nb, IDX_HEADS, IDX_DIM), ((0, 0), (0, 0), (0, LANES - IDX_DIM)))
    wi_col = tail_s[:, IDX_DIM:IDX_DIM + IDX_HEADS].reshape(nb, IDX_HEADS, 1)
    ki_pad = jnp.pad(tail_s[:, :IDX_DIM], ((0, 0), (0, LANES - IDX_DIM))).reshape(nb, 1, LANES)
    scores = sample_dsa_score_call(cache_kidx, layer, page_table.reshape(-1), qi_pad, wi_col, n_pages)
    picked, new_sel = sample_dsa_select_call(scores, qi_pad, wi_col, ki_pad, n_sel)
    hit = (picked > 0.5).astype(I32)
    in_page = jnp.cumsum(hit, axis=2)
    page_tot = in_page[:, :, -1]
    running = (in_page + (jnp.cumsum(page_tot, axis=1) - page_tot)[:, :, None]).reshape(nb, past)
    slots = jnp.arange(1, n_sel + 1, dtype=I32)
    idx = jnp.sum((running[:, None, :] < slots[None, :, None]).astype(I32), axis=-1)
    pos = jnp.minimum(idx, past - 1)
    rows = page_table[jnp.arange(nb)[:, None], pos // PAGE_SIZE] * PAGE_SIZE + pos % PAGE_SIZE
    valid = (idx < past).astype(F32)
    bucket = _rel_bucket(past - pos).astype(I32)
    q8 = jnp.pad(zs[:, :Q_DIM].reshape(nb, N_KV_HEADS, GQA_GROUP, HEAD_DIM),
                 ((0, 0), (0, 0), (0, 8 - GQA_GROUP), (0, 0)))
    o8 = sample_dsa_attn_call(rows.reshape(-1), q8, valid[:, None], bucket[:, None], table_t,
                              k_new, v_new, new_sel, cache_k, cache_v, layer, n_sel)
    return o8[:, :, :GQA_GROUP].reshape(nb, Q_DIM)


def kernel(x_prompt, x_sample, cache_k_a, cache_v_a, cache_k_b, cache_v_b, cache_kidx_b, page_table, c_prompt, c_sample, rel_table, ada_w, ada_b, norm_attn, norm_ffn, q_norm, k_norm, w_in_a, w_in_b, w_o, w_grp, b_grp, w_rt, b_rt, w_gate, w_up, w_down):
    b, s, d = x_prompt.shape
    nb = x_sample.shape[0]
    depth = ada_w.shape[0]
    n_pages = page_table.shape[1]
    past = n_pages * PAGE_SIZE
    own = past // MOBA_BLOCK
    assert x_sample.shape[1] == 1 and past % MOBA_BLOCK == 0 and MOBA_TOPK <= own <= LANES
    assert s % ROW_TILE == 0 and nb % 8 == 0

    n_mod_rows = -(-(b + nb) // 8) * 8
    c_all = jnp.concatenate([c_prompt, c_sample, jnp.zeros((n_mod_rows - b - nb, d), F32)])
    mod = adaln_call(c_all, ada_w, ada_b)

    bias = bias_tiles_call(rel_table)
    lane_rep = lambda v: jnp.broadcast_to(v[:, None, None], (N_HEADS, 1, LANES))
    far = lane_rep(rel_table[REL_BUCKETS - 1])
    far2 = far * LOG2E
    own_bias = lane_rep(rel_table[0])
    prev_row = rel_table[_rel_bucket(MOBA_BLOCK - jnp.arange(MOBA_BLOCK))].T[:, None, :]
    table_t = jnp.pad(rel_table.T, ((0, 0), (0, LANES - REL_BUCKETS)))

    xp = x_prompt.reshape(b * s, d)
    xs = x_sample.reshape(nb, d)
    kv_p = {"ka": [], "va": [], "kb": [], "vb": [], "ib": []}
    kv_s = {"ka": [], "va": [], "kb": [], "vb": [], "ib": []}

    for i in range(depth):
        j = i // 2
        mp = mod[i, :b].reshape(b, 1, 6, d)
        ms = mod[i, b:b + nb].reshape(1, nb, 6, d)
        sh1, sc1, gt1, sh2, sc2, gt2 = (mp[:, :, k] for k in range(6))
        sh1s, sc1s, gt1s, sh2s, sc2s, gt2s = (ms[:, :, k] for k in range(6))
        moba = i % 2 == 0
        w_in = w_in_a if moba else w_in_b
        z, k_p, v_p, tail = _project(xp, norm_attn[i], sc1, sh1, w_in, j, q_norm[i], k_norm[i],
                                     tm=ROW_TILE, rows_per_mod=s)
        zs, k_s, v_s, tail_s = _project(xs, norm_attn[i], sc1s, sh1s, w_in, j, q_norm[i], k_norm[i],
                                        tm=nb, rows_per_mod=nb)
        z3 = z.reshape(b, s, -1)
        tag = "a" if moba else "b"
        kv_p["k" + tag].append(k_p.reshape(b, s, N_KV_HEADS, HEAD_DIM))
        kv_p["v" + tag].append(v_p.reshape(b, s, N_KV_HEADS, HEAD_DIM))
        kv_s["k" + tag].append(k_s.reshape(nb, 1, N_KV_HEADS, HEAD_DIM))
        kv_s["v" + tag].append(v_s.reshape(nb, 1, N_KV_HEADS, HEAD_DIM))

        if moba:
            sel = moba_select_call(z3, s // MOBA_BLOCK)
            o = attn_call(z3, sel, jnp.zeros((b, 1, s), I32), bias, far2, dsa=False)

            o_s = moba_sample_attention(zs, k_s, v_s, cache_k_a, cache_v_a, j, page_table, prev_row, far, own_bias)
        else:
            tail3 = tail.reshape(b, s, LANES)
            kv_p["ib"].append(tail3[:, :, :IDX_DIM])
            kv_s["ib"].append(tail_s[:, None, :IDX_DIM])
            wi_t = jnp.swapaxes(tail3[:, :, IDX_DIM:IDX_DIM + IDX_HEADS], 1, 2) * (IDX_SCALE * IDX_W_SCALE)
            key_t, thr = dsa_score_call(z3, tail3, wi_t, min(DSA_TOPK, s // 4))
            o = attn_call(z3, key_t, thr, bias, far2, dsa=True)

            o_s = dsa_sample_attention(zs, tail_s, k_s, v_s, cache_k_b, cache_v_b, cache_kidx_b, j, page_table,
                                       table_t)

        xp = linear_residual_call(o.reshape(b * s, Q_DIM), w_o, i, xp, gt1, tm=ROW_TILE, rows_per_mod=s)
        xs = linear_residual_call(o_s.astype(BF16), w_o, i, xs, gt1s, tm=nb, rows_per_mod=nb)
        xp, xs = moe_layer([xp, xs], [(sc2, sh2, gt2), (sc2s, sh2s, gt2s)], [s, nb], norm_ffn[i],
                           w_grp[i], b_grp[i], w_rt[i], b_rt[i], w_gate, w_up, w_down, i, tms=[512, nb])

    stack = lambda lst: jnp.stack(lst)
    return (xp.reshape(b, s, d), xs.reshape(nb, 1, d),
            stack(kv_p["ka"]), stack(kv_p["va"]), stack(kv_p["kb"]), stack(kv_p["vb"]), stack(kv_p["ib"]),
            stack(kv_s["ka"]), stack(kv_s["va"]), stack(kv_s["kb"]), stack(kv_s["vb"]), stack(kv_s["ib"]))
```

```python
import functools
import math

import numpy as np
import jax
import jax.numpy as jnp
from jax import lax
from jax.experimental import pallas as pl
from jax.experimental.pallas import tpu as pltpu

F32 = jnp.float32
BF16 = jnp.bfloat16
I32 = jnp.int32

D_MODEL = 2048
N_HEADS = 16
HEAD_DIM = 128
N_KV_HEADS = 4
GQA_GROUP = N_HEADS // N_KV_HEADS
Q_DIM = N_HEADS * HEAD_DIM
KV_DIM = N_KV_HEADS * HEAD_DIM
ATTN_SCALE = HEAD_DIM ** -0.5
MOBA_BLOCK = 256
MOBA_TOPK = 3
IDX_HEADS = 16
IDX_DIM = 64
IDX_SCALE = IDX_DIM ** -0.5
IDX_W_SCALE = IDX_HEADS ** -0.5
DSA_TOPK = 256
REL_BUCKETS = 32
REL_MAX_DIST = 128
N_GROUPS = 4
EXPERTS_PER_GROUP = 8
N_EXPERTS = N_GROUPS * EXPERTS_PER_GROUP
D_EXPERT = 768
PAGE_SIZE = 128
IN_DIM_A = Q_DIM + 2 * KV_DIM
IDX_Q_DIM = IDX_HEADS * IDX_DIM
RMS_EPS = 1e-6
NEG = -1e30
LOG2E = math.log2(math.e)

LANES = 128
ATT_TILE = 256
ROW_TILE = 1024
COL_TILE = 512
MOE_BLOCK = 256
FF_CHUNK = 256
VMEM_LIMIT = 56 * 1024 * 1024


def _params(sem):
    return pltpu.CompilerParams(dimension_semantics=sem, vmem_limit_bytes=VMEM_LIMIT)


def _adaln_body(c_ref, w_ref, b_ref, o_ref):
    c = c_ref[...]
    a = c * (1.0 / (1.0 + jnp.exp(-c)))
    o_ref[0] = jnp.dot(a, w_ref[0], preferred_element_type=F32) + b_ref[0]


def adaln_call(c_all, ada_w, ada_b):
    n_layers, d, n = ada_w.shape
    r = c_all.shape[0]
    tn = 1024
    return pl.pallas_call(
        _adaln_body,
        grid=(n_layers, n // tn),
        in_specs=[pl.BlockSpec((r, d), lambda l, j: (0, 0)),
                  pl.BlockSpec((1, d, tn), lambda l, j: (l, 0, j)),
                  pl.BlockSpec((1, 1, tn), lambda l, j: (l, 0, j))],
        out_specs=pl.BlockSpec((1, r, tn), lambda l, j: (l, 0, j)),
        out_shape=jax.ShapeDtypeStruct((n_layers, r, n), F32),
        compiler_params=_params(("arbitrary", "arbitrary")),
        name="adaln",
    )(c_all, ada_w, ada_b.reshape(n_layers, 1, n))


def _modulated_norm(x, g, sc, sh):
    ms = jnp.mean(x * x, axis=-1, keepdims=True)
    return (x * lax.rsqrt(ms + RMS_EPS) * g) * (1.0 + sc) + sh


def _norm_proj_body(x_ref, g_ref, sc_ref, sh_ref, w_ref, gain_ref, o_ref, *rest, n_norm_tiles, tn, kv_out):
    h_scr = rest[-1]
    j = pl.program_id(1)

    @pl.when(j == 0)
    def _():
        h_scr[...] = _modulated_norm(x_ref[...], g_ref[...], sc_ref[0], sh_ref[0]).astype(BF16)

    z = jnp.dot(h_scr[...], w_ref[0].astype(BF16), preferred_element_type=F32)

    @pl.when(j < n_norm_tiles)
    def _():
        for k in range(tn // HEAD_DIM):
            sl = slice(k * HEAD_DIM, (k + 1) * HEAD_DIM)
            zk = z[:, sl]
            ms = jnp.mean(zk * zk, axis=-1, keepdims=True)
            o_ref[:, sl] = zk * lax.rsqrt(ms + RMS_EPS) * gain_ref[:, sl]

    @pl.when(j >= n_norm_tiles)
    def _():
        o_ref[...] = z

    if kv_out:
        for ref, tile in ((rest[0], Q_DIM // tn), (rest[1], Q_DIM // tn + 1)):
            @pl.when(j == tile)
            def _(ref=ref):
                for kv in range(N_KV_HEADS):
                    ref[:, kv, :] = o_ref[:, kv * HEAD_DIM:(kv + 1) * HEAD_DIM]


def norm_proj_call(x2d, g, sc, sh, w, layer, gain, *, n_cols, n_norm_cols, tm, rows_per_mod, tn=COL_TILE,
                   kv_out=False):
    m, d = x2d.shape
    assert m % tm == 0 and n_cols % tn == 0 and n_norm_cols % tn == 0
    assert not kv_out or tn == KV_DIM
    r = sc.shape[1]
    tiles_per_mod = rows_per_mod // tm
    mod_spec = pl.BlockSpec((1, r, d), lambda i, j: (i // tiles_per_mod, 0, 0))
    out_specs = [pl.BlockSpec((tm, tn), lambda i, j: (i, j))]
    out_shape = [jax.ShapeDtypeStruct((m, n_cols), F32)]
    if kv_out:
        out_specs += [pl.BlockSpec((tm, N_KV_HEADS, HEAD_DIM), lambda i, j: (i, 0, 0))] * 2
        out_shape += [jax.ShapeDtypeStruct((m, N_KV_HEADS, HEAD_DIM), F32)] * 2
    return pl.pallas_call(
        functools.partial(_norm_proj_body, n_norm_tiles=n_norm_cols // tn, tn=tn, kv_out=kv_out),
        grid=(m // tm, n_cols // tn),
        in_specs=[pl.BlockSpec((tm, d), lambda i, j: (i, 0)),
                  pl.BlockSpec((1, d), lambda i, j: (0, 0)),
                  mod_spec, mod_spec,
                  pl.BlockSpec((1, d, tn), lambda i, j: (layer, 0, j)),
                  pl.BlockSpec((1, tn), lambda i, j: (0, j))],
        out_specs=out_specs,
        out_shape=out_shape,
        scratch_shapes=[pltpu.VMEM((tm, d), BF16)],
        compiler_params=_params(("arbitrary", "arbitrary")),
        name="norm_proj",
    )(x2d, g.reshape(1, d), sc, sh, w, gain)


def _linear_residual_body(a_ref, w_ref, x_ref, gt_ref, o_ref):
    y = jnp.dot(a_ref[...], w_ref[0].astype(BF16), preferred_element_type=F32)
    o_ref[...] = x_ref[...] + gt_ref[0] * y


def linear_residual_call(a, w, layer, x2d, gt, *, tm, rows_per_mod, tn=COL_TILE):
    m, k = a.shape
    n = w.shape[2]
    r = gt.shape[1]
    tiles_per_mod = rows_per_mod // tm
    return pl.pallas_call(
        _linear_residual_body,
        grid=(m // tm, n // tn),
        in_specs=[pl.BlockSpec((tm, k), lambda i, j: (i, 0)),
                  pl.BlockSpec((1, k, tn), lambda i, j: (layer, 0, j)),
                  pl.BlockSpec((tm, tn), lambda i, j: (i, j)),
                  pl.BlockSpec((1, r, tn), lambda i, j: (i // tiles_per_mod, 0, j))],
        out_specs=pl.BlockSpec((tm, tn), lambda i, j: (i, j)),
        out_shape=jax.ShapeDtypeStruct((m, n), F32),
        compiler_params=_params(("arbitrary", "arbitrary")),
        name="linear_residual",
    )(a, w, x2d, gt)


def _rel_bucket_np(dist):
    n = np.maximum(dist, 0)
    exact = REL_BUCKETS // 2
    logr = (np.log(np.maximum(n, 1).astype(np.float32) / np.float32(exact))
            / np.float32(math.log(REL_MAX_DIST / exact))).astype(np.float32)
    large = np.minimum(exact + (logr * np.float32(REL_BUCKETS - exact)).astype(np.int32), REL_BUCKETS - 1)
    return np.where(n < exact, n, large).astype(np.int32)


def _bias_body(bk_ref, tab_ref, o_ref):
    h = pl.program_id(0)
    bk = bk_ref[...]
    acc = jnp.zeros(bk.shape, F32)
    for b in range(REL_BUCKETS):
        acc = jnp.where(bk == b, tab_ref[b, h], acc)
    o_ref[0] = jnp.where(bk < 0, NEG, acc * LOG2E)


def bias_tiles_call(rel_table):
    t = np.arange(ATT_TILE)
    d = t[:, None] - t[None, :]
    buckets = np.stack([np.where(d >= 0, _rel_bucket_np(d), -1), _rel_bucket_np(d + ATT_TILE)]).astype(np.int32)
    return pl.pallas_call(
        _bias_body,
        grid=(N_HEADS,),
        in_specs=[pl.BlockSpec((2, ATT_TILE, ATT_TILE), lambda h: (0, 0, 0)),
                  pl.BlockSpec(memory_space=pltpu.SMEM)],
        out_specs=pl.BlockSpec((1, 2, ATT_TILE, ATT_TILE), lambda h: (h, 0, 0, 0)),
        out_shape=jax.ShapeDtypeStruct((N_HEADS, 2, ATT_TILE, ATT_TILE), F32),
        compiler_params=_params(("arbitrary",)),
        name="bias_tiles",
    )(jnp.asarray(buckets), rel_table)


def _moba_select_body(q_ref, k_ref, o_ref, kbar_scr, code_scr):
    i = pl.program_id(1)
    nblk = kbar_scr.shape[0]

    @pl.when(i == 0)
    def _():
        k = k_ref[0]
        kbar_scr[...] = jnp.mean(k.reshape(nblk, MOBA_BLOCK, KV_DIM), axis=1)
        code_scr[...] = jnp.zeros(code_scr.shape, F32)

    blk = lax.broadcasted_iota(I32, (nblk, ATT_TILE), 0)
    for h in range(N_HEADS):
        kv = h // GQA_GROUP
        gate = lax.dot_general(kbar_scr[:, kv * HEAD_DIM:(kv + 1) * HEAD_DIM].astype(BF16),
                               q_ref[0, :, h * HEAD_DIM:(h + 1) * HEAD_DIM].astype(BF16),
                               (((1,), (1,)), ((), ())), preferred_element_type=F32)
        gate = jnp.where(blk < i, gate, NEG)
        code = jnp.zeros((1, ATT_TILE), F32)
        for _ in range(MOBA_TOPK):
            mx = jnp.max(gate, axis=0, keepdims=True)
            idx = jnp.min(jnp.where(gate == mx, blk, nblk), axis=0, keepdims=True)
            code = code + jnp.where(idx < i, jnp.exp2(idx.astype(F32)), 0.0)
            gate = jnp.where(blk == idx, -jnp.inf, gate)
        code_scr[h:h + 1, :] = code
    o_ref[0] = code_scr[...].T


def moba_select_call(z3, n_blocks):
    b, s, _ = z3.shape
    return pl.pallas_call(
        _moba_select_body,
        grid=(b, s // ATT_TILE),
        in_specs=[pl.BlockSpec((1, ATT_TILE, Q_DIM), lambda bb, i: (bb, i, 0)),
                  pl.BlockSpec((1, s, KV_DIM), lambda bb, i: (bb, 0, Q_DIM // KV_DIM))],
        out_specs=pl.BlockSpec((1, ATT_TILE, LANES), lambda bb, i: (bb, i, 0)),
        out_shape=jax.ShapeDtypeStruct((b, s, LANES), F32),
        scratch_shapes=[pltpu.VMEM((n_blocks, KV_DIM), F32), pltpu.VMEM((LANES, ATT_TILE), F32)],
        compiler_params=_params(("arbitrary", "arbitrary")),
        name="moba_select",
    )(z3, z3)


def _flash_update(kv, q4, kb, vb, bias4, mask3, m_scr, l_scr, acc_scr):
    s = lax.dot_general(q4, kb, (((1,), (1,)), ((), ())), preferred_element_type=F32)
    s = s.reshape(GQA_GROUP, ATT_TILE, ATT_TILE) * (ATTN_SCALE * LOG2E) + bias4
    if mask3 is not None:
        s = jnp.where(mask3, s, NEG)
    m_prev = m_scr[kv]
    m_new = jnp.maximum(m_prev, jnp.max(s, axis=-1, keepdims=True))
    alpha = jnp.exp2(m_prev - m_new)
    p = jnp.exp2(s - jnp.concatenate([m_new] * (ATT_TILE // LANES), axis=-1))
    pv = jnp.dot(p.reshape(GQA_GROUP * ATT_TILE, ATT_TILE).astype(BF16), vb, preferred_element_type=F32)
    pv = pv.reshape(GQA_GROUP, ATT_TILE, HEAD_DIM + LANES)
    acc_scr[kv] = alpha * acc_scr[kv] + pv[:, :, :HEAD_DIM]
    l_scr[kv] = alpha * l_scr[kv] + pv[:, :, HEAD_DIM:]
    m_scr[kv] = m_new


def _attn_body(it_ref, jt_ref, q_ref, k_ref, v_ref, sel_ref, thr_ref, bias_ref, far_ref, o_ref, m_scr, l_scr,
               acc_scr, code_scr, q_scr, *, dsa):
    i = it_ref[pl.program_id(1)]
    j = jt_ref[pl.program_id(1)]

    @pl.when(j == 0)
    def _():
        m_scr[...] = jnp.full(m_scr.shape, NEG, F32)
        l_scr[...] = jnp.zeros(l_scr.shape, F32)
        acc_scr[...] = jnp.zeros(acc_scr.shape, F32)
        for h in range(N_HEADS):
            rows = pl.ds((h % GQA_GROUP) * ATT_TILE, ATT_TILE)
            q_scr[h // GQA_GROUP, rows, :] = q_ref[0, :, h * HEAD_DIM:(h + 1) * HEAD_DIM].astype(BF16)
        if not dsa:
            code = sel_ref[0].astype(I32)
            for h in range(N_HEADS):
                code_scr[h] = jnp.broadcast_to(code[:, h:h + 1], (ATT_TILE, LANES))

    def step(which):
        if dsa:
            mask_all = (jnp.where(sel_ref[0, 0] >= thr_ref[0], 1.0, 0.0).T > 0.5)[None]
        ones = jnp.ones((ATT_TILE, LANES), BF16)
        for kv in range(N_KV_HEADS):
            heads = range(kv * GQA_GROUP, (kv + 1) * GQA_GROUP)
            q4 = q_scr[kv]
            kb = k_ref[0, :, kv * HEAD_DIM:(kv + 1) * HEAD_DIM].astype(BF16)
            vb = jnp.concatenate([v_ref[0, :, kv * HEAD_DIM:(kv + 1) * HEAD_DIM].astype(BF16), ones], axis=1)
            if which == 2:
                bias4 = far_ref[kv * GQA_GROUP:(kv + 1) * GQA_GROUP, :, 0:1]
            else:
                bias4 = bias_ref[kv * GQA_GROUP:(kv + 1) * GQA_GROUP, which]
            if dsa:
                mask3 = mask_all
            elif which == 0:
                mask3 = None
            else:
                bits = jnp.stack([jnp.right_shift(code_scr[h], j) & 1 for h in heads])
                mask3 = jnp.concatenate([bits] * (ATT_TILE // LANES), axis=-1) > 0
            _flash_update(kv, q4, kb, vb, bias4, mask3, m_scr, l_scr, acc_scr)

    @pl.when(j == i)
    def _():
        step(0)

    @pl.when(j == i - 1)
    def _():
        step(1)

    @pl.when(j < i - 1)
    def _():
        step(2)

    @pl.when(j == i)
    def _():
        for kv in range(N_KV_HEADS):
            out = acc_scr[kv] / l_scr[kv]
            for g in range(GQA_GROUP):
                h = kv * GQA_GROUP + g
                o_ref[0, :, h * HEAD_DIM:(h + 1) * HEAD_DIM] = out[g].astype(o_ref.dtype)


def _tile_pairs(nt):
    pairs = [(i, j) for i in range(nt) for j in range(i + 1)]
    return (jnp.asarray([p[0] for p in pairs], I32), jnp.asarray([p[1] for p in pairs], I32))


def attn_call(z3, sel, thr, bias, far, *, dsa):
    b, s, _ = z3.shape
    it, jt = _tile_pairs(s // ATT_TILE)
    if dsa:
        sel_spec = pl.BlockSpec((1, 1, ATT_TILE, ATT_TILE), lambda bb, p, it, jt: (bb, p, 0, 0))
    else:
        sel_spec = pl.BlockSpec((1, ATT_TILE, LANES), lambda bb, p, it, jt: (bb, it[p], 0))
    kcol, vcol = Q_DIM // KV_DIM, Q_DIM // KV_DIM + 1
    return pl.pallas_call(
        functools.partial(_attn_body, dsa=dsa),
        grid_spec=pltpu.PrefetchScalarGridSpec(
            num_scalar_prefetch=2,
            grid=(b, it.shape[0]),
            in_specs=[pl.BlockSpec((1, ATT_TILE, Q_DIM), lambda bb, p, it, jt: (bb, it[p], 0)),
                      pl.BlockSpec((1, ATT_TILE, KV_DIM), lambda bb, p, it, jt: (bb, jt[p], kcol)),
                      pl.BlockSpec((1, ATT_TILE, KV_DIM), lambda bb, p, it, jt: (bb, jt[p], vcol)),
                      sel_spec,
                      pl.BlockSpec((1, 1, ATT_TILE), lambda bb, p, it, jt: (bb, 0, it[p])),
                      pl.BlockSpec((N_HEADS, 2, ATT_TILE, ATT_TILE), lambda bb, p, it, jt: (0, 0, 0, 0)),
                      pl.BlockSpec((N_HEADS, 1, LANES), lambda bb, p, it, jt: (0, 0, 0))],
            out_specs=pl.BlockSpec((1, ATT_TILE, Q_DIM), lambda bb, p, it, jt: (bb, it[p], 0)),
            scratch_shapes=[pltpu.VMEM((N_KV_HEADS, GQA_GROUP, ATT_TILE, LANES), F32),
                            pltpu.VMEM((N_KV_HEADS, GQA_GROUP, ATT_TILE, LANES), F32),
                            pltpu.VMEM((N_KV_HEADS, GQA_GROUP, ATT_TILE, HEAD_DIM), F32),
                            pltpu.VMEM((1 if dsa else N_HEADS, ATT_TILE, LANES), I32),
                            pltpu.VMEM((N_KV_HEADS, GQA_GROUP * ATT_TILE, HEAD_DIM), BF16)]),
        out_shape=jax.ShapeDtypeStruct((b, s, Q_DIM), BF16),
        compiler_params=_params(("arbitrary", "arbitrary")),
        name="dsa_attn" if dsa else "moba_attn",
    )(it, jt, z3, z3, z3, sel, thr, bias, far)


def _sortable_key(x):
    bits = pltpu.bitcast(x, I32)
    return bits ^ (jnp.right_shift(bits, 31) & 0x7FFFFFFF)


def _kth_largest_key(count_ge, k, shape):
    def bit_body(bi, t):
        cand = t ^ jnp.left_shift(jnp.int32(1), 31 - bi)
        return jnp.where(count_ge(cand) >= k, cand, t)
    return lax.fori_loop(0, 32, bit_body, jnp.full(shape, -2 ** 31, I32))


def _dsa_score_body(it_ref, jt_ref, qi_ref, tail_ref, wi_ref, key_ref, thr_ref, key_scr, *, n_sel):
    i = it_ref[pl.program_id(1)]
    j = jt_ref[pl.program_id(1)]

    def score_tile():
        t = tail_ref[0]
        lane = lax.broadcasted_iota(I32, t.shape, 1)
        klo = jnp.where(lane < IDX_DIM, t, 0.0)
        khi = pltpu.roll(klo, IDX_DIM, 1)
        qstk = jnp.concatenate([qi_ref[0, :, p * LANES:(p + 1) * LANES] for p in range(IDX_HEADS // 2)],
                               axis=0).astype(BF16)
        dn = (((1,), (1,)), ((), ()))
        s_lo = lax.dot_general(klo.astype(BF16), qstk, dn, preferred_element_type=F32)
        s_hi = lax.dot_general(khi.astype(BF16), qstk, dn, preferred_element_type=F32)
        sc = jnp.zeros((ATT_TILE, ATT_TILE), F32)
        for p in range(IDX_HEADS // 2):
            sl = slice(p * ATT_TILE, (p + 1) * ATT_TILE)
            sc = sc + jnp.maximum(s_lo[:, sl], 0.0) * wi_ref[0, 2 * p:2 * p + 1, :]
            sc = sc + jnp.maximum(s_hi[:, sl], 0.0) * wi_ref[0, 2 * p + 1:2 * p + 2, :]
        kpos = j * ATT_TILE + lax.broadcasted_iota(I32, sc.shape, 0)
        qpos = i * ATT_TILE + lax.broadcasted_iota(I32, sc.shape, 1)
        key = _sortable_key(jnp.where(kpos <= qpos, sc, NEG))
        key_ref[0, 0] = key
        key_scr[pl.ds(pl.multiple_of(j * ATT_TILE, ATT_TILE), ATT_TILE), :] = key

    score_tile()

    @pl.when(j == i)
    def _():
        def count_ge(cand):
            def chunk(c, cnt):
                kc = key_scr[pl.ds(pl.multiple_of(c * ATT_TILE, ATT_TILE), ATT_TILE), :]
                hit = jnp.where(kc >= cand, 1, 0).astype(I32)
                return cnt + jnp.sum(hit.reshape(ATT_TILE // 8, 8, ATT_TILE), axis=0)
            cnt8 = lax.fori_loop(0, i + 1, chunk, jnp.zeros((8, ATT_TILE), I32))
            return jnp.sum(cnt8, axis=0, keepdims=True)
        thr_ref[0] = _kth_largest_key(count_ge, n_sel, (1, ATT_TILE))


def dsa_score_call(z3, tail3, wi_t, n_sel):
    b, s, _ = z3.shape
    it, jt = _tile_pairs(s // ATT_TILE)
    n_pairs = it.shape[0]
    return pl.pallas_call(
        functools.partial(_dsa_score_body, n_sel=n_sel),
        grid_spec=pltpu.PrefetchScalarGridSpec(
            num_scalar_prefetch=2,
            grid=(b, n_pairs),
            in_specs=[pl.BlockSpec((1, ATT_TILE, IDX_Q_DIM),
                                   lambda bb, p, it, jt: (bb, it[p], IN_DIM_A // IDX_Q_DIM)),
                      pl.BlockSpec((1, ATT_TILE, LANES), lambda bb, p, it, jt: (bb, jt[p], 0)),
                      pl.BlockSpec((1, IDX_HEADS, ATT_TILE), lambda bb, p, it, jt: (bb, 0, it[p]))],
            out_specs=[pl.BlockSpec((1, 1, ATT_TILE, ATT_TILE), lambda bb, p, it, jt: (bb, p, 0, 0)),
                       pl.BlockSpec((1, 1, ATT_TILE), lambda bb, p, it, jt: (bb, 0, it[p]))],
            scratch_shapes=[pltpu.VMEM((s, ATT_TILE), I32)]),
        out_shape=[jax.ShapeDtypeStruct((b, n_pairs, ATT_TILE, ATT_TILE), I32),
                   jax.ShapeDtypeStruct((b, 1, s), I32)],
        compiler_params=_params(("arbitrary", "arbitrary")),
        name="dsa_score",
    )(it, jt, z3, tail3, wi_t)


def _router_body(x_ref, g_ref, sc_ref, sh_ref, wr_ref, br_ref, h_ref, r_ref):
    h = _modulated_norm(x_ref[...], g_ref[...], sc_ref[0], sh_ref[0])
    h_ref[...] = h.astype(h_ref.dtype)
    lg = jnp.dot(h.astype(BF16), wr_ref[...].astype(BF16), preferred_element_type=F32) + br_ref[...]
    lane = lax.broadcasted_iota(I32, lg.shape, 1)
    gl = jnp.where(lane < N_GROUPS, lg, -jnp.inf)
    gmax = jnp.max(gl, axis=-1, keepdims=True)
    g_sel = jnp.min(jnp.where(gl == gmax, lane, LANES), axis=-1, keepdims=True)
    g_w = 1.0 / jnp.sum(jnp.exp(gl - gmax), axis=-1, keepdims=True)
    lo = N_GROUPS + EXPERTS_PER_GROUP * g_sel
    el = jnp.where((lane >= lo) & (lane < lo + EXPERTS_PER_GROUP), lg, -jnp.inf)
    e1 = jnp.max(el, axis=-1, keepdims=True)
    i1 = jnp.min(jnp.where(el == e1, lane, LANES), axis=-1, keepdims=True)
    el2 = jnp.where(lane == i1, -jnp.inf, el)
    e2 = jnp.max(el2, axis=-1, keepdims=True)
    i2 = jnp.min(jnp.where(el2 == e2, lane, LANES), axis=-1, keepdims=True)
    v2 = jnp.exp(e2 - e1)
    w1 = g_w / (1.0 + v2)
    w2 = g_w * v2 / (1.0 + v2)
    r_ref[...] = jnp.where(lane == 0, (i1 - N_GROUPS).astype(F32),
                           jnp.where(lane == 1, (i2 - N_GROUPS).astype(F32),
                                     jnp.where(lane == 2, w1, jnp.where(lane == 3, w2, 0.0))))


def router_call(x2d, g, sc, sh, w_route, b_route, *, tm, rows_per_mod):
    m, d = x2d.shape
    r = sc.shape[1]
    tiles_per_mod = rows_per_mod // tm
    mod_spec = pl.BlockSpec((1, r, d), lambda i: (i // tiles_per_mod, 0, 0))
    return pl.pallas_call(
        _router_body,
        grid=(m // tm,),
        in_specs=[pl.BlockSpec((tm, d), lambda i: (i, 0)),
                  pl.BlockSpec((1, d), lambda i: (0, 0)),
                  mod_spec, mod_spec,
                  pl.BlockSpec((d, LANES), lambda i: (0, 0)),
                  pl.BlockSpec((1, LANES), lambda i: (0, 0))],
        out_specs=[pl.BlockSpec((tm, d), lambda i: (i, 0)), pl.BlockSpec((tm, LANES), lambda i: (i, 0))],
        out_shape=[jax.ShapeDtypeStruct((m, d), F32), jax.ShapeDtypeStruct((m, LANES), F32)],
        compiler_params=_params(("arbitrary",)),
        name="router",
    )(x2d, g.reshape(1, d), sc, sh, w_route, b_route)


def _row_dma_wait(ref_hbm, dst, n_rows, sem):
    pltpu.make_async_copy(ref_hbm.at[pl.ds(0, n_rows)], dst, sem).wait()


def _ffn_body(be_ref, nu_ref, cur_ref, nxt_ref, h_ref, wg_ref, wu_ref, wd_ref, o_ref, xbuf, sems):
    i = pl.program_id(0)
    nblk = pl.num_programs(0)
    nu = nu_ref[0]
    slot = i % 2

    def gather(src_ref, sl):
        for r in range(MOE_BLOCK):
            pltpu.make_async_copy(h_ref.at[pl.ds(src_ref[0, 0, r], 1)], xbuf.at[sl, pl.ds(r, 1)],
                                  sems.at[sl]).start()

    def gather_wait(sl):
        _row_dma_wait(h_ref, xbuf.at[sl], MOE_BLOCK, sems.at[sl])

    @pl.when(i == 0)
    def _():
        gather(cur_ref, 0)

    @pl.when(i < nu)
    def _():
        gather(nxt_ref, 1 - slot)
        gather_wait(slot)
        x = xbuf[slot].astype(BF16)
        acc = jnp.zeros(o_ref.shape, F32)
        for c in range(D_EXPERT // FF_CHUNK):
            sl = slice(c * FF_CHUNK, (c + 1) * FF_CHUNK)
            a = jnp.dot(x, wg_ref[0, 0, :, sl].astype(BF16), preferred_element_type=F32)
            u = jnp.dot(x, wu_ref[0, 0, :, sl].astype(BF16), preferred_element_type=F32)
            act = a * (1.0 / (1.0 + jnp.exp(-a))) * u
            acc = acc + jnp.dot(act.astype(BF16), wd_ref[0, 0, sl, :].astype(BF16), preferred_element_type=F32)
        o_ref[...] = acc

    @pl.when(i >= nu)
    def _():
        o_ref[...] = jnp.zeros(o_ref.shape, F32)

    @pl.when(i == nu)
    def _():
        gather_wait(slot)

    @pl.when((i == nblk - 1) & (nu == nblk))
    def _():
        gather_wait(1 - slot)


def ffn_call(blk_e, n_used, src_tok, h_all, w_gate, w_up, w_down, layer):
    d = h_all.shape[1]
    nblk = blk_e.shape[0]
    last = lambda i, nu: jnp.minimum(i, nu[0] - 1)
    src3 = src_tok.reshape(nblk, 1, MOE_BLOCK)
    w_in_spec = pl.BlockSpec((1, 1, d, D_EXPERT), lambda i, be, nu: (layer, be[last(i, nu)], 0, 0))
    grid_spec = pltpu.PrefetchScalarGridSpec(
        num_scalar_prefetch=2,
        grid=(nblk,),
        in_specs=[pl.BlockSpec((1, 1, MOE_BLOCK), lambda i, be, nu: (i, 0, 0), memory_space=pltpu.SMEM),
                  pl.BlockSpec((1, 1, MOE_BLOCK), lambda i, be, nu: (jnp.minimum(i + 1, nblk - 1), 0, 0),
                               memory_space=pltpu.SMEM),
                  pl.BlockSpec(memory_space=pl.ANY),
                  w_in_spec, w_in_spec,
                  pl.BlockSpec((1, 1, D_EXPERT, d), lambda i, be, nu: (layer, be[last(i, nu)], 0, 0))],
        out_specs=pl.BlockSpec((MOE_BLOCK, d), lambda i, be, nu: (i, 0)),
        scratch_shapes=[pltpu.VMEM((2, MOE_BLOCK, d), F32), pltpu.SemaphoreType.DMA((2,))],
    )
    return pl.pallas_call(
        _ffn_body,
        grid_spec=grid_spec,
        out_shape=jax.ShapeDtypeStruct((nblk * MOE_BLOCK, d), F32),
        compiler_params=_params(("arbitrary",)),
        name="expert_ffn",
    )(blk_e, n_used, src3, src3, h_all, w_gate, w_up, w_down)


def _combine_body(dcur_ref, dnxt_ref, x_ref, rt_ref, gt_ref, ob_ref, o_ref, rbuf, sems, *, tm):
    t = pl.program_id(0)
    slot = t % 2

    def issue(dref, sl):
        for r in range(2 * tm):
            dst = (r % 2) * tm + r // 2
            pltpu.make_async_copy(ob_ref.at[pl.ds(dref[0, 0, r], 1)], rbuf.at[sl, pl.ds(dst, 1)],
                                  sems.at[sl]).start()

    @pl.when(t == 0)
    def _():
        issue(dcur_ref, 0)

    @pl.when(t + 1 < pl.num_programs(0))
    def _():
        issue(dnxt_ref, 1 - slot)

    _row_dma_wait(ob_ref, rbuf.at[slot], 2 * tm, sems.at[slot])
    w0 = rt_ref[:, 2:3]
    w1 = rt_ref[:, 3:4]
    o_ref[...] = x_ref[...] + gt_ref[0] * (w0 * rbuf[slot, 0:tm] + w1 * rbuf[slot, tm:2 * tm])


def combine_call(x2d, out_buf, dest, route, gt, *, tm, rows_per_mod):
    m, d = x2d.shape
    r = gt.shape[1]
    nt = m // tm
    tiles_per_mod = rows_per_mod // tm
    dest3 = dest.reshape(nt, 1, 2 * tm)
    return pl.pallas_call(
        functools.partial(_combine_body, tm=tm),
        grid=(nt,),
        in_specs=[pl.BlockSpec((1, 1, 2 * tm), lambda t: (t, 0, 0), memory_space=pltpu.SMEM),
                  pl.BlockSpec((1, 1, 2 * tm), lambda t: (jnp.minimum(t + 1, nt - 1), 0, 0),
                               memory_space=pltpu.SMEM),
                  pl.BlockSpec((tm, d), lambda t: (t, 0)),
                  pl.BlockSpec((tm, LANES), lambda t: (t, 0)),
                  pl.BlockSpec((1, r, d), lambda t: (t // tiles_per_mod, 0, 0)),
                  pl.BlockSpec(memory_space=pl.ANY)],
        out_specs=pl.BlockSpec((tm, d), lambda t: (t, 0)),
        out_shape=jax.ShapeDtypeStruct((m, d), F32),
        scratch_shapes=[pltpu.VMEM((2, 2 * tm, d), F32), pltpu.SemaphoreType.DMA((2,))],
        compiler_params=_params(("arbitrary",)),
        name="moe_combine",
    )(dest3, dest3, x2d, route, gt, out_buf)


def moe_dispatch(eid):
    a = eid.size
    flat_e = eid.reshape(a)
    order = jnp.argsort(flat_e, stable=True)
    rank = jnp.argsort(order)
    experts = jnp.arange(N_EXPERTS, dtype=I32)
    pick = lambda table, ids: jnp.sum(jnp.where(ids[:, None] == experts[None, :], table[None, :], 0), axis=1)
    counts = jnp.sum((flat_e[:, None] == experts[None, :]).astype(I32), axis=0)
    padded = (counts + MOE_BLOCK - 1) // MOE_BLOCK * MOE_BLOCK
    pend = jnp.cumsum(padded)
    pstart = pend - padded
    cstart = jnp.cumsum(counts) - counts
    dest = (pick(pstart - cstart, flat_e) + rank).astype(I32)
    nblk = -(-(a + N_EXPERTS * (MOE_BLOCK - 1)) // MOE_BLOCK)
    blk_start = jnp.arange(nblk, dtype=I32) * MOE_BLOCK
    blk_e = jnp.minimum(jnp.sum((pend[None, :] <= blk_start[:, None]).astype(I32), axis=1), N_EXPERTS - 1)
    n_used = (pend[-1] // MOE_BLOCK).astype(I32).reshape(1)
    k = (blk_start - pick(pstart, blk_e))[:, None] + jnp.arange(MOE_BLOCK, dtype=I32)[None, :]
    filled = (k >= 0) & (k < pick(counts, blk_e)[:, None])
    sorted_pos = jnp.clip(pick(cstart, blk_e)[:, None] + k, 0, a - 1)
    src_tok = jnp.where(filled, order[sorted_pos] // 2, 0).astype(I32).reshape(nblk * MOE_BLOCK)
    return dest, src_tok, blk_e, n_used


def moe_layer(xs, mods, rows_per_mods, g_ffn, w_grp, b_grp, w_rt, b_rt, w_gate, w_up, w_down, layer, tms):
    n_route = N_GROUPS + N_EXPERTS
    w_route = jnp.pad(jnp.concatenate([w_grp, w_rt], axis=1), ((0, 0), (0, LANES - n_route)))
    b_route = jnp.pad(jnp.concatenate([b_grp, b_rt]), (0, LANES - n_route)).reshape(1, LANES)
    hs, routes = [], []
    for x, (sc, sh, _), rpm, tm in zip(xs, mods, rows_per_mods, tms):
        h, rt = router_call(x, g_ffn, sc, sh, w_route, b_route, tm=tm, rows_per_mod=rpm)
        hs.append(h)
        routes.append(rt)
    route = jnp.concatenate(routes) if len(routes) > 1 else routes[0]
    dest, src_tok, blk_e, n_used = moe_dispatch(route[:, :2].astype(I32))
    h_all = jnp.concatenate(hs) if len(hs) > 1 else hs[0]
    out_buf = ffn_call(blk_e, n_used, src_tok, h_all, w_gate, w_up, w_down, layer)
    outs, off = [], 0
    for x, (_, _, gt), rt, rpm, tm in zip(xs, mods, routes, rows_per_mods, tms):
        m = x.shape[0]
        outs.append(combine_call(x, out_buf, dest[2 * off:2 * (off + m)], rt, gt,
                                 tm=min(tm, MOE_BLOCK), rows_per_mod=rpm))
        off += m
    return outs


MEAN_PAGES = 16
SCORE_PAGES = 16


def _page_spec(block, layer, slot_of):
    def index_map(*args):
        ids, pages = args[:2], args[2]
        return (layer, pages[slot_of(*ids)]) + (0,) * (len(block) - 2)
    return pl.BlockSpec(block, index_map)


def _block_mean_body(pt_ref, *refs):
    pages, o_ref = refs[:MEAN_PAGES], refs[MEAN_PAGES]
    per_blk = MOBA_BLOCK // PAGE_SIZE
    for blk in range(MEAN_PAGES // per_blk):
        tot = sum(jnp.sum(pages[blk * per_blk + p][0, 0], axis=0) for p in range(per_blk))
        o_ref[0, 0, blk] = tot * (1.0 / MOBA_BLOCK)


def paged_block_mean_call(cache, layer, pt_flat, nb, n_pages):
    per_step = MEAN_PAGES * PAGE_SIZE // MOBA_BLOCK
    specs = [_page_spec((1, 1, PAGE_SIZE, N_KV_HEADS, HEAD_DIM), layer, functools.partial(
        lambda p, b, g: b * n_pages + g * MEAN_PAGES + p, p)) for p in range(MEAN_PAGES)]
    out = pl.pallas_call(
        _block_mean_body,
        grid_spec=pltpu.PrefetchScalarGridSpec(
            num_scalar_prefetch=1, grid=(nb, n_pages // MEAN_PAGES), in_specs=specs,
            out_specs=pl.BlockSpec((1, 1, per_step, N_KV_HEADS, HEAD_DIM), lambda b, g, pt: (b, g, 0, 0, 0))),
        out_shape=jax.ShapeDtypeStruct((nb, n_pages // MEAN_PAGES, per_step, N_KV_HEADS, HEAD_DIM), F32),
        compiler_params=_params(("arbitrary", "arbitrary")),
        name="paged_block_mean",
    )(pt_flat, *([cache] * MEAN_PAGES))
    return out.reshape(nb, n_pages * PAGE_SIZE // MOBA_BLOCK, N_KV_HEADS, HEAD_DIM)


def _sample_select_body(q_ref, kbar_ref, o_ref, *, own):
    q = q_ref[0]
    row = lax.broadcasted_iota(I32, (N_HEADS, LANES), 0)
    lane = lax.broadcasted_iota(I32, (N_HEADS, LANES), 1)
    gate = jnp.zeros((N_HEADS, LANES), F32)
    for kv in range(N_KV_HEADS):
        gk = lax.dot_general(q.astype(BF16), kbar_ref[0, :, kv, :].astype(BF16), (((1,), (1,)), ((), ())),
                             preferred_element_type=F32)
        gate = jnp.where(row // GQA_GROUP == kv, gk, gate)
    gate = jnp.where(lane < own, gate, NEG)
    out = jnp.zeros((N_HEADS, LANES), I32)
    for c in range(MOBA_TOPK):
        mx = jnp.max(gate, axis=-1, keepdims=True)
        idx = jnp.min(jnp.where(gate == mx, lane, LANES), axis=-1, keepdims=True)
        out = jnp.where(lane == c, idx, out)
        gate = jnp.where(lane == idx, -jnp.inf, gate)
    o_ref[0] = out


def sample_moba_select_call(q3, kbar_pad, own):
    nb = q3.shape[0]
    return pl.pallas_call(
        functools.partial(_sample_select_body, own=own),
        grid=(nb,),
        in_specs=[pl.BlockSpec((1, N_HEADS, HEAD_DIM), lambda b: (b, 0, 0)),
                  pl.BlockSpec((1, LANES, N_KV_HEADS, HEAD_DIM), lambda b: (b, 0, 0, 0))],
        out_specs=pl.BlockSpec((1, N_HEADS, LANES), lambda b: (b, 0, 0)),
        out_shape=jax.ShapeDtypeStruct((nb, N_HEADS, LANES), I32),
        compiler_params=_params(("arbitrary",)),
        name="sample_moba_select",
    )(q3, kbar_pad)


SEL_PAGES = MOBA_TOPK * (MOBA_BLOCK // PAGE_SIZE)


def _sample_moba_attn_body(pg_ref, blk_ref, q_ref, kn_ref, vn_ref, prev_ref, far_ref, own_ref, kc_ref, vc_ref,
                           o_ref, kbuf, vbuf, sems, *, own, layer):
    b = pl.program_id(0)
    per_blk = MOBA_BLOCK // PAGE_SIZE
    copies = []
    for h in range(N_HEADS):
        kv = h // GQA_GROUP
        for s in range(SEL_PAGES):
            page = pg_ref[(b * N_HEADS + h) * SEL_PAGES + s]
            rows = pl.ds(s * PAGE_SIZE, PAGE_SIZE)
            copies.append(pltpu.make_async_copy(kc_ref.at[layer, page, :, kv, :], kbuf.at[h, rows], sems.at[0]))
            copies.append(pltpu.make_async_copy(vc_ref.at[layer, page, :, kv, :], vbuf.at[h, rows], sems.at[1]))
    for cp in copies:
        cp.start()
    for cp in copies:
        cp.wait()

    for h in range(N_HEADS):
        kv = h // GQA_GROUP
        q = q_ref[0, h:h + 1, :]
        q8 = jnp.broadcast_to(q, (8, HEAD_DIM)).astype(BF16)
        s = lax.dot_general(q8, kbuf[h].astype(BF16), (((1,), (1,)), ((), ())),
                            preferred_element_type=F32)
        far = jnp.broadcast_to(far_ref[h, :, 0:1], (1, MOBA_BLOCK))
        bias = jnp.concatenate(
            [jnp.where(blk_ref[(b * N_HEADS + h) * MOBA_TOPK + n] == own - 1, prev_ref[h], far)
             for n in range(MOBA_TOPK)], axis=1)
        s = s * ATTN_SCALE + bias
        s_own = (jnp.sum(q * kn_ref[0, kv:kv + 1, :], axis=-1, keepdims=True) * ATTN_SCALE
                 + own_ref[h, :, 0:1])
        m = jnp.maximum(jnp.max(s, axis=-1, keepdims=True), s_own)
        p = jnp.exp(s - m)
        p_own = jnp.exp(s_own - m)
        l = jnp.sum(p, axis=-1, keepdims=True) + p_own
        acc = jnp.dot(p.astype(BF16), vbuf[h].astype(BF16), preferred_element_type=F32)
        o_ref[0, h:h + 1, :] = ((acc + p_own * vn_ref[0, kv:kv + 1, :]) / l)[0:1]


def sample_moba_attn_call(q3, k_new, v_new, cache_k, cache_v, layer, pages, blks, prev_row, far, own_bias, own):
    nb = q3.shape[0]
    whole = lambda shape: pl.BlockSpec(shape, lambda b, pg, bl: (0,) * len(shape))
    out = pl.pallas_call(
        functools.partial(_sample_moba_attn_body, own=own, layer=layer),
        grid_spec=pltpu.PrefetchScalarGridSpec(
            num_scalar_prefetch=2, grid=(nb,),
            in_specs=[pl.BlockSpec((1, N_HEADS, HEAD_DIM), lambda b, pg, bl: (b, 0, 0)),
                      pl.BlockSpec((1, N_KV_HEADS, HEAD_DIM), lambda b, pg, bl: (b, 0, 0)),
                      pl.BlockSpec((1, N_KV_HEADS, HEAD_DIM), lambda b, pg, bl: (b, 0, 0)),
                      whole((N_HEADS, 1, MOBA_BLOCK)), whole((N_HEADS, 1, LANES)), whole((N_HEADS, 1, LANES)),
                      pl.BlockSpec(memory_space=pl.ANY), pl.BlockSpec(memory_space=pl.ANY)],
            out_specs=pl.BlockSpec((1, N_HEADS, HEAD_DIM), lambda b, pg, bl: (b, 0, 0)),
            scratch_shapes=[pltpu.VMEM((N_HEADS, SEL_PAGES * PAGE_SIZE, HEAD_DIM), F32),
                            pltpu.VMEM((N_HEADS, SEL_PAGES * PAGE_SIZE, HEAD_DIM), F32),
                            pltpu.SemaphoreType.DMA((2,))]),
        out_shape=jax.ShapeDtypeStruct((nb, N_HEADS, HEAD_DIM), F32),
        compiler_params=_params(("arbitrary",)),
        name="sample_moba_attn",
    )(pages, blks, q3, k_new, v_new, prev_row, far, own_bias, cache_k, cache_v)
    return out.reshape(nb, Q_DIM)


def _indexer_score(qi, wi_col, keys_pad):
    s = lax.dot_general(qi.astype(BF16), keys_pad.astype(BF16), (((1,), (1,)), ((), ())),
                        preferred_element_type=F32)
    return jnp.sum(jnp.maximum(s * IDX_SCALE, 0.0) * wi_col, axis=0, keepdims=True) * IDX_W_SCALE


def _sample_score_body(pt_ref, qi_ref, wi_ref, *refs):
    pages, o_ref = refs[:SCORE_PAGES], refs[SCORE_PAGES]
    pad = jnp.zeros((PAGE_SIZE, LANES - IDX_DIM), F32)
    for p in range(SCORE_PAGES):
        keys = jnp.concatenate([pages[p][0, 0], pad], axis=1)
        o_ref[0, p:p + 1, :] = _indexer_score(qi_ref[0], wi_ref[0], keys)


def sample_dsa_score_call(cache_kidx, layer, pt_flat, qi_pad, wi_col, n_pages):
    nb = qi_pad.shape[0]
    specs = [_page_spec((1, 1, PAGE_SIZE, IDX_DIM), layer, functools.partial(
        lambda p, b, g: b * n_pages + g * SCORE_PAGES + p, p)) for p in range(SCORE_PAGES)]
    return pl.pallas_call(
        _sample_score_body,
        grid_spec=pltpu.PrefetchScalarGridSpec(
            num_scalar_prefetch=1, grid=(nb, n_pages // SCORE_PAGES),
            in_specs=[pl.BlockSpec((1, IDX_HEADS, LANES), lambda b, g, pt: (b, 0, 0)),
                      pl.BlockSpec((1, IDX_HEADS, 1), lambda b, g, pt: (b, 0, 0))] + specs,
            out_specs=pl.BlockSpec((1, SCORE_PAGES, PAGE_SIZE), lambda b, g, pt: (b, g, 0))),
        out_shape=jax.ShapeDtypeStruct((nb, n_pages, PAGE_SIZE), F32),
        compiler_params=_params(("arbitrary", "arbitrary")),
        name="sample_dsa_score",
    )(pt_flat, qi_pad, wi_col, *([cache_kidx] * SCORE_PAGES))


def _sample_dsa_select_body(sc_ref, qi_ref, wi_ref, ki_ref, m_ref, mnew_ref, *, n_sel):
    keys = _sortable_key(sc_ref[0])
    key_new = _sortable_key(_indexer_score(qi_ref[0], wi_ref[0], jnp.broadcast_to(ki_ref[0], (8, LANES))))[:, 0:1]

    def count_ge(cand):
        hit = jnp.where(keys >= cand, 1, 0).astype(I32)
        tot = jnp.sum(jnp.sum(hit, axis=0, keepdims=True), axis=1, keepdims=True)
        return tot + jnp.where(key_new >= cand, 1, 0).astype(I32)

    thr = _kth_largest_key(count_ge, n_sel, (1, 1))
    m_ref[0] = jnp.where(keys >= thr, 1.0, 0.0)
    mnew_ref[0] = jnp.broadcast_to(jnp.where(key_new >= thr, 1.0, 0.0), (1, LANES))


def sample_dsa_select_call(scores, qi_pad, wi_col, ki_pad, n_sel):
    nb, n_pages, _ = scores.shape
    return pl.pallas_call(
        functools.partial(_sample_dsa_select_body, n_sel=n_sel),
        grid=(nb,),
        in_specs=[pl.BlockSpec((1, n_pages, PAGE_SIZE), lambda b: (b, 0, 0)),
                  pl.BlockSpec((1, IDX_HEADS, LANES), lambda b: (b, 0, 0)),
                  pl.BlockSpec((1, IDX_HEADS, 1), lambda b: (b, 0, 0)),
                  pl.BlockSpec((1, 1, LANES), lambda b: (b, 0, 0))],
        out_specs=[pl.BlockSpec((1, n_pages, PAGE_SIZE), lambda b: (b, 0, 0)),
                   pl.BlockSpec((1, 1, LANES), lambda b: (b, 0, 0))],
        out_shape=[jax.ShapeDtypeStruct((nb, n_pages, PAGE_SIZE), F32), jax.ShapeDtypeStruct((nb, 1, LANES), F32)],
        compiler_params=_params(("arbitrary",)),
        name="sample_dsa_select",
    )(scores, qi_pad, wi_col, ki_pad)


def _row_copy(cache_hbm, layer, dst_vmem, row, slot, sem):
    page = lax.shift_right_logical(row, PAGE_SIZE.bit_length() - 1)
    off = row & (PAGE_SIZE - 1)
    return pltpu.make_async_copy(cache_hbm.at[layer, page, pl.ds(off, 1)], dst_vmem.at[pl.ds(slot, 1)], sem)


def _sample_dsa_attn_body(rows_ref, q_ref, valid_ref, bucket_ref, tab_ref, kn_ref, vn_ref, new_ref,
                          kc_ref, vc_ref, o_ref, kbuf, vbuf, sems, *, n_sel, layer):
    b = pl.program_id(0)

    def start(r, carry):
        row = rows_ref[b * n_sel + r]
        _row_copy(kc_ref, layer, kbuf, row, r, sems.at[0]).start()
        _row_copy(vc_ref, layer, vbuf, row, r, sems.at[1]).start()
        return carry

    def wait(r, carry):
        _row_copy(kc_ref, layer, kbuf, 0, r, sems.at[0]).wait()
        _row_copy(vc_ref, layer, vbuf, 0, r, sems.at[1]).wait()
        return carry

    lax.fori_loop(0, n_sel, start, 0)
    lax.fori_loop(0, n_sel, wait, 0)

    onehot = jnp.where(lax.broadcasted_iota(I32, (LANES, n_sel), 0) == bucket_ref[0], 1.0, 0.0)
    bias = jnp.dot(tab_ref[...], onehot, precision=lax.Precision.HIGHEST, preferred_element_type=F32)
    valid = valid_ref[0] > 0.5
    new_ok = new_ref[0, :, 0:1] > 0.5
    for kv in range(N_KV_HEADS):
        q8 = q_ref[0, kv]
        s = lax.dot_general(q8.astype(BF16), kbuf[:, kv, :].astype(BF16), (((1,), (1,)), ((), ())),
                            preferred_element_type=F32)
        bias8 = jnp.concatenate([bias[kv * GQA_GROUP:(kv + 1) * GQA_GROUP],
                                 jnp.zeros((8 - GQA_GROUP, n_sel), F32)], axis=0)
        s = jnp.where(valid, s * ATTN_SCALE + bias8, NEG)
        own_bias = jnp.concatenate([tab_ref[kv * GQA_GROUP:(kv + 1) * GQA_GROUP, 0:1],
                                    jnp.zeros((8 - GQA_GROUP, 1), F32)], axis=0)
        s_new = jnp.sum(q8 * kn_ref[0, kv:kv + 1, :], axis=-1, keepdims=True) * ATTN_SCALE + own_bias
        s_new = jnp.where(new_ok, s_new, NEG)
        m = jnp.maximum(jnp.max(s, axis=-1, keepdims=True), s_new)
        p = jnp.exp(s - m)
        p_new = jnp.exp(s_new - m)
        l = jnp.sum(p, axis=-1, keepdims=True) + p_new
        acc = jnp.dot(p.astype(BF16), vbuf[:, kv, :].astype(BF16), preferred_element_type=F32)
        o_ref[0, kv] = (acc + p_new * vn_ref[0, kv:kv + 1, :]) / l


def sample_dsa_attn_call(rows, q8, valid, bucket, table_t, k_new, v_new, new_sel, cache_k, cache_v, layer, n_sel):
    nb = q8.shape[0]
    return pl.pallas_call(
        functools.partial(_sample_dsa_attn_body, n_sel=n_sel, layer=layer),
        grid_spec=pltpu.PrefetchScalarGridSpec(
            num_scalar_prefetch=1, grid=(nb,),
            in_specs=[pl.BlockSpec((1, N_KV_HEADS, 8, HEAD_DIM), lambda b, rw: (b, 0, 0, 0)),
                      pl.BlockSpec((1, 1, n_sel), lambda b, rw: (b, 0, 0)),
                      pl.BlockSpec((1, 1, n_sel), lambda b, rw: (b, 0, 0)),
                      pl.BlockSpec((N_HEADS, LANES), lambda b, rw: (0, 0)),
                      pl.BlockSpec((1, N_KV_HEADS, HEAD_DIM), lambda b, rw: (b, 0, 0)),
                      pl.BlockSpec((1, N_KV_HEADS, HEAD_DIM), lambda b, rw: (b, 0, 0)),
                      pl.BlockSpec((1, 1, LANES), lambda b, rw: (b, 0, 0)),
                      pl.BlockSpec(memory_space=pl.ANY),
                      pl.BlockSpec(memory_space=pl.ANY)],
            out_specs=pl.BlockSpec((1, N_KV_HEADS, 8, HEAD_DIM), lambda b, rw: (b, 0, 0, 0)),
            scratch_shapes=[pltpu.VMEM((n_sel, N_KV_HEADS, HEAD_DIM), F32),
                            pltpu.VMEM((n_sel, N_KV_HEADS, HEAD_DIM), F32),
                            pltpu.SemaphoreType.DMA((2,))]),
        out_shape=jax.ShapeDtypeStruct((nb, N_KV_HEADS, 8, HEAD_DIM), F32),
        compiler_params=_params(("arbitrary",)),
        name="sample_dsa_attn",
    )(rows, q8, valid, bucket, table_t, k_new, v_new, new_sel, cache_k, cache_v)


def _rel_bucket(dist):
    n = jnp.maximum(dist, 0)
    exact = REL_BUCKETS // 2
    logr = jnp.log(jnp.maximum(n, 1).astype(F32) / exact) / math.log(REL_MAX_DIST / exact)
    large = jnp.minimum(exact + (logr * (REL_BUCKETS - exact)).astype(I32), REL_BUCKETS - 1)
    return jnp.where(n < exact, n, large)


def _head_gain(qn, kn, n_cols):
    return jnp.concatenate([jnp.tile(qn, N_HEADS), jnp.tile(kn, N_KV_HEADS),
                            jnp.ones((n_cols - Q_DIM - KV_DIM,), F32)]).reshape(1, n_cols)


def _project(x2d, g, sc, sh, w, layer, qn, kn, *, tm, rows_per_mod):
    n_w = w.shape[2]
    n_main = IN_DIM_A if n_w == IN_DIM_A else IN_DIM_A + IDX_Q_DIM
    z, k_out, v_out = norm_proj_call(x2d, g, sc, sh, w, layer, _head_gain(qn, kn, n_main), n_cols=n_main,
                                     n_norm_cols=Q_DIM + KV_DIM, tm=tm, rows_per_mod=rows_per_mod, kv_out=True)
    if n_w == IN_DIM_A:
        return z, k_out, v_out, None
    w_tail = jnp.pad(w[layer, :, n_main:], ((0, 0), (0, LANES - (n_w - n_main))))[None]
    tail, = norm_proj_call(x2d, g, sc, sh, w_tail, 0, jnp.ones((1, LANES), F32), n_cols=LANES, n_norm_cols=0,
                           tm=tm, rows_per_mod=rows_per_mod, tn=LANES)
    return z, k_out, v_out, tail


def moba_sample_attention(zs, k_new, v_new, cache_k, cache_v, layer, page_table, prev_row, far, own_bias):
    nb, n_pages = page_table.shape
    own = n_pages * PAGE_SIZE // MOBA_BLOCK
    kbar = paged_block_mean_call(cache_k, layer, page_table.reshape(-1), nb, n_pages)
    kbar_pad = jnp.pad(kbar, ((0, 0), (0, LANES - own), (0, 0), (0, 0)))
    q3 = zs[:, :Q_DIM].reshape(nb, N_HEADS, HEAD_DIM)
    sel = sample_moba_select_call(q3, kbar_pad, own)[:, :, :MOBA_TOPK]
    per_blk = MOBA_BLOCK // PAGE_SIZE
    logical = sel[..., None] * per_blk + jnp.arange(per_blk)
    pages = page_table[jnp.arange(nb)[:, None, None, None], logical]
    return sample_moba_attn_call(q3, k_new, v_new, cache_k, cache_v, layer, pages.reshape(-1), sel.reshape(-1),
                                 prev_row, far, own_bias, own)


def dsa_sample_attention(zs, tail_s, k_new, v_new, cache_k, cache_v, cache_kidx, layer, page_table, table_t):
    nb, n_pages = page_table.shape
    past = n_pages * PAGE_SIZE
    n_sel = min(DSA_TOPK, (past + 1) // 4)
    qi_pad = jnp.pad(zs[:, IN_DIM_A:].reshape(nb, IDX_HEADS, IDX_DIM), ((0, 0), (0, 0), (0, LANES - IDX_DIM)))
    wi_col = tail_s[:, IDX_DIM:IDX_DIM + IDX_HEADS].reshape(nb, IDX_HEADS, 1)
    ki_pad = jnp.pad(tail_s[:, :IDX_DIM], ((0, 0), (0, LANES - IDX_DIM))).reshape(nb, 1, LANES)
    scores = sample_dsa_score_call(cache_kidx, layer, page_table.reshape(-1), qi_pad, wi_col, n_pages)
    picked, new_sel = sample_dsa_select_call(scores, qi_pad, wi_col, ki_pad, n_sel)
    hit = (picked > 0.5).astype(I32)
    in_page = jnp.cumsum(hit, axis=2)
    page_tot = in_page[:, :, -1]
    running = (in_page + (jnp.cumsum(page_tot, axis=1) - page_tot)[:, :, None]).reshape(nb, past)
    slots = jnp.arange(1, n_sel + 1, dtype=I32)
    idx = jnp.sum((running[:, None, :] < slots[None, :, None]).astype(I32), axis=-1)
    pos = jnp.minimum(idx, past - 1)
    rows = page_table[jnp.arange(nb)[:, None], pos // PAGE_SIZE] * PAGE_SIZE + pos % PAGE_SIZE
    valid = (idx < past).astype(F32)
    bucket = _rel_bucket(past - pos).astype(I32)
    q8 = jnp.pad(zs[:, :Q_DIM].reshape(nb, N_KV_HEADS, GQA_GROUP, HEAD_DIM),
                 ((0, 0), (0, 0), (0, 8 - GQA_GROUP), (0, 0)))
    o8 = sample_dsa_attn_call(rows.reshape(-1), q8, valid[:, None], bucket[:, None], table_t,
                              k_new, v_new, new_sel, cache_k, cache_v, layer, n_sel)
    return o8[:, :, :GQA_GROUP].reshape(nb, Q_DIM)


def kernel(x_prompt, x_sample, cache_k_a, cache_v_a, cache_k_b, cache_v_b, cache_kidx_b, page_table, c_prompt, c_sample, rel_table, ada_w, ada_b, norm_attn, norm_ffn, q_norm, k_norm, w_in_a, w_in_b, w_o, w_grp, b_grp, w_rt, b_rt, w_gate, w_up, w_down):
    b, s, d = x_prompt.shape
    nb = x_sample.shape[0]
    depth = ada_w.shape[0]
    n_pages = page_table.shape[1]
    past = n_pages * PAGE_SIZE
    own = past // MOBA_BLOCK
    assert x_sample.shape[1] == 1 and past % MOBA_BLOCK == 0 and MOBA_TOPK <= own <= LANES
    assert s % ROW_TILE == 0 and nb % 8 == 0

    n_mod_rows = -(-(b + nb) // 8) * 8
    c_all = jnp.concatenate([c_prompt, c_sample, jnp.zeros((n_mod_rows - b - nb, d), F32)])
    mod = adaln_call(c_all, ada_w, ada_b)

    bias = bias_tiles_call(rel_table)
    lane_rep = lambda v: jnp.broadcast_to(v[:, None, None], (N_HEADS, 1, LANES))
    far = lane_rep(rel_table[REL_BUCKETS - 1])
    far2 = far * LOG2E
    own_bias = lane_rep(rel_table[0])
    prev_row = rel_table[_rel_bucket(MOBA_BLOCK - jnp.arange(MOBA_BLOCK))].T[:, None, :]
    table_t = jnp.pad(rel_table.T, ((0, 0), (0, LANES - REL_BUCKETS)))

    xp = x_prompt.reshape(b * s, d)
    xs = x_sample.reshape(nb, d)
    kv_p = {"ka": [], "va": [], "kb": [], "vb": [], "ib": []}
    kv_s = {"ka": [], "va": [], "kb": [], "vb": [], "ib": []}

    for i in range(depth):
        j = i // 2
        mp = mod[i, :b].reshape(b, 1, 6, d)
        ms = mod[i, b:b + nb].reshape(1, nb, 6, d)
        sh1, sc1, gt1, sh2, sc2, gt2 = (mp[:, :, k] for k in range(6))
        sh1s, sc1s, gt1s, sh2s, sc2s, gt2s = (ms[:, :, k] for k in range(6))
        moba = i % 2 == 0
        w_in = w_in_a if moba else w_in_b
        z, k_p, v_p, tail = _project(xp, norm_attn[i], sc1, sh1, w_in, j, q_norm[i], k_norm[i],
                                     tm=ROW_TILE, rows_per_mod=s)
        zs, k_s, v_s, tail_s = _project(xs, norm_attn[i], sc1s, sh1s, w_in, j, q_norm[i], k_norm[i],
                                        tm=nb, rows_per_mod=nb)
        z3 = z.reshape(b, s, -1)
        tag = "a" if moba else "b"
        kv_p["k" + tag].append(k_p.reshape(b, s, N_KV_HEADS, HEAD_DIM))
        kv_p["v" + tag].append(v_p.reshape(b, s, N_KV_HEADS, HEAD_DIM))
        kv_s["k" + tag].append(k_s.reshape(nb, 1, N_KV_HEADS, HEAD_DIM))
        kv_s["v" + tag].append(v_s.reshape(nb, 1, N_KV_HEADS, HEAD_DIM))

        if moba:
            sel = moba_select_call(z3, s // MOBA_BLOCK)
            o = attn_call(z3, sel, jnp.zeros((b, 1, s), I32), bias, far2, dsa=False)

            o_s = moba_sample_attention(zs, k_s, v_s, cache_k_a, cache_v_a, j, page_table, prev_row, far, own_bias)
        else:
            tail3 = tail.reshape(b, s, LANES)
            kv_p["ib"].append(tail3[:, :, :IDX_DIM])
            kv_s["ib"].append(tail_s[:, None, :IDX_DIM])
            wi_t = jnp.swapaxes(tail3[:, :, IDX_DIM:IDX_DIM + IDX_HEADS], 1, 2) * (IDX_SCALE * IDX_W_SCALE)
            key_t, thr = dsa_score_call(z3, tail3, wi_t, min(DSA_TOPK, s // 4))
            o = attn_call(z3, key_t, thr, bias, far2, dsa=True)

            o_s = dsa_sample_attention(zs, tail_s, k_s, v_s, cache_k_b, cache_v_b, cache_kidx_b, j, page_table,
                                       table_t)

        xp = linear_residual_call(o.reshape(b * s, Q_DIM), w_o, i, xp, gt1, tm=ROW_TILE, rows_per_mod=s)
        xs = linear_residual_call(o_s.astype(BF16), w_o, i, xs, gt1s, tm=nb, rows_per_mod=nb)
        xp, xs = moe_layer([xp, xs], [(sc2, sh2, gt2), (sc2s, sh2s, gt2s)], [s, nb], norm_ffn[i],
                           w_grp[i], b_grp[i], w_rt[i], b_rt[i], w_gate, w_up, w_down, i, tms=[512, nb])

    stack = lambda lst: jnp.stack(lst)
    return (xp.reshape(b, s, d), xs.reshape(nb, 1, d),
            stack(kv_p["ka"]), stack(kv_p["va"]), stack(kv_p["kb"]), stack(kv_p["vb"]), stack(kv_p["ib"]),
            stack(kv_s["ka"]), stack(kv_s["va"]), stack(kv_s["kb"]), stack(kv_s["vb"]), stack(kv_s["ib"]))
```
